```python
import jax, jax.numpy as jnp
from jax import lax
import numpy as np

D_MODEL = 2048
BATCH = 4
SEQ = 2048
DEPTH = 4

CHUNK = 64
GDN_HEAD_DIM = 128
GDN_HEADS = D_MODEL // 256
GDN_WIDTH = GDN_HEADS * GDN_HEAD_DIM
CONV_WIDTH = 4
GMLP_WIDTH = D_MODEL // 2
GMLP_GROUPS = 8
GMLP_GROUP_DIM = GMLP_WIDTH // GMLP_GROUPS
GMLP_BLOCK = 128
SBA_HEAD_DIM = 128
SBA_HEADS = D_MODEL // 256
SBA_WIDTH = SBA_HEADS * SBA_HEAD_DIM
QUERY_BLOCK = 128
N_BRANCHES = 3
D_FF = 4 * D_MODEL
EPS = 1e-6
PROJ_SIZES = (3 * GDN_WIDTH,
              GDN_HEADS,
              GDN_HEADS,
              GDN_WIDTH,
              2 * GMLP_WIDTH,
              3 * SBA_WIDTH,
              N_BRANCHES * D_MODEL)
D_IN = sum(PROJ_SIZES)

kernel_name = "hybrid_gdn_gmlp_stickbreak_trunk"


def rms_norm(x, gain):
    xf = x.astype(jnp.float32)
    y = xf * lax.rsqrt(jnp.mean(xf * xf, axis=-1, keepdims=True) + EPS)
    return (y * gain.astype(jnp.float32)).astype(x.dtype)


def layer_norm(x, gain):
    xf = x.astype(jnp.float32)
    mu = jnp.mean(xf, axis=-1, keepdims=True)
    xc = xf - mu
    y = xc * lax.rsqrt(jnp.mean(xc * xc, axis=-1, keepdims=True) + EPS)
    return (y * gain.astype(jnp.float32)).astype(x.dtype)


def l2_norm(x):
    return x * lax.rsqrt(jnp.sum(x * x, axis=-1, keepdims=True) + EPS)


def causal_depthwise_conv(x, w):
    K, C = w.shape
    return lax.conv_general_dilated(x, w[:, None, :].astype(x.dtype), window_strides=(1,),
                                    padding=[(K - 1, 0)],
                                    dimension_numbers=('NWC', 'WIO', 'NWC'),
                                    feature_group_count=C)


def gated_delta_rule_chunked(q, k, v, g, beta):
    out_dtype = v.dtype
    B, H, T, dk = q.shape
    dv = v.shape[-1]
    N = T // CHUNK
    f32 = jnp.float32
    q = q.astype(f32).reshape(B, H, N, CHUNK, dk)
    k = k.astype(f32).reshape(B, H, N, CHUNK, dk)
    v = v.astype(f32).reshape(B, H, N, CHUNK, dv)
    g = jnp.cumsum(g.astype(f32).reshape(B, H, N, CHUNK), axis=-1)
    beta = beta.astype(f32).reshape(B, H, N, CHUNK)
    incl = jnp.tril(jnp.ones((CHUNK, CHUNK), dtype=bool))
    strict = jnp.tril(jnp.ones((CHUNK, CHUNK), dtype=bool), -1)
    decay = jnp.exp(jnp.where(incl, g[..., :, None] - g[..., None, :], -jnp.inf))
    kb = k * beta[..., None]
    L = jnp.where(strict, jnp.einsum('bhncd,bhnsd->bhncs', kb, k) * decay, 0.0)
    eye = jnp.eye(CHUNK, dtype=f32)
    rhs = jnp.concatenate([v * beta[..., None], kb * jnp.exp(g)[..., None]], axis=-1)
    sol = lax.linalg.triangular_solve(L + eye, rhs, left_side=True, lower=True,
                                      unit_diagonal=True)
    u, w = sol[..., :dv], sol[..., dv:]
    intra = jnp.einsum('bhncd,bhnsd->bhncs', q, k) * decay
    q_dec = q * jnp.exp(g)[..., None]
    g_last = g[..., -1]
    k_dec = k * jnp.exp(g_last[..., None] - g)[..., None]
    xs = tuple(jnp.moveaxis(t, 2, 0) for t in (u, w, intra, q_dec, k_dec, g_last))

    def step(S, inp):
        u_n, w_n, a_n, qd_n, kd_n, gl_n = inp
        v_new = u_n - jnp.einsum('bhck,bhkv->bhcv', w_n, S)
        o_n = jnp.einsum('bhck,bhkv->bhcv', qd_n, S) + jnp.einsum('bhcs,bhsv->bhcv', a_n, v_new)
        S = S * jnp.exp(gl_n)[..., None, None] + jnp.einsum('bhck,bhcv->bhkv', kd_n, v_new)
        return S, o_n

    S0 = jnp.zeros((B, H, dk, dv), f32)
    _, o = lax.scan(step, S0, xs)
    return jnp.moveaxis(o, 0, 2).reshape(B, H, T, dv).astype(out_dtype)


def stick_breaking_attention(q, k, v):
    B, H, T, d = q.shape
    nb = T // QUERY_BLOCK
    qb = jnp.moveaxis(q.reshape(B, H, nb, QUERY_BLOCK, d), 2, 0)
    kpos = jnp.arange(T)
    scale = d ** -0.5

    def block(args):
        q_blk, i = args
        z = jnp.einsum('bhqd,bhkd->bhqk', q_blk, k).astype(jnp.float32) * scale
        qpos = i * QUERY_BLOCK + jnp.arange(QUERY_BLOCK)
        strict = kpos[None, :] < qpos[:, None]
        log_keep = jnp.where(strict, jax.nn.log_sigmoid(-z), 0.0)
        suffix = lax.cumsum(log_keep, axis=3, reverse=True) - log_keep
        A = jnp.where(strict, jnp.exp(jax.nn.log_sigmoid(z) + suffix), 0.0)
        return jnp.einsum('bhqk,bhkd->bhqd', A.astype(v.dtype), v)

    out = lax.map(block, (qb, jnp.arange(nb)))
    return jnp.moveaxis(out, 0, 2).reshape(B, H, T, d)


def hybrid_mixer(h, w_in, conv_w, a_log, dt_bias, gdn_norm_g, gmlp_ln_g, w_spatial, b_spatial,
                 sba_q_g, sba_k_g, w_out_a, w_out_b, w_out_c, w_out):
    B, T, _ = h.shape
    z = h @ w_in
    split_idx = np.cumsum(PROJ_SIZES)[:-1].tolist()
    gdn_qkv, gdn_a, gdn_b, gdn_gate, gmlp_uv, sba_qkv, gate_logits = jnp.split(z, split_idx, axis=-1)

    qkv = jax.nn.silu(causal_depthwise_conv(gdn_qkv, conv_w))
    qa, ka, va = jnp.split(qkv, 3, axis=-1)
    to_heads = lambda t, H, d: jnp.transpose(t.reshape(B, T, H, d), (0, 2, 1, 3))
    qa = l2_norm(to_heads(qa, GDN_HEADS, GDN_HEAD_DIM).astype(jnp.float32)) * GDN_HEAD_DIM ** -0.5
    ka = l2_norm(to_heads(ka, GDN_HEADS, GDN_HEAD_DIM).astype(jnp.float32))
    va = to_heads(va, GDN_HEADS, GDN_HEAD_DIM)
    beta = jnp.transpose(jax.nn.sigmoid(gdn_b.astype(jnp.float32)), (0, 2, 1))
    g = -jnp.exp(a_log.astype(jnp.float32)) * jax.nn.softplus(
        gdn_a.astype(jnp.float32) + dt_bias.astype(jnp.float32))
    g = jnp.transpose(g, (0, 2, 1))
    oa = gated_delta_rule_chunked(qa, ka, va, g, beta)
    oa = jnp.transpose(oa, (0, 2, 1, 3))
    oa = rms_norm(oa, gdn_norm_g) * jax.nn.silu(gdn_gate.reshape(B, T, GDN_HEADS, GDN_HEAD_DIM))
    branch_a = oa.reshape(B, T, GDN_WIDTH) @ w_out_a

    uv = jax.nn.gelu(gmlp_uv, approximate=False)
    u, vb = jnp.split(uv, 2, axis=-1)
    vb = layer_norm(vb, gmlp_ln_g).reshape(B, T // GMLP_BLOCK, GMLP_BLOCK, GMLP_GROUPS, GMLP_GROUP_DIM)
    pos = jnp.arange(GMLP_BLOCK) // CHUNK
    chunk_causal = pos[None, :] <= pos[:, None]
    ws = jnp.where(chunk_causal[None], w_spatial, 0.0).astype(vb.dtype)
    s = jnp.einsum('gts,bnsgc->bntgc', ws, vb) + jnp.transpose(b_spatial)[None, None, :, :, None]
    branch_b = (u * s.reshape(B, T, GMLP_WIDTH)) @ w_out_b

    qc, kc, vc = jnp.split(sba_qkv, 3, axis=-1)
    qc = to_heads(rms_norm(qc.reshape(B, T, SBA_HEADS, SBA_HEAD_DIM), sba_q_g).reshape(B, T, SBA_WIDTH), SBA_HEADS, SBA_HEAD_DIM)
    kc = to_heads(rms_norm(kc.reshape(B, T, SBA_HEADS, SBA_HEAD_DIM), sba_k_g).reshape(B, T, SBA_WIDTH), SBA_HEADS, SBA_HEAD_DIM)
    vc = to_heads(vc, SBA_HEADS, SBA_HEAD_DIM)
    oc = stick_breaking_attention(qc, kc, vc)
    branch_c = jnp.transpose(oc, (0, 2, 1, 3)).reshape(B, T, SBA_WIDTH) @ w_out_c

    gates = jax.nn.sigmoid(gate_logits).reshape(B, T, N_BRANCHES, D_MODEL)
    y = gates[:, :, 0] * branch_a + gates[:, :, 1] * branch_b + gates[:, :, 2] * branch_c
    return y @ w_out


def setup_inputs(seed: int = 0) -> dict:
    key = jax.random.key(seed)
    ks = jax.random.split(key, 20)
    nrm = lambda k, shape, scale: jax.random.normal(k, shape, jnp.float32) * scale
    gain = lambda k, shape: 1.0 + 0.02 * jax.random.normal(k, shape, jnp.float32)
    dt = jnp.exp(jax.random.uniform(ks[4], (DEPTH, GDN_HEADS), jnp.float32,
                                    np.log(1e-3).astype(np.float32), np.log(1e-1).astype(np.float32)))
    return {
        "x": nrm(ks[0], (BATCH, SEQ, D_MODEL), 1.0),
        "w_in": nrm(ks[1], (DEPTH, D_MODEL, D_IN), D_MODEL ** -0.5),
        "conv_w": nrm(ks[2], (DEPTH, CONV_WIDTH, 3 * GDN_WIDTH), CONV_WIDTH ** -0.5),
        "a_log": jnp.log(jax.random.uniform(ks[3], (DEPTH, GDN_HEADS), jnp.float32, 1.0, 16.0)),
        "dt_bias": dt + jnp.log(-jnp.expm1(-dt)),
        "gdn_norm_g": gain(ks[5], (DEPTH, GDN_HEAD_DIM)),
        "gmlp_ln_g": gain(ks[6], (DEPTH, GMLP_WIDTH)),
        "w_spatial": nrm(ks[7], (DEPTH, GMLP_GROUPS, GMLP_BLOCK, GMLP_BLOCK), GMLP_BLOCK ** -0.5),
        "b_spatial": gain(ks[8], (DEPTH, GMLP_GROUPS, GMLP_BLOCK)),
        "sba_q_g": gain(ks[9], (DEPTH, SBA_HEAD_DIM)),
        "sba_k_g": gain(ks[10], (DEPTH, SBA_HEAD_DIM)),
        "w_out_a": nrm(ks[11], (DEPTH, GDN_WIDTH, D_MODEL), GDN_WIDTH ** -0.5),
        "w_out_b": nrm(ks[12], (DEPTH, GMLP_WIDTH, D_MODEL), GMLP_WIDTH ** -0.5),
        "w_out_c": nrm(ks[13], (DEPTH, SBA_WIDTH, D_MODEL), SBA_WIDTH ** -0.5),
        "w_out": nrm(ks[14], (DEPTH, D_MODEL, D_MODEL), D_MODEL ** -0.5),
        "norm_mix_g": gain(ks[15], (DEPTH, D_MODEL)),
        "norm_mlp_g": gain(ks[16], (DEPTH, D_MODEL)),
        "w_ff1": nrm(ks[17], (DEPTH, D_MODEL, D_FF), D_MODEL ** -0.5),
        "w_ff2": nrm(ks[18], (DEPTH, D_FF, D_MODEL), D_FF ** -0.5),
    }


def reference(x, w_in, conv_w, a_log, dt_bias, gdn_norm_g, gmlp_ln_g, w_spatial, b_spatial,
              sba_q_g, sba_k_g, w_out_a, w_out_b, w_out_c, w_out, norm_mix_g, norm_mlp_g,
              w_ff1, w_ff2):
    for l in range(DEPTH):
        h = rms_norm(x, norm_mix_g[l])
        x = x + hybrid_mixer(h, w_in[l], conv_w[l], a_log[l], dt_bias[l], gdn_norm_g[l],
                             gmlp_ln_g[l], w_spatial[l], b_spatial[l], sba_q_g[l], sba_k_g[l],
                             w_out_a[l], w_out_b[l], w_out_c[l], w_out[l])
        h = rms_norm(x, norm_mlp_g[l])
        x = x + jnp.square(jax.nn.relu(h @ w_ff1[l])) @ w_ff2[l]
    return x
```

```python
import functools

import jax
import jax.numpy as jnp
import numpy as np
from jax import lax
from jax.experimental import pallas as pl
from jax.experimental.pallas import tpu as pltpu

f32 = jnp.float32
bf16 = jnp.bfloat16

D_MODEL = 2048
CHUNK = 64
HEAD_DIM = 128
N_HEADS = 8
WIDTH = N_HEADS * HEAD_DIM
CONV_WIDTH = 4
GMLP_WIDTH = 1024
GMLP_GROUPS = 8
GMLP_BLOCK = 128
N_BRANCHES = 3
EPS = 1e-6

LANES = 128
SUBLANES = 8
VMEM_LIMIT_BYTES = 60000 * 1024

COL_GDN_QKV = 0
COL_GDN_GATE = 3 * WIDTH
COL_GMLP_UV = COL_GDN_GATE + WIDTH
COL_SBA_QKV = COL_GMLP_UV + 2 * GMLP_WIDTH
COL_GATES = COL_SBA_QKV + 3 * WIDTH
N_MAIN = COL_GATES + N_BRANCHES * D_MODEL

TM = 1024
TN = 1024
TK = 2048

GDN_GROUP = 256
SBA_TQ = 128
SBA_TK = 512
SBA_SUB = 128
GMLP_ROWS = 512

_HI = lax.Precision.HIGHEST


def _params(*sem):
    return pltpu.CompilerParams(dimension_semantics=sem, vmem_limit_bytes=VMEM_LIMIT_BYTES)


def _dot(a, b, precision=None):
    return jnp.dot(a, b, preferred_element_type=f32, precision=precision)


def _dot_nt(a, b, precision=None):
    return lax.dot_general(a, b, (((1,), (1,)), ((), ())), preferred_element_type=f32,
                           precision=precision)


def _dot_tn(a, b):
    return lax.dot_general(a, b, (((0,), (0,)), ((), ())), preferred_element_type=f32)


def _sigmoid(x):
    return 1.0 / (1.0 + jnp.exp(-x))


def _silu(x):
    return x * _sigmoid(x)


def _softplus(x):
    return jnp.maximum(x, 0.0) + jnp.log1p(jnp.exp(-jnp.abs(x)))


def _block_id(idx, size):
    return lax.shift_right_logical(idx, int(size).bit_length() - 1)


def _rms_rows(x, gain):
    ms = jnp.mean(x * x, axis=-1, keepdims=True)
    return x * lax.rsqrt(ms + EPS) * gain


def _inproj_kernel(x_ref, g_ref, w_ref, wab_ref, alog_ref, dt_ref, z_ref, gb_ref, hn_ref):
    @pl.when(pl.program_id(1) == 0)
    def _():
        hn = _rms_rows(x_ref[...], g_ref[...]).astype(bf16)
        hn_ref[...] = hn
        ab = _dot(hn, wab_ref[...])
        lane = lax.broadcasted_iota(jnp.int32, ab.shape, 1)
        g = -jnp.exp(alog_ref[...]) * _softplus(ab + dt_ref[...])
        gb_ref[...] = jnp.where(lane < N_HEADS, g, _sigmoid(ab))

    z_ref[...] = _dot(hn_ref[...], w_ref[...]).astype(z_ref.dtype)


def _inproj(x, gain, w_main, w_ab, alog_row, dt_row):
    m = x.shape[0]
    return pl.pallas_call(
        _inproj_kernel,
        grid=(m // TM, N_MAIN // TN),
        in_specs=[
            pl.BlockSpec((TM, D_MODEL), lambda i, j: (i, 0)),
            pl.BlockSpec((1, D_MODEL), lambda i, j: (0, 0)),
            pl.BlockSpec((D_MODEL, TN), lambda i, j: (0, j)),
            pl.BlockSpec((D_MODEL, LANES), lambda i, j: (0, 0)),
            pl.BlockSpec((1, LANES), lambda i, j: (0, 0)),
            pl.BlockSpec((1, LANES), lambda i, j: (0, 0)),
        ],
        out_specs=[
            pl.BlockSpec((TM, TN), lambda i, j: (i, j)),
            pl.BlockSpec((TM, LANES), lambda i, j: (i, 0)),
        ],
        out_shape=[
            jax.ShapeDtypeStruct((m, N_MAIN), bf16),
            jax.ShapeDtypeStruct((m, LANES), f32),
        ],
        scratch_shapes=[pltpu.VMEM((TM, D_MODEL), bf16)],
        compiler_params=_params("arbitrary", "arbitrary"),
        name="inproj",
    )(x, gain, w_main, w_ab, alog_row, dt_row)


def _ffn_up_kernel(x_ref, g_ref, w_ref, h_ref, hn_ref):
    @pl.when(pl.program_id(1) == 0)
    def _():
        hn_ref[...] = _rms_rows(x_ref[...], g_ref[...]).astype(bf16)

    a = jnp.maximum(_dot(hn_ref[...], w_ref[...]), 0.0)
    h_ref[...] = (a * a).astype(h_ref.dtype)


def _ffn_up(x, gain, w1):
    m, n = x.shape[0], w1.shape[1]
    return pl.pallas_call(
        _ffn_up_kernel,
        grid=(m // TM, n // TN),
        in_specs=[
            pl.BlockSpec((TM, D_MODEL), lambda i, j: (i, 0)),
            pl.BlockSpec((1, D_MODEL), lambda i, j: (0, 0)),
            pl.BlockSpec((D_MODEL, TN), lambda i, j: (0, j)),
        ],
        out_specs=pl.BlockSpec((TM, TN), lambda i, j: (i, j)),
        out_shape=jax.ShapeDtypeStruct((m, n), bf16),
        scratch_shapes=[pltpu.VMEM((TM, D_MODEL), bf16)],
        compiler_params=_params("arbitrary", "arbitrary"),
        name="ffn_up",
    )(x, gain, w1)


def _matmul_res_kernel(a_ref, w_ref, r_ref, o_ref):
    d = _dot(a_ref[...], w_ref[...])

    @pl.when(pl.program_id(2) == 0)
    def _():
        o_ref[...] = r_ref[...] + d

    @pl.when(pl.program_id(2) != 0)
    def _():
        o_ref[...] += d


def _matmul_res(a, w, res):
    m, k = a.shape
    n = w.shape[1]
    return pl.pallas_call(
        _matmul_res_kernel,
        grid=(m // TM, n // TN, k // TK),
        in_specs=[
            pl.BlockSpec((TM, TK), lambda i, j, kk: (i, kk)),
            pl.BlockSpec((TK, TN), lambda i, j, kk: (kk, j)),
            pl.BlockSpec((TM, TN), lambda i, j, kk: (i, j)),
        ],
        out_specs=pl.BlockSpec((TM, TN), lambda i, j, kk: (i, j)),
        out_shape=jax.ShapeDtypeStruct((m, n), f32),
        compiler_params=_params("arbitrary", "arbitrary", "arbitrary"),
        name="matmul_res",
    )(a, w, res)


MERGE_TN = 512


def _merge_kernel(oa_ref, ob_ref, oc_ref, wa_ref, wb_ref, wc_ref, ga_ref, gb_ref, gc_ref, y_ref):
    y = _sigmoid(ga_ref[...].astype(f32)) * _dot(oa_ref[...], wa_ref[...])
    y += _sigmoid(gb_ref[...].astype(f32)) * _dot(ob_ref[...], wb_ref[...])
    y += _sigmoid(gc_ref[...].astype(f32)) * _dot(oc_ref[...], wc_ref[...])
    y_ref[...] = y.astype(y_ref.dtype)


def _merge(oa, ob, oc, wa, wb, wc, z):
    m = oa.shape[0]
    tn = MERGE_TN
    per_branch = D_MODEL // tn
    gate0 = COL_GATES // tn
    branch_in = pl.BlockSpec((TM, WIDTH), lambda i, j: (i, 0))
    branch_w = pl.BlockSpec((WIDTH, tn), lambda i, j: (0, j))

    def gate_spec(b):
        return pl.BlockSpec((TM, tn), lambda i, j: (i, gate0 + b * per_branch + j))

    return pl.pallas_call(
        _merge_kernel,
        grid=(m // TM, D_MODEL // tn),
        in_specs=[branch_in, branch_in, branch_in, branch_w, branch_w, branch_w,
                  gate_spec(0), gate_spec(1), gate_spec(2)],
        out_specs=pl.BlockSpec((TM, tn), lambda i, j: (i, j)),
        out_shape=jax.ShapeDtypeStruct((m, D_MODEL), bf16),
        compiler_params=_params("arbitrary", "arbitrary"),
        name="merge",
    )(oa, ob, oc, wa, wb, wc, z, z, z)


def _gelu(x):
    return 0.5 * x * (1.0 + lax.erf(x * np.float32(np.sqrt(0.5))))


def _gmlp_kernel(zu_ref, zv_ref, g_ref, ws_ref, bs_ref, o_ref):
    u = _gelu(zu_ref[...].astype(f32))
    v = _gelu(zv_ref[...].astype(f32))
    mu = jnp.mean(v, axis=-1, keepdims=True)
    vc = v - mu
    var = jnp.mean(vc * vc, axis=-1, keepdims=True)
    vb = (vc * lax.rsqrt(var + EPS) * g_ref[...]).astype(bf16)

    t_chunk = _block_id(lax.broadcasted_iota(jnp.int32, (GMLP_BLOCK, GMLP_BLOCK), 0), CHUNK)
    s_chunk = _block_id(lax.broadcasted_iota(jnp.int32, (GMLP_BLOCK, GMLP_BLOCK), 1), CHUNK)
    causal = s_chunk <= t_chunk
    gdim = GMLP_WIDTH // GMLP_GROUPS
    for g in range(GMLP_GROUPS):
        ws = jnp.where(causal, ws_ref[g], 0.0).astype(bf16)
        cols = slice(g * gdim, (g + 1) * gdim)
        for blk in range(GMLP_ROWS // GMLP_BLOCK):
            rows = slice(blk * GMLP_BLOCK, (blk + 1) * GMLP_BLOCK)
            s = _dot(ws, vb[rows, cols]) + bs_ref[g]
            o_ref[rows, cols] = (u[rows, cols] * s).astype(o_ref.dtype)


def _gmlp(z, ln_gain, w_spatial, b_rep):
    m = z.shape[0]
    cu = COL_GMLP_UV // GMLP_WIDTH
    return pl.pallas_call(
        _gmlp_kernel,
        grid=(m // GMLP_ROWS,),
        in_specs=[
            pl.BlockSpec((GMLP_ROWS, GMLP_WIDTH), lambda i: (i, cu)),
            pl.BlockSpec((GMLP_ROWS, GMLP_WIDTH), lambda i: (i, cu + 1)),
            pl.BlockSpec((1, GMLP_WIDTH), lambda i: (0, 0)),
            pl.BlockSpec((GMLP_GROUPS, GMLP_BLOCK, GMLP_BLOCK), lambda i: (0, 0, 0)),
            pl.BlockSpec((GMLP_GROUPS, GMLP_BLOCK, LANES), lambda i: (0, 0, 0)),
        ],
        out_specs=pl.BlockSpec((GMLP_ROWS, GMLP_WIDTH), lambda i: (i, 0)),
        out_shape=jax.ShapeDtypeStruct((m, GMLP_WIDTH), bf16),
        compiler_params=_params("arbitrary"),
        name="gmlp",
    )(z, z, ln_gain, w_spatial, b_rep)


def _sba_kernel(q_ref, k_ref, v_ref, qg_ref, kg_ref, o_ref, qn_ref, kn_ref):
    t = q_ref.shape[0]
    qn_ref[...] = _rms_rows(q_ref[...].astype(f32), qg_ref[...]).astype(bf16)
    kn_ref[...] = _rms_rows(k_ref[...].astype(f32), kg_ref[...]).astype(bf16)
    scale = np.float32(HEAD_DIM ** -0.5)
    nsub = SBA_TK // SBA_SUB
    upper = (lax.broadcasted_iota(jnp.int32, (SBA_SUB, SBA_SUB), 0)
             > lax.broadcasted_iota(jnp.int32, (SBA_SUB, SBA_SUB), 1)).astype(bf16)

    def q_block(qi, carry):
        q0 = pl.multiple_of(qi * SBA_TQ, SBA_TQ)
        qb = qn_ref[pl.ds(q0, SBA_TQ), :]
        rel = (lax.broadcasted_iota(jnp.int32, (SBA_TQ, SBA_SUB), 1)
               - lax.broadcasted_iota(jnp.int32, (SBA_TQ, SBA_SUB), 0)) - q0
        n_kblocks = lax.shift_right_logical(q0 + (SBA_TQ + SBA_TK - 1), SBA_TK.bit_length() - 1)

        def k_block(it, state):
            acc, run = state
            k0 = pl.multiple_of((n_kblocks - 1 - it) * SBA_TK, SBA_TK)
            kb = kn_ref[pl.ds(k0, SBA_TK), :]
            vb = v_ref[pl.ds(k0, SBA_TK), :]
            z = _dot_nt(qb, kb) * scale
            soft = jnp.log1p(jnp.exp(-jnp.abs(z)))
            log_keep = -jnp.maximum(z, 0.0) - soft
            log_beta = jnp.minimum(z, 0.0) - soft
            probs = []
            offs = run
            for c in reversed(range(nsub)):
                cols = slice(c * SBA_SUB, (c + 1) * SBA_SUB)
                strict = (rel + (k0 + c * SBA_SUB)) < 0
                lk = jnp.where(strict, log_keep[:, cols], 0.0)
                hi = lk.astype(bf16)
                lo = (lk - hi.astype(f32)).astype(bf16)
                within = _dot(hi, upper) + _dot(lo, upper)
                a = jnp.where(strict, jnp.exp(log_beta[:, cols] + within + offs), 0.0)
                probs.append(a.astype(bf16))
                offs = offs + jnp.sum(lk, axis=-1, keepdims=True)
            a_blk = jnp.concatenate(probs[::-1], axis=-1)
            return acc + _dot(a_blk, vb), offs

        acc, _ = lax.fori_loop(
            0, n_kblocks, k_block,
            (jnp.zeros((SBA_TQ, HEAD_DIM), f32), jnp.zeros((SBA_TQ, 1), f32)))
        o_ref[pl.ds(q0, SBA_TQ), :] = acc.astype(o_ref.dtype)
        return carry

    lax.fori_loop(0, t // SBA_TQ, q_block, 0)


def _sba(z, q_gain, k_gain, batch, seq):
    m = z.shape[0]
    c0 = COL_SBA_QKV // HEAD_DIM

    def head_spec(seg):
        return pl.BlockSpec((seq, HEAD_DIM), lambda b, h: (b, c0 + seg * N_HEADS + h))

    gain_spec = pl.BlockSpec((1, HEAD_DIM), lambda b, h: (0, 0))
    return pl.pallas_call(
        _sba_kernel,
        grid=(batch, N_HEADS),
        in_specs=[head_spec(0), head_spec(1), head_spec(2), gain_spec, gain_spec],
        out_specs=pl.BlockSpec((seq, HEAD_DIM), lambda b, h: (b, h)),
        out_shape=jax.ShapeDtypeStruct((m, WIDTH), bf16),
        scratch_shapes=[pltpu.VMEM((seq, HEAD_DIM), bf16), pltpu.VMEM((seq, HEAD_DIM), bf16)],
        compiler_params=_params("arbitrary", "arbitrary"),
        name="sba",
    )(z, z, z, q_gain, k_gain)


PAD = SUBLANES


def _unit_lower_inverse(l_mat, row, col):
    def same_block(d):
        return _block_id(row, d) == _block_id(col, d)

    eye = (row == col).astype(f32)
    l8 = jnp.where(same_block(8), l_mat, 0.0)
    x = eye - l8
    p = l8.astype(bf16)
    p2 = _dot(p, p)
    x = x + _dot(x.astype(bf16), p2.astype(bf16))
    p4 = _dot(p2.astype(bf16), p2.astype(bf16))
    x = x + _dot(x.astype(bf16), p4.astype(bf16))
    d = 8
    while d < CHUNK:
        off = jnp.where(same_block(2 * d) & jnp.logical_not(same_block(d)), l_mat, 0.0)
        xb = x.astype(bf16)
        x = x - _dot(xb, _dot(off.astype(bf16), xb).astype(bf16))
        d *= 2
    return x


def _gdn_kernel(zq_ref, zk_ref, zv_ref, zg_ref, gb_ref, cwq_ref, cwk_ref, cwv_ref, ng_ref, o_ref,
                xq_ref, xk_ref, xv_ref, g_ref, b_ref, u_ref, w_ref, qd_ref, kd_ref, at_ref, eg_ref):
    t = zq_ref.shape[0]
    head = pl.program_id(1)
    grp = GDN_GROUP

    sel_row = lax.broadcasted_iota(jnp.int32, (LANES, LANES), 0)
    gb = gb_ref[...]
    g_ref[...] = _dot(gb, (sel_row == head).astype(f32), _HI)
    b_ref[...] = _dot(gb, (sel_row == head + N_HEADS).astype(f32), _HI)

    zeros_pad = jnp.zeros((PAD, HEAD_DIM), f32)
    for src, dst in ((zq_ref, xq_ref), (zk_ref, xk_ref), (zv_ref, xv_ref)):
        dst[pl.ds(0, PAD), :] = zeros_pad
        dst[pl.ds(PAD, t), :] = src[...].astype(f32)

    row = lax.broadcasted_iota(jnp.int32, (grp, grp), 0)
    col = lax.broadcasted_iota(jnp.int32, (grp, grp), 1)
    same_chunk = _block_id(row, CHUNK) == _block_id(col, CHUNK)
    incl = same_chunk & (row >= col)
    strict = same_chunk & (row > col)
    tri = incl.astype(f32)
    chunk_ones = same_chunk.astype(f32)
    lane0 = (lax.broadcasted_iota(jnp.int32, (grp, LANES), 1) == 0).astype(f32)
    fold = (jnp.bitwise_and(lax.broadcasted_iota(jnp.int32, (grp, CHUNK), 0), CHUNK - 1)
            == lax.broadcasted_iota(jnp.int32, (grp, CHUNK), 1)).astype(bf16)

    def conv_silu(x_ref, cw_ref, r0):
        x = x_ref[pl.ds(r0, grp + PAD), :]
        acc = cw_ref[pl.ds(CONV_WIDTH - 1, 1), :] * x[PAD:PAD + grp]
        for i in range(CONV_WIDTH - 1):
            s = PAD - (CONV_WIDTH - 1) + i
            acc += cw_ref[pl.ds(i, 1), :] * x[s:s + grp]
        return _silu(acc)

    def l2n(x):
        return x * lax.rsqrt(jnp.sum(x * x, axis=-1, keepdims=True) + EPS)

    def prepare(it, carry):
        r0 = pl.multiple_of(it * grp, grp)
        rows = pl.ds(r0, grp)
        q = l2n(conv_silu(xq_ref, cwq_ref, r0)) * np.float32(HEAD_DIM ** -0.5)
        k = l2n(conv_silu(xk_ref, cwk_ref, r0))
        v = conv_silu(xv_ref, cwv_ref, r0)
        g = g_ref[rows, :]
        beta = b_ref[rows, :]
        gc = _dot(tri, g, _HI)
        g_tot = _dot(chunk_ones, g, _HI)
        gc_cols = _dot_nt(lane0, gc, _HI)
        diff = jnp.concatenate([gc, gc], axis=-1) - gc_cols
        decay = jnp.where(incl, jnp.exp(jnp.where(incl, diff, 0.0)), 0.0)
        kb = k * beta
        k16 = k.astype(bf16)
        l_mat = jnp.where(strict, _dot_nt(kb.astype(bf16), k16) * decay, 0.0)
        intra = _dot_nt(q.astype(bf16), k16) * decay
        e_gc = jnp.exp(gc)
        rhs = jnp.concatenate([v * beta, kb * e_gc], axis=-1)
        t_inv = _unit_lower_inverse(l_mat, row, col)
        sol = _dot(t_inv.astype(bf16), rhs.astype(bf16))
        u_ref[rows, :] = sol[:, :HEAD_DIM]
        w_ref[rows, :] = sol[:, HEAD_DIM:].astype(bf16)
        qd_ref[rows, :] = (q * e_gc).astype(bf16)
        kd_ref[rows, :] = (k * jnp.exp(g_tot - gc)).astype(bf16)
        at_ref[rows, :] = _dot(intra.astype(bf16), fold).astype(bf16)
        eg_ref[rows, :] = jnp.exp(g_tot)
        return carry

    lax.fori_loop(0, t // grp, prepare, 0)

    def scan(n, state):
        r0 = pl.multiple_of(n * CHUNK, CHUNK)
        rows = pl.ds(r0, CHUNK)
        s16 = state.astype(bf16)
        v_new = u_ref[rows, :] - _dot(w_ref[rows, :], s16)
        v16 = v_new.astype(bf16)
        o = _dot(qd_ref[rows, :], s16) + _dot(at_ref[rows, :], v16)
        state = state * eg_ref[pl.ds(r0, 1), :] + _dot_tn(kd_ref[rows, :], v16)
        gate = zg_ref[rows, :].astype(f32)
        o_ref[rows, :] = (_rms_rows(o, ng_ref[...]) * _silu(gate)).astype(o_ref.dtype)
        return state

    lax.fori_loop(0, t // CHUNK, scan, jnp.zeros((HEAD_DIM, HEAD_DIM), f32))


def _gdn(z, gb, conv_w, norm_gain, batch, seq):
    m = z.shape[0]
    cq = COL_GDN_QKV // HEAD_DIM
    cg = COL_GDN_GATE // HEAD_DIM

    def head_spec(c):
        return pl.BlockSpec((seq, HEAD_DIM), lambda b, h: (b, c + h))

    def conv_spec(seg):
        return pl.BlockSpec((CONV_WIDTH, HEAD_DIM), lambda b, h: (0, seg * N_HEADS + h))

    tok = lambda dt: pltpu.VMEM((seq, HEAD_DIM), dt)
    padded = pltpu.VMEM((seq + PAD, HEAD_DIM), f32)
    return pl.pallas_call(
        _gdn_kernel,
        grid=(batch, N_HEADS),
        in_specs=[
            head_spec(cq), head_spec(cq + N_HEADS), head_spec(cq + 2 * N_HEADS), head_spec(cg),
            pl.BlockSpec((seq, LANES), lambda b, h: (b, 0)),
            conv_spec(0), conv_spec(1), conv_spec(2),
            pl.BlockSpec((1, HEAD_DIM), lambda b, h: (0, 0)),
        ],
        out_specs=pl.BlockSpec((seq, HEAD_DIM), lambda b, h: (b, h)),
        out_shape=jax.ShapeDtypeStruct((m, WIDTH), bf16),
        scratch_shapes=[
            padded, padded, padded,
            tok(f32), tok(f32),
            tok(f32), tok(bf16), tok(bf16), tok(bf16),
            pltpu.VMEM((seq, CHUNK), bf16),
            tok(f32),
        ],
        compiler_params=_params("arbitrary", "arbitrary"),
        name="gdn",
    )(z, z, z, z, gb, conv_w, conv_w, conv_w, norm_gain)


def _row(v):
    return v.reshape(1, -1).astype(f32)


def _pad_row(v):
    return jnp.pad(v.astype(f32), (0, LANES - v.shape[0])).reshape(1, LANES)


def kernel(x, w_in, conv_w, a_log, dt_bias, gdn_norm_g, gmlp_ln_g, w_spatial, b_spatial, sba_q_g, sba_k_g, w_out_a, w_out_b, w_out_c, w_out, norm_mix_g, norm_mlp_g, w_ff1, w_ff2):
    batch, seq, d = x.shape
    depth = w_in.shape[0]
    m = batch * seq

    o_qkv = 0
    o_a = 3 * WIDTH
    o_gate = o_a + 2 * N_HEADS
    o_uv = o_gate + WIDTH
    o_sba = o_uv + 2 * GMLP_WIDTH
    o_gates = o_sba + 3 * WIDTH
    w_main = jnp.concatenate(
        [w_in[:, :, o_qkv:o_a], w_in[:, :, o_gate:o_uv], w_in[:, :, o_uv:o_sba],
         w_in[:, :, o_sba:o_gates], w_in[:, :, o_gates:]], axis=-1).astype(bf16)
    w_ab = jnp.pad(w_in[:, :, o_a:o_gate], ((0, 0), (0, 0), (0, LANES - 2 * N_HEADS))).astype(bf16)
    wa, wb, wc, wo = (w.astype(bf16) for w in (w_out_a, w_out_b, w_out_c, w_out))
    w1, w2 = w_ff1.astype(bf16), w_ff2.astype(bf16)
    b_rep = jnp.broadcast_to(b_spatial[..., None], b_spatial.shape + (LANES,)).astype(f32)

    xf = x.reshape(m, d).astype(f32)
    for l in range(depth):
        z, gb = _inproj(xf, _row(norm_mix_g[l]), w_main[l], w_ab[l], _pad_row(a_log[l]), _pad_row(dt_bias[l]))
        oa = _gdn(z, gb, conv_w[l].astype(f32), _row(gdn_norm_g[l]), batch, seq)
        ob = _gmlp(z, _row(gmlp_ln_g[l]), w_spatial[l].astype(f32), b_rep[l])
        oc = _sba(z, _row(sba_q_g[l]), _row(sba_k_g[l]), batch, seq)
        y = _merge(oa, ob, oc, wa[l], wb[l], wc[l], z)
        xf = _matmul_res(y, wo[l], xf)
        h1 = _ffn_up(xf, _row(norm_mlp_g[l]), w1[l])
        xf = _matmul_res(h1, w2[l], xf)
    return xf.reshape(batch, seq, d).astype(x.dtype)
```

```python
import jax
import jax.numpy as jnp
import numpy as np
from jax import lax
from jax.experimental import pallas as pl
from jax.experimental.pallas import tpu as pltpu

f32 = jnp.float32
bf16 = jnp.bfloat16

D_MODEL = 2048
CHUNK = 64
HEAD_DIM = 128
N_HEADS = 8
WIDTH = N_HEADS * HEAD_DIM
CONV_WIDTH = 4
GMLP_WIDTH = 1024
GMLP_GROUPS = 8
GMLP_BLOCK = 128
N_BRANCHES = 3
EPS = 1e-6
LOG2_E = float(np.log2(np.e))

LANES = 128
SUBLANES = 8
BF16_ROWS = 16
VMEM_LIMIT_BYTES = 60000 * 1024

COL_GDN_QKV = 0
COL_GDN_GATE = 3 * WIDTH
COL_GMLP_UV = COL_GDN_GATE + WIDTH
COL_SBA_QKV = COL_GMLP_UV + 2 * GMLP_WIDTH
COL_GATES = COL_SBA_QKV + 3 * WIDTH
N_MAIN = COL_GATES + N_BRANCHES * D_MODEL

TM = 1024
TN = 1024
TK = 2048
MERGE_TN = 512

HEADS_PER_STEP = 4
STEP_WIDTH = HEADS_PER_STEP * HEAD_DIM
GDN_GROUP = 512
SBA_TQ = 128
SBA_TK = 512
SBA_SUB = 128
GMLP_ROWS = 512


def _params(*sem):
    return pltpu.CompilerParams(dimension_semantics=sem, vmem_limit_bytes=VMEM_LIMIT_BYTES)


def _dot(a, b):
    return jnp.dot(a, b, preferred_element_type=f32)


def _dot_nt(a, b):
    return lax.dot_general(a, b, (((1,), (1,)), ((), ())), preferred_element_type=f32)


def _sigmoid(x):
    return 1.0 / (1.0 + jnp.exp(-x))


def _silu(x):
    return x * _sigmoid(x)


def _softplus(x):
    return jnp.maximum(x, 0.0) + jnp.log1p(jnp.exp(-jnp.abs(x)))


def _block_id(idx, size):
    return lax.shift_right_logical(idx, int(size).bit_length() - 1)


def _split3(x):
    hi = x.astype(bf16)
    r1 = x - hi.astype(f32)
    mid = r1.astype(bf16)
    lo = (r1 - mid.astype(f32)).astype(bf16)
    return jnp.concatenate([hi, mid, lo], axis=-1)


def _sum3(x, n):
    return x[..., :n] + x[..., n:2 * n] + x[..., 2 * n:3 * n]


def _rms_rows(x, gain):
    ms = jnp.mean(x * x, axis=-1, keepdims=True)
    return x * lax.rsqrt(ms + EPS) * gain


def _inproj_kernel(x_ref, g_ref, w_ref, wab_ref, alog_ref, dt_ref, z_ref, gb_ref, hn_ref):
    @pl.when(pl.program_id(1) == 0)
    def _():
        hn = _rms_rows(x_ref[...], g_ref[...]).astype(bf16)
        hn_ref[...] = hn
        ab = _dot(hn, wab_ref[...])
        lane = lax.broadcasted_iota(jnp.int32, ab.shape, 1)
        g = -jnp.exp(alog_ref[...]) * _softplus(ab + dt_ref[...])
        gb_ref[...] = jnp.where(lane < N_HEADS, g, _sigmoid(ab))

    z_ref[...] = _dot(hn_ref[...], w_ref[...]).astype(z_ref.dtype)


def _inproj(x, gain, w_main, w_ab, alog_row, dt_row):
    m = x.shape[0]
    return pl.pallas_call(
        _inproj_kernel,
        grid=(m // TM, N_MAIN // TN),
        in_specs=[
            pl.BlockSpec((TM, D_MODEL), lambda i, j: (i, 0)),
            pl.BlockSpec((1, D_MODEL), lambda i, j: (0, 0)),
            pl.BlockSpec((D_MODEL, TN), lambda i, j: (0, j)),
            pl.BlockSpec((D_MODEL, LANES), lambda i, j: (0, 0)),
            pl.BlockSpec((1, LANES), lambda i, j: (0, 0)),
            pl.BlockSpec((1, LANES), lambda i, j: (0, 0)),
        ],
        out_specs=[
            pl.BlockSpec((TM, TN), lambda i, j: (i, j)),
            pl.BlockSpec((TM, LANES), lambda i, j: (i, 0)),
        ],
        out_shape=[
            jax.ShapeDtypeStruct((m, N_MAIN), bf16),
            jax.ShapeDtypeStruct((m, LANES), f32),
        ],
        scratch_shapes=[pltpu.VMEM((TM, D_MODEL), bf16)],
        compiler_params=_params("arbitrary", "arbitrary"),
        name="inproj",
    )(x, gain, w_main, w_ab, alog_row, dt_row)


def _ffn_up_kernel(x_ref, g_ref, w_ref, h_ref, hn_ref):
    @pl.when(pl.program_id(1) == 0)
    def _():
        hn_ref[...] = _rms_rows(x_ref[...], g_ref[...]).astype(bf16)

    a = jnp.maximum(_dot(hn_ref[...], w_ref[...]), 0.0)
    h_ref[...] = (a * a).astype(h_ref.dtype)


def _ffn_up(x, gain, w1):
    m, n = x.shape[0], w1.shape[1]
    return pl.pallas_call(
        _ffn_up_kernel,
        grid=(m // TM, n // TN),
        in_specs=[
            pl.BlockSpec((TM, D_MODEL), lambda i, j: (i, 0)),
            pl.BlockSpec((1, D_MODEL), lambda i, j: (0, 0)),
            pl.BlockSpec((D_MODEL, TN), lambda i, j: (0, j)),
        ],
        out_specs=pl.BlockSpec((TM, TN), lambda i, j: (i, j)),
        out_shape=jax.ShapeDtypeStruct((m, n), bf16),
        scratch_shapes=[pltpu.VMEM((TM, D_MODEL), bf16)],
        compiler_params=_params("arbitrary", "arbitrary"),
        name="ffn_up",
    )(x, gain, w1)


def _matmul_res_kernel(a_ref, w_ref, r_ref, o_ref):
    d = _dot(a_ref[...], w_ref[...])

    @pl.when(pl.program_id(2) == 0)
    def _():
        o_ref[...] = r_ref[...] + d

    @pl.when(pl.program_id(2) != 0)
    def _():
        o_ref[...] += d


def _matmul_res(a, w, res):
    m, k = a.shape
    n = w.shape[1]
    return pl.pallas_call(
        _matmul_res_kernel,
        grid=(m // TM, n // TN, k // TK),
        in_specs=[
            pl.BlockSpec((TM, TK), lambda i, j, kk: (i, kk)),
            pl.BlockSpec((TK, TN), lambda i, j, kk: (kk, j)),
            pl.BlockSpec((TM, TN), lambda i, j, kk: (i, j)),
        ],
        out_specs=pl.BlockSpec((TM, TN), lambda i, j, kk: (i, j)),
        out_shape=jax.ShapeDtypeStruct((m, n), f32),
        compiler_params=_params("arbitrary", "arbitrary", "arbitrary"),
        name="matmul_res",
    )(a, w, res)


def _merge_kernel(oa_ref, ob_ref, oc_ref, wa_ref, wb_ref, wc_ref, ga_ref, gb_ref, gc_ref, y_ref):
    y = _sigmoid(ga_ref[...].astype(f32)) * _dot(oa_ref[...], wa_ref[...])
    y += _sigmoid(gb_ref[...].astype(f32)) * _dot(ob_ref[...], wb_ref[...])
    y += _sigmoid(gc_ref[...].astype(f32)) * _dot(oc_ref[...], wc_ref[...])
    y_ref[...] = y.astype(y_ref.dtype)


def _merge(oa, ob, oc, wa, wb, wc, z):
    m = oa.shape[0]
    tn = MERGE_TN
    per_branch = D_MODEL // tn
    gate0 = COL_GATES // tn
    branch_in = pl.BlockSpec((TM, WIDTH), lambda i, j: (i, 0))
    branch_w = pl.BlockSpec((WIDTH, tn), lambda i, j: (0, j))

    def gate_spec(b):
        return pl.BlockSpec((TM, tn), lambda i, j: (i, gate0 + b * per_branch + j))

    return pl.pallas_call(
        _merge_kernel,
        grid=(m // TM, D_MODEL // tn),
        in_specs=[branch_in, branch_in, branch_in, branch_w, branch_w, branch_w,
                  gate_spec(0), gate_spec(1), gate_spec(2)],
        out_specs=pl.BlockSpec((TM, tn), lambda i, j: (i, j)),
        out_shape=jax.ShapeDtypeStruct((m, D_MODEL), bf16),
        compiler_params=_params("arbitrary", "arbitrary"),
        name="merge",
    )(oa, ob, oc, wa, wb, wc, z, z, z)


def _gelu(x):
    return 0.5 * x * (1.0 + lax.erf(x * np.float32(np.sqrt(0.5))))


def _gmlp_kernel(zu_ref, zv_ref, g_ref, ws_ref, bs_ref, o_ref):
    u = _gelu(zu_ref[...].astype(f32))
    v = _gelu(zv_ref[...].astype(f32))
    mu = jnp.mean(v, axis=-1, keepdims=True)
    vc = v - mu
    var = jnp.mean(vc * vc, axis=-1, keepdims=True)
    vb = (vc * lax.rsqrt(var + EPS) * g_ref[...]).astype(bf16)

    t_chunk = _block_id(lax.broadcasted_iota(jnp.int32, (GMLP_BLOCK, GMLP_BLOCK), 0), CHUNK)
    s_chunk = _block_id(lax.broadcasted_iota(jnp.int32, (GMLP_BLOCK, GMLP_BLOCK), 1), CHUNK)
    causal = s_chunk <= t_chunk
    gdim = GMLP_WIDTH // GMLP_GROUPS
    for g in range(GMLP_GROUPS):
        ws = jnp.where(causal, ws_ref[g], 0.0).astype(bf16)
        cols = slice(g * gdim, (g + 1) * gdim)
        for blk in range(GMLP_ROWS // GMLP_BLOCK):
            rows = slice(blk * GMLP_BLOCK, (blk + 1) * GMLP_BLOCK)
            s = _dot(ws, vb[rows, cols]) + bs_ref[g]
            o_ref[rows, cols] = (u[rows, cols] * s).astype(o_ref.dtype)


def _gmlp(z, ln_gain, w_spatial, b_rep):
    m = z.shape[0]
    cu = COL_GMLP_UV // GMLP_WIDTH
    return pl.pallas_call(
        _gmlp_kernel,
        grid=(m // GMLP_ROWS,),
        in_specs=[
            pl.BlockSpec((GMLP_ROWS, GMLP_WIDTH), lambda i: (i, cu)),
            pl.BlockSpec((GMLP_ROWS, GMLP_WIDTH), lambda i: (i, cu + 1)),
            pl.BlockSpec((1, GMLP_WIDTH), lambda i: (0, 0)),
            pl.BlockSpec((GMLP_GROUPS, GMLP_BLOCK, GMLP_BLOCK), lambda i: (0, 0, 0)),
            pl.BlockSpec((GMLP_GROUPS, GMLP_BLOCK, LANES), lambda i: (0, 0, 0)),
        ],
        out_specs=pl.BlockSpec((GMLP_ROWS, GMLP_WIDTH), lambda i: (i, 0)),
        out_shape=jax.ShapeDtypeStruct((m, GMLP_WIDTH), bf16),
        compiler_params=_params("arbitrary"),
        name="gmlp",
    )(z, z, ln_gain, w_spatial, b_rep)


def _sba_kernel(q_ref, k_ref, v_ref, qg_ref, kg_ref, o_ref, qn_ref, kn_ref):
    t = q_ref.shape[0]
    head_cols = [slice(j * HEAD_DIM, (j + 1) * HEAD_DIM) for j in range(HEADS_PER_STEP)]
    for hc in head_cols:
        qn_ref[:, hc] = _rms_rows(q_ref[:, hc].astype(f32), qg_ref[...]).astype(bf16)
        kn_ref[:, hc] = _rms_rows(k_ref[:, hc].astype(f32), kg_ref[...]).astype(bf16)
    to_log2 = np.float32(HEAD_DIM ** -0.5 * LOG2_E)
    nsub = SBA_TK // SBA_SUB
    rows_all = HEADS_PER_STEP * SBA_TQ
    wr = lax.broadcasted_iota(jnp.int32, (2 * SBA_SUB, 2 * SBA_SUB), 0)
    wc = lax.broadcasted_iota(jnp.int32, (2 * SBA_SUB, 2 * SBA_SUB), 1)
    suffix_w = ((jnp.bitwise_and(wr, SBA_SUB - 1) > wc) | (wc >= SBA_SUB)).astype(bf16)

    def block(qbs, k0, rel, state):
        acc, run = state
        z = jnp.concatenate([_dot_nt(qbs[j], kn_ref[pl.ds(k0, SBA_TK), head_cols[j]])
                             for j in range(HEADS_PER_STEP)], axis=0) * to_log2
        neg_abs = pltpu.bitcast(pltpu.bitcast(z, jnp.uint32) | jnp.uint32(0x80000000), f32)
        soft = jnp.log(1.0 + jnp.exp2(neg_abs)) * np.float32(LOG2_E)
        drop = jnp.maximum(z, 0.0) + soft
        log_beta = z - drop
        probs = []
        offs = run
        for c in reversed(range(nsub)):
            cols = slice(c * SBA_SUB, (c + 1) * SBA_SUB)
            dc = drop[:, cols]
            if rel is not None:
                strict = (rel + (k0 + c * SBA_SUB)) < 0
                dc = jnp.where(strict, dc, 0.0)
            hi = dc.astype(bf16)
            lo = (dc - hi.astype(f32)).astype(bf16)
            sums = _dot(jnp.concatenate([hi, lo], axis=-1), suffix_w)
            a = jnp.exp2(log_beta[:, cols] - sums[:, :SBA_SUB] - offs)
            if rel is not None:
                a = jnp.where(strict, a, 0.0)
            probs.append(a.astype(bf16))
            offs = offs + sums[:, SBA_SUB:]
        a_blk = jnp.concatenate(probs[::-1], axis=-1)
        pv = jnp.concatenate(
            [_dot(a_blk[j * SBA_TQ:(j + 1) * SBA_TQ], v_ref[pl.ds(k0, SBA_TK), head_cols[j]])
             for j in range(HEADS_PER_STEP)], axis=0)
        return acc + pv, offs

    def q_block(qi, carry):
        q0 = pl.multiple_of(qi * SBA_TQ, SBA_TQ)
        qbs = [qn_ref[pl.ds(q0, SBA_TQ), hc] for hc in head_cols]
        rel = (lax.broadcasted_iota(jnp.int32, (rows_all, SBA_SUB), 1)
               - jnp.bitwise_and(lax.broadcasted_iota(jnp.int32, (rows_all, SBA_SUB), 0), SBA_TQ - 1)) - q0
        n_left = _block_id(q0, SBA_TK)
        k_diag = pl.multiple_of(n_left * SBA_TK, SBA_TK)
        zeros = jnp.zeros((rows_all, HEAD_DIM), f32)
        state = block(qbs, k_diag, rel, (zeros, zeros))

        def left_block(it, state):
            k0 = pl.multiple_of((n_left - 1 - it) * SBA_TK, SBA_TK)
            return block(qbs, k0, None, state)

        acc, _ = lax.fori_loop(0, n_left, left_block, state)
        for j in range(HEADS_PER_STEP):
            o_ref[pl.ds(q0, SBA_TQ), head_cols[j]] = acc[j * SBA_TQ:(j + 1) * SBA_TQ].astype(o_ref.dtype)
        return carry

    lax.fori_loop(0, t // SBA_TQ, q_block, 0)


def _sba(z, q_gain, k_gain, batch, seq):
    m = z.shape[0]
    c0 = COL_SBA_QKV // STEP_WIDTH
    per_seg = WIDTH // STEP_WIDTH

    def head_spec(seg):
        return pl.BlockSpec((seq, STEP_WIDTH), lambda b, h: (b, c0 + seg * per_seg + h))

    gain_spec = pl.BlockSpec((1, HEAD_DIM), lambda b, h: (0, 0))
    tok = pltpu.VMEM((seq, STEP_WIDTH), bf16)
    return pl.pallas_call(
        _sba_kernel,
        grid=(batch, per_seg),
        in_specs=[head_spec(0), head_spec(1), head_spec(2), gain_spec, gain_spec],
        out_specs=pl.BlockSpec((seq, STEP_WIDTH), lambda b, h: (b, h)),
        out_shape=jax.ShapeDtypeStruct((m, WIDTH), bf16),
        scratch_shapes=[tok, tok],
        compiler_params=_params("arbitrary", "arbitrary"),
        name="sba",
    )(z, z, z, q_gain, k_gain)


XPAD = BF16_ROWS


def _bmm(a, b):
    return jnp.einsum('bij,bjk->bik', a, b, preferred_element_type=f32)


def _bmm_nt(a, b):
    return jnp.einsum('bid,bjd->bij', a, b, preferred_element_type=f32)


def _bmm_tn(a, b):
    return jnp.einsum('bck,bcv->bkv', a, b, preferred_element_type=f32)


def _unit_lower_inverse(l_mat, row, col):
    def same_block(d):
        return _block_id(row, d) == _block_id(col, d)

    eye = (row == col).astype(f32)
    l8 = jnp.where(same_block(8), l_mat, 0.0)
    p = l8.astype(bf16)
    x = (eye - l8).astype(bf16)
    p2 = _bmm(p, p)
    x = _bmm(x, (eye + p2).astype(bf16)).astype(bf16)
    p2 = p2.astype(bf16)
    p4 = _bmm(p2, p2)
    x = _bmm(x, (eye + p4).astype(bf16)).astype(bf16)
    d = 8
    while d < CHUNK:
        off = jnp.where(same_block(2 * d) & jnp.logical_not(same_block(d)), l_mat, 0.0).astype(bf16)
        x = _bmm(x, (eye - _bmm(off, x)).astype(bf16)).astype(bf16)
        d *= 2
    return x


SCAN_ROWS = HEAD_DIM + CHUNK


def _gdn_kernel(zq_ref, zk_ref, zv_ref, zg_ref, gb_ref, cwq_ref, cwk_ref, cwv_ref, ng_ref, o_ref,
                xq_ref, xk_ref, xv_ref, lin_ref, off_ref, eg_ref):
    t = zq_ref.shape[0]
    head0 = pl.program_id(1) * HEADS_PER_STEP
    grp = GDN_GROUP
    nc = grp // CHUNK
    head_cols = [slice(j * HEAD_DIM, (j + 1) * HEAD_DIM) for j in range(HEADS_PER_STEP)]

    zeros_pad = jnp.zeros((XPAD, STEP_WIDTH), bf16)
    for src, dst in ((zq_ref, xq_ref), (zk_ref, xk_ref), (zv_ref, xv_ref)):
        dst[pl.ds(0, XPAD), :] = zeros_pad
        dst[pl.ds(XPAD, t), :] = src[...]

    row = lax.broadcasted_iota(jnp.int32, (CHUNK, CHUNK), 0)
    col = lax.broadcasted_iota(jnp.int32, (CHUNK, CHUNK), 1)
    incl = row >= col
    strict = row > col
    tri_ones = jnp.concatenate([incl.astype(bf16), jnp.ones((CHUNK, CHUNK), bf16)], axis=0)
    tri_ones = jnp.broadcast_to(tri_ones, (nc, 2 * CHUNK, CHUNK))
    sel_k = jnp.bitwise_and(lax.broadcasted_iota(jnp.int32, (3 * LANES, LANES), 0), LANES - 1)
    sel_rows = (jnp.bitwise_and(lax.broadcasted_iota(jnp.int32, (SUBLANES, 3 * LANES), 1), LANES - 1)
                == head0 + lax.broadcasted_iota(jnp.int32, (SUBLANES, 3 * LANES), 0)).astype(bf16)
    sel_rows = jnp.broadcast_to(sel_rows, (nc, SUBLANES, 3 * LANES))

    def conv_silu(x_ref, cw_ref, r0, hc):
        x = x_ref[pl.ds(r0, grp + XPAD), hc].astype(f32)
        acc = cw_ref[pl.ds(CONV_WIDTH - 1, 1), hc] * x[XPAD:XPAD + grp]
        for i in range(CONV_WIDTH - 1):
            s = XPAD - (CONV_WIDTH - 1) + i
            acc += cw_ref[pl.ds(i, 1), hc] * x[s:s + grp]
        return _silu(acc)

    def l2n(x):
        return x * lax.rsqrt(jnp.sum(x * x, axis=-1, keepdims=True) + EPS)

    def chunked(x):
        return x.reshape(nc, CHUNK, x.shape[-1])

    def prepare(it, carry):
        r0 = pl.multiple_of(it * grp, grp)
        gb3 = _split3(gb_ref[pl.ds(r0, grp), :])
        ct = _bmm(tri_ones, chunked(gb3))
        cum3 = _split3(_sum3(ct[:, :CHUNK], LANES).reshape(grp, LANES))
        tot3 = _split3(_sum3(ct[:, CHUNK:], LANES).reshape(grp, LANES))
        gc_rows = _bmm_nt(sel_rows, chunked(cum3))

        k16s, kb16s, q16s, rhs16s, diffs, qds, kd16s = [], [], [], [], [], [], []
        for j, hc in enumerate(head_cols):
            pick_g = (sel_k == head0 + j).astype(bf16)
            pick_b = (sel_k == head0 + j + N_HEADS).astype(bf16)
            gc = _dot(cum3, pick_g)
            g_tot = _dot(tot3, pick_g)
            beta = _dot(gb3, pick_b)
            q = l2n(conv_silu(xq_ref, cwq_ref, r0, hc)) * np.float32(HEAD_DIM ** -0.5)
            k = l2n(conv_silu(xk_ref, cwk_ref, r0, hc))
            v = conv_silu(xv_ref, cwv_ref, r0, hc)
            kb = k * beta
            e_gc = jnp.exp(gc)
            k16s.append(chunked(k.astype(bf16)))
            kb16s.append(chunked(kb.astype(bf16)))
            q16s.append(chunked(q.astype(bf16)))
            rhs16s.append(chunked(jnp.concatenate([v * beta, kb * e_gc], axis=-1).astype(bf16)))
            diffs.append(chunked(gc)[:, :, :CHUNK] - gc_rows[:, j:j + 1, :])
            qds.append(chunked(q * e_gc))
            kd16s.append(chunked((k * jnp.exp(g_tot - gc)).astype(bf16)))
            e0 = pl.multiple_of(it * (nc * SUBLANES), nc * SUBLANES)
            eg_ref[pl.ds(e0, nc * SUBLANES), hc] = (
                chunked(jnp.exp(g_tot))[:, :SUBLANES, :].reshape(nc * SUBLANES, HEAD_DIM))

        cat = lambda xs: jnp.concatenate(xs, axis=0)
        k16, diff = cat(k16s), cat(diffs)
        decay = jnp.where(incl, jnp.exp(jnp.where(incl, diff, 0.0)), 0.0)
        l_mat = jnp.where(strict, _bmm_nt(cat(kb16s), k16) * decay, 0.0)
        intra = (_bmm_nt(cat(q16s), k16) * decay).astype(bf16)
        t_inv = _unit_lower_inverse(l_mat, row, col)
        uw = _bmm(t_inv, cat(rhs16s)).astype(bf16)
        state_map = _bmm_tn(cat(kd16s), uw)
        out_map = _bmm(intra, uw)
        q_eff = cat(qds) - out_map[:, :, HEAD_DIM:]
        lin = jnp.concatenate([state_map[:, :, HEAD_DIM:], q_eff], axis=1).astype(bf16)
        off = jnp.concatenate([state_map[:, :, :HEAD_DIM], out_map[:, :, :HEAD_DIM]], axis=1)
        s0 = pl.multiple_of(it * (nc * SCAN_ROWS), nc * SCAN_ROWS)
        for j, hc in enumerate(head_cols):
            mine = slice(j * nc, (j + 1) * nc)
            lin_ref[pl.ds(s0, nc * SCAN_ROWS), hc] = lin[mine].reshape(nc * SCAN_ROWS, HEAD_DIM)
            off_ref[pl.ds(s0, nc * SCAN_ROWS), hc] = off[mine].reshape(nc * SCAN_ROWS, HEAD_DIM)
        return carry

    lax.fori_loop(0, t // grp, prepare, 0)

    def scan(n, states):
        s0 = pl.multiple_of(n * SCAN_ROWS, SCAN_ROWS)
        e0 = pl.multiple_of(n * SUBLANES, SUBLANES)
        r0 = pl.multiple_of(n * CHUNK, CHUNK)
        new_states = []
        for j, hc in enumerate(head_cols):
            state = states[j]
            prod = _dot(lin_ref[pl.ds(s0, SCAN_ROWS), hc], state.astype(bf16))
            off = off_ref[pl.ds(s0, SCAN_ROWS), hc]
            new_states.append(state * eg_ref[pl.ds(e0, 1), hc] + (off[:HEAD_DIM] - prod[:HEAD_DIM]))
            o = prod[HEAD_DIM:] + off[HEAD_DIM:]
            gate = zg_ref[pl.ds(r0, CHUNK), hc].astype(f32)
            o_ref[pl.ds(r0, CHUNK), hc] = (_rms_rows(o, ng_ref[...]) * _silu(gate)).astype(o_ref.dtype)
        return tuple(new_states)

    zero_state = jnp.zeros((HEAD_DIM, HEAD_DIM), f32)
    lax.fori_loop(0, t // CHUNK, scan, (zero_state,) * HEADS_PER_STEP)


def _gdn(z, gb, conv_w, norm_gain, batch, seq):
    m = z.shape[0]
    per_seg = WIDTH // STEP_WIDTH
    cq = COL_GDN_QKV // STEP_WIDTH
    cg = COL_GDN_GATE // STEP_WIDTH

    def head_spec(c):
        return pl.BlockSpec((seq, STEP_WIDTH), lambda b, h: (b, c + h))

    def conv_spec(seg):
        return pl.BlockSpec((CONV_WIDTH, STEP_WIDTH), lambda b, h: (0, seg * per_seg + h))

    n_chunks = seq // CHUNK
    padded = pltpu.VMEM((seq + XPAD, STEP_WIDTH), bf16)
    return pl.pallas_call(
        _gdn_kernel,
        grid=(batch, per_seg),
        in_specs=[
            head_spec(cq), head_spec(cq + per_seg), head_spec(cq + 2 * per_seg), head_spec(cg),
            pl.BlockSpec((seq, LANES), lambda b, h: (b, 0)),
            conv_spec(0), conv_spec(1), conv_spec(2),
            pl.BlockSpec((1, HEAD_DIM), lambda b, h: (0, 0)),
        ],
        out_specs=pl.BlockSpec((seq, STEP_WIDTH), lambda b, h: (b, h)),
        out_shape=jax.ShapeDtypeStruct((m, WIDTH), bf16),
        scratch_shapes=[
            padded, padded, padded,
            pltpu.VMEM((n_chunks * SCAN_ROWS, STEP_WIDTH), bf16),
            pltpu.VMEM((n_chunks * SCAN_ROWS, STEP_WIDTH), f32),
            pltpu.VMEM((n_chunks * SUBLANES, STEP_WIDTH), f32),
        ],
        compiler_params=_params("arbitrary", "arbitrary"),
        name="gdn",
    )(z, z, z, z, gb, conv_w, conv_w, conv_w, norm_gain)


def _row(v):
    return v.reshape(1, -1).astype(f32)


def _pad_row(v):
    return jnp.pad(v.astype(f32), (0, LANES - v.shape[0])).reshape(1, LANES)


def kernel(x, w_in, conv_w, a_log, dt_bias, gdn_norm_g, gmlp_ln_g, w_spatial, b_spatial, sba_q_g, sba_k_g, w_out_a, w_out_b, w_out_c, w_out, norm_mix_g, norm_mlp_g, w_ff1, w_ff2):
    batch, seq, d = x.shape
    depth = w_in.shape[0]
    m = batch * seq

    o_a = 3 * WIDTH
    o_rest = o_a + 2 * N_HEADS
    w_main = jnp.concatenate([w_in[:, :, :o_a], w_in[:, :, o_rest:]], axis=-1).astype(bf16)
    w_ab = jnp.pad(w_in[:, :, o_a:o_rest], ((0, 0), (0, 0), (0, LANES - 2 * N_HEADS))).astype(bf16)
    wa, wb, wc, wo = (w.astype(bf16) for w in (w_out_a, w_out_b, w_out_c, w_out))
    w1, w2 = w_ff1.astype(bf16), w_ff2.astype(bf16)
    b_rep = jnp.broadcast_to(b_spatial[..., None], b_spatial.shape + (LANES,)).astype(f32)

    xf = x.reshape(m, d).astype(f32)
    for l in range(depth):
        z, gb = _inproj(xf, _row(norm_mix_g[l]), w_main[l], w_ab[l], _pad_row(a_log[l]), _pad_row(dt_bias[l]))
        oa = _gdn(z, gb, conv_w[l].astype(f32), _row(gdn_norm_g[l]), batch, seq)
        ob = _gmlp(z, _row(gmlp_ln_g[l]), w_spatial[l].astype(f32), b_rep[l])
        oc = _sba(z, _row(sba_q_g[l]), _row(sba_k_g[l]), batch, seq)
        y = _merge(oa, ob, oc, wa[l], wb[l], wc[l], z)
        xf = _matmul_res(y, wo[l], xf)
        h1 = _ffn_up(xf, _row(norm_mlp_g[l]), w1[l])
        xf = _matmul_res(h1, w2[l], xf)
    return xf.reshape(batch, seq, d).astype(x.dtype)
```

```python
import jax
import jax.numpy as jnp
import numpy as np
from jax import lax
from jax.experimental import pallas as pl
from jax.experimental.pallas import tpu as pltpu

f32 = jnp.float32
bf16 = jnp.bfloat16

D_MODEL = 2048
CHUNK = 64
HEAD_DIM = 128
N_HEADS = 8
WIDTH = N_HEADS * HEAD_DIM
CONV_WIDTH = 4
GMLP_WIDTH = 1024
GMLP_GROUPS = 8
GMLP_BLOCK = 128
N_BRANCHES = 3
EPS = 1e-6
LOG2_E = float(np.log2(np.e))

LANES = 128
SUBLANES = 8
BF16_ROWS = 16
VMEM_LIMIT_BYTES = 60000 * 1024

COL_GDN_QKV = 0
COL_GDN_GATE = 3 * WIDTH
COL_GMLP_UV = COL_GDN_GATE + WIDTH
COL_SBA_QKV = COL_GMLP_UV + 2 * GMLP_WIDTH
COL_GATES = COL_SBA_QKV + 3 * WIDTH
N_MAIN = COL_GATES + N_BRANCHES * D_MODEL

TM = 1024
TN = 1024
TK = 2048
MERGE_TM = 512

HEADS_PER_STEP = 4
STEP_WIDTH = HEADS_PER_STEP * HEAD_DIM
GDN_GROUP = 512
SBA_TQ = 128
SBA_TK = 512
SBA_SUB = 128
GMLP_ROWS = 512


def _params(*sem):
    return pltpu.CompilerParams(dimension_semantics=sem, vmem_limit_bytes=VMEM_LIMIT_BYTES)


def _dot(a, b):
    return jnp.dot(a, b, preferred_element_type=f32)


def _dot_nt(a, b):
    return lax.dot_general(a, b, (((1,), (1,)), ((), ())), preferred_element_type=f32)


def _sigmoid(x):
    return 1.0 / (1.0 + jnp.exp(-x))


def _silu(x):
    return x * _sigmoid(x)


def _softplus(x):
    return jnp.maximum(x, 0.0) + jnp.log1p(jnp.exp(-jnp.abs(x)))


def _block_id(idx, size):
    return lax.shift_right_logical(idx, int(size).bit_length() - 1)


def _split3(x):
    hi = x.astype(bf16)
    r1 = x - hi.astype(f32)
    mid = r1.astype(bf16)
    lo = (r1 - mid.astype(f32)).astype(bf16)
    return jnp.concatenate([hi, mid, lo], axis=-1)


def _sum3(x, n):
    return x[..., :n] + x[..., n:2 * n] + x[..., 2 * n:3 * n]


def _rms_rows(x, gain):
    ms = jnp.mean(x * x, axis=-1, keepdims=True)
    return x * lax.rsqrt(ms + EPS) * gain


N_DIRECT = COL_GDN_GATE // TN


def _inproj_kernel(x_ref, g_ref, w_head_ref, w_tail_ref, wab_ref, alog_ref, dt_ref, z_ref, gb_ref, hn_ref):
    j = pl.program_id(1)

    @pl.when(j == 0)
    def _():
        hn = _rms_rows(x_ref[...], g_ref[...]).astype(bf16)
        hn_ref[...] = hn
        ab = _dot(hn, wab_ref[...])
        lane = lax.broadcasted_iota(jnp.int32, ab.shape, 1)
        g = -jnp.exp(alog_ref[...]) * _softplus(ab + dt_ref[...])
        gb_ref[...] = jnp.where(lane < N_HEADS, g, _sigmoid(ab))

    @pl.when(j < N_DIRECT)
    def _():
        z_ref[...] = _dot(hn_ref[...], w_head_ref[...].astype(bf16)).astype(z_ref.dtype)

    @pl.when(j >= N_DIRECT)
    def _():
        z_ref[...] = _dot(hn_ref[...], w_tail_ref[...]).astype(z_ref.dtype)


def _inproj(x, gain, w_in, w_tail, w_ab, alog_row, dt_row, l):
    m = x.shape[0]
    return pl.pallas_call(
        _inproj_kernel,
        grid=(m // TM, N_MAIN // TN),
        in_specs=[
            pl.BlockSpec((TM, D_MODEL), lambda i, j: (i, 0)),
            pl.BlockSpec((None, 1, D_MODEL), lambda i, j: (l, 0, 0)),
            pl.BlockSpec((None, D_MODEL, TN), lambda i, j: (l, 0, jnp.minimum(j, N_DIRECT - 1))),
            pl.BlockSpec((None, D_MODEL, TN), lambda i, j: (l, 0, jnp.maximum(j - N_DIRECT, 0))),
            pl.BlockSpec((None, D_MODEL, LANES), lambda i, j: (l, 0, 0)),
            pl.BlockSpec((None, 1, LANES), lambda i, j: (l, 0, 0)),
            pl.BlockSpec((None, 1, LANES), lambda i, j: (l, 0, 0)),
        ],
        out_specs=[
            pl.BlockSpec((TM, TN), lambda i, j: (i, j)),
            pl.BlockSpec((TM, LANES), lambda i, j: (i, 0)),
        ],
        out_shape=[
            jax.ShapeDtypeStruct((m, N_MAIN), bf16),
            jax.ShapeDtypeStruct((m, LANES), f32),
        ],
        scratch_shapes=[pltpu.VMEM((TM, D_MODEL), bf16)],
        compiler_params=_params("arbitrary", "arbitrary"),
        name="inproj",
    )(x, gain, w_in, w_tail, w_ab, alog_row, dt_row)


def _ffn_up_kernel(x_ref, g_ref, w_ref, h_ref, hn_ref):
    @pl.when(pl.program_id(1) == 0)
    def _():
        hn_ref[...] = _rms_rows(x_ref[...], g_ref[...]).astype(bf16)

    a = jnp.maximum(_dot(hn_ref[...], w_ref[...].astype(bf16)), 0.0)
    h_ref[...] = (a * a).astype(h_ref.dtype)


def _ffn_up(x, gain, w1, l):
    m, n = x.shape[0], w1.shape[-1]
    return pl.pallas_call(
        _ffn_up_kernel,
        grid=(m // TM, n // TN),
        in_specs=[
            pl.BlockSpec((TM, D_MODEL), lambda i, j: (i, 0)),
            pl.BlockSpec((None, 1, D_MODEL), lambda i, j: (l, 0, 0)),
            pl.BlockSpec((None, D_MODEL, TN), lambda i, j: (l, 0, j)),
        ],
        out_specs=pl.BlockSpec((TM, TN), lambda i, j: (i, j)),
        out_shape=jax.ShapeDtypeStruct((m, n), bf16),
        scratch_shapes=[pltpu.VMEM((TM, D_MODEL), bf16)],
        compiler_params=_params("arbitrary", "arbitrary"),
        name="ffn_up",
    )(x, gain, w1)


def _matmul_res_kernel(a_ref, w_ref, r_ref, o_ref):
    d = _dot(a_ref[...], w_ref[...].astype(bf16))

    @pl.when(pl.program_id(2) == 0)
    def _():
        o_ref[...] = r_ref[...] + d

    @pl.when(pl.program_id(2) != 0)
    def _():
        o_ref[...] += d


def _matmul_res(a, w, res, l):
    m, k = a.shape
    n = w.shape[-1]
    return pl.pallas_call(
        _matmul_res_kernel,
        grid=(m // TM, n // TN, k // TK),
        in_specs=[
            pl.BlockSpec((TM, TK), lambda i, j, kk: (i, kk)),
            pl.BlockSpec((None, TK, TN), lambda i, j, kk: (l, kk, j)),
            pl.BlockSpec((TM, TN), lambda i, j, kk: (i, j)),
        ],
        out_specs=pl.BlockSpec((TM, TN), lambda i, j, kk: (i, j)),
        out_shape=jax.ShapeDtypeStruct((m, n), f32),
        compiler_params=_params("arbitrary", "arbitrary", "arbitrary"),
        name="matmul_res",
    )(a, w, res)


MERGE_GATE_COLS = 1024
MERGE_GATE_BLOCKS = D_MODEL // MERGE_GATE_COLS


def _merge_kernel(oa_ref, ob_ref, oc_ref, wa_ref, wb_ref, wc_ref, *rest):
    gate_refs, y_ref = rest[:-1], rest[-1]
    branches = ((oa_ref, wa_ref), (ob_ref, wb_ref), (oc_ref, wc_ref))
    for c in range(MERGE_GATE_BLOCKS):
        cols = slice(c * MERGE_GATE_COLS, (c + 1) * MERGE_GATE_COLS)
        y = None
        for b, (o_ref, w_ref) in enumerate(branches):
            gate = _sigmoid(gate_refs[b * MERGE_GATE_BLOCKS + c][...].astype(f32))
            term = gate * _dot(o_ref[...], w_ref[:, cols].astype(bf16))
            y = term if y is None else y + term
        y_ref[:, cols] = y.astype(y_ref.dtype)


def _merge(oa, ob, oc, wa, wb, wc, z, l):
    m = oa.shape[0]
    gate0 = COL_GATES // MERGE_GATE_COLS
    branch_in = pl.BlockSpec((MERGE_TM, WIDTH), lambda i: (i, 0))
    branch_w = pl.BlockSpec((None, WIDTH, D_MODEL), lambda i: (l, 0, 0), pipeline_mode=pl.Buffered(1))
    gate_specs = [pl.BlockSpec((MERGE_TM, MERGE_GATE_COLS), lambda i, blk=gate0 + g: (i, blk))
                  for g in range(N_BRANCHES * MERGE_GATE_BLOCKS)]
    return pl.pallas_call(
        _merge_kernel,
        grid=(m // MERGE_TM,),
        in_specs=[branch_in, branch_in, branch_in, branch_w, branch_w, branch_w] + gate_specs,
        out_specs=pl.BlockSpec((MERGE_TM, D_MODEL), lambda i: (i, 0)),
        out_shape=jax.ShapeDtypeStruct((m, D_MODEL), bf16),
        compiler_params=_params("arbitrary"),
        name="merge",
    )(oa, ob, oc, wa, wb, wc, *([z] * (N_BRANCHES * MERGE_GATE_BLOCKS)))


def _gelu(x):
    return 0.5 * x * (1.0 + lax.erf(x * np.float32(np.sqrt(0.5))))


def _gmlp_kernel(zu_ref, zv_ref, g_ref, ws_ref, bs_ref, o_ref):
    u = _gelu(zu_ref[...].astype(f32))
    v = _gelu(zv_ref[...].astype(f32))
    mu = jnp.mean(v, axis=-1, keepdims=True)
    vc = v - mu
    var = jnp.mean(vc * vc, axis=-1, keepdims=True)
    vb = (vc * lax.rsqrt(var + EPS) * g_ref[...]).astype(bf16)

    t_chunk = _block_id(lax.broadcasted_iota(jnp.int32, (GMLP_BLOCK, GMLP_BLOCK), 0), CHUNK)
    s_chunk = _block_id(lax.broadcasted_iota(jnp.int32, (GMLP_BLOCK, GMLP_BLOCK), 1), CHUNK)
    causal = s_chunk <= t_chunk
    gdim = GMLP_WIDTH // GMLP_GROUPS
    for g in range(GMLP_GROUPS):
        ws = jnp.where(causal, ws_ref[g], 0.0).astype(bf16)
        cols = slice(g * gdim, (g + 1) * gdim)
        for blk in range(GMLP_ROWS // GMLP_BLOCK):
            rows = slice(blk * GMLP_BLOCK, (blk + 1) * GMLP_BLOCK)
            s = _dot(ws, vb[rows, cols]) + bs_ref[g]
            o_ref[rows, cols] = (u[rows, cols] * s).astype(o_ref.dtype)


def _gmlp(z, ln_gain, w_spatial, b_rep, l):
    m = z.shape[0]
    cu = COL_GMLP_UV // GMLP_WIDTH
    return pl.pallas_call(
        _gmlp_kernel,
        grid=(m // GMLP_ROWS,),
        in_specs=[
            pl.BlockSpec((GMLP_ROWS, GMLP_WIDTH), lambda i: (i, cu)),
            pl.BlockSpec((GMLP_ROWS, GMLP_WIDTH), lambda i: (i, cu + 1)),
            pl.BlockSpec((None, 1, GMLP_WIDTH), lambda i: (l, 0, 0)),
            pl.BlockSpec((None, GMLP_GROUPS, GMLP_BLOCK, GMLP_BLOCK), lambda i: (l, 0, 0, 0)),
            pl.BlockSpec((None, GMLP_GROUPS, GMLP_BLOCK, LANES), lambda i: (l, 0, 0, 0)),
        ],
        out_specs=pl.BlockSpec((GMLP_ROWS, GMLP_WIDTH), lambda i: (i, 0)),
        out_shape=jax.ShapeDtypeStruct((m, GMLP_WIDTH), bf16),
        compiler_params=_params("arbitrary"),
        name="gmlp",
    )(z, z, ln_gain, w_spatial, b_rep)


def _sba_kernel(q_ref, k_ref, v_ref, qg_ref, kg_ref, o_ref, qn_ref, kn_ref):
    t = q_ref.shape[0]
    head_cols = [slice(j * HEAD_DIM, (j + 1) * HEAD_DIM) for j in range(HEADS_PER_STEP)]
    for hc in head_cols:
        qn_ref[:, hc] = _rms_rows(q_ref[:, hc].astype(f32), qg_ref[...]).astype(bf16)
        kn_ref[:, hc] = _rms_rows(k_ref[:, hc].astype(f32), kg_ref[...]).astype(bf16)
    to_log2 = np.float32(HEAD_DIM ** -0.5 * LOG2_E)
    nsub = SBA_TK // SBA_SUB
    rows_all = HEADS_PER_STEP * SBA_TQ
    wr = lax.broadcasted_iota(jnp.int32, (2 * SBA_SUB, 2 * SBA_SUB), 0)
    wc = lax.broadcasted_iota(jnp.int32, (2 * SBA_SUB, 2 * SBA_SUB), 1)
    suffix_w = ((jnp.bitwise_and(wr, SBA_SUB - 1) > wc) | (wc >= SBA_SUB)).astype(bf16)

    def block(qbs, k0, rel, state):
        acc, run = state
        z = jnp.concatenate([_dot_nt(qbs[j], kn_ref[pl.ds(k0, SBA_TK), head_cols[j]])
                             for j in range(HEADS_PER_STEP)], axis=0) * to_log2
        neg_abs = pltpu.bitcast(pltpu.bitcast(z, jnp.uint32) | jnp.uint32(0x80000000), f32)
        soft = jnp.log(1.0 + jnp.exp2(neg_abs)) * np.float32(LOG2_E)
        drop = jnp.maximum(z, 0.0) + soft
        log_beta = z - drop
        probs = []
        offs = run
        for c in reversed(range(nsub)):
            cols = slice(c * SBA_SUB, (c + 1) * SBA_SUB)
            dc = drop[:, cols]
            if rel is not None:
                strict = (rel + (k0 + c * SBA_SUB)) < 0
                dc = jnp.where(strict, dc, 0.0)
            hi = dc.astype(bf16)
            lo = (dc - hi.astype(f32)).astype(bf16)
            sums = _dot(jnp.concatenate([hi, lo], axis=-1), suffix_w)
            a = jnp.exp2(log_beta[:, cols] - sums[:, :SBA_SUB] - offs)
            if rel is not None:
                a = jnp.where(strict, a, 0.0)
            probs.append(a.astype(bf16))
            offs = offs + sums[:, SBA_SUB:]
        a_blk = jnp.concatenate(probs[::-1], axis=-1)
        pv = jnp.concatenate(
            [_dot(a_blk[j * SBA_TQ:(j + 1) * SBA_TQ], v_ref[pl.ds(k0, SBA_TK), head_cols[j]])
             for j in range(HEADS_PER_STEP)], axis=0)
        return acc + pv, offs

    def q_block(qi, carry):
        q0 = pl.multiple_of(qi * SBA_TQ, SBA_TQ)
        qbs = [qn_ref[pl.ds(q0, SBA_TQ), hc] for hc in head_cols]
        rel = (lax.broadcasted_iota(jnp.int32, (rows_all, SBA_SUB), 1)
               - jnp.bitwise_and(lax.broadcasted_iota(jnp.int32, (rows_all, SBA_SUB), 0), SBA_TQ - 1)) - q0
        n_left = _block_id(q0, SBA_TK)
        k_diag = pl.multiple_of(n_left * SBA_TK, SBA_TK)
        zeros = jnp.zeros((rows_all, HEAD_DIM), f32)
        state = block(qbs, k_diag, rel, (zeros, zeros))

        def left_block(it, state):
            k0 = pl.multiple_of((n_left - 1 - it) * SBA_TK, SBA_TK)
            return block(qbs, k0, None, state)

        acc, _ = lax.fori_loop(0, n_left, left_block, state)
        for j in range(HEADS_PER_STEP):
            o_ref[pl.ds(q0, SBA_TQ), head_cols[j]] = acc[j * SBA_TQ:(j + 1) * SBA_TQ].astype(o_ref.dtype)
        return carry

    lax.fori_loop(0, t // SBA_TQ, q_block, 0)


def _sba(z, q_gain, k_gain, batch, seq, l):
    m = z.shape[0]
    c0 = COL_SBA_QKV // STEP_WIDTH
    per_seg = WIDTH // STEP_WIDTH

    def head_spec(seg):
        return pl.BlockSpec((seq, STEP_WIDTH), lambda b, h: (b, c0 + seg * per_seg + h))

    gain_spec = pl.BlockSpec((None, 1, HEAD_DIM), lambda b, h: (l, 0, 0))
    tok = pltpu.VMEM((seq, STEP_WIDTH), bf16)
    return pl.pallas_call(
        _sba_kernel,
        grid=(batch, per_seg),
        in_specs=[head_spec(0), head_spec(1), head_spec(2), gain_spec, gain_spec],
        out_specs=pl.BlockSpec((seq, STEP_WIDTH), lambda b, h: (b, h)),
        out_shape=jax.ShapeDtypeStruct((m, WIDTH), bf16),
        scratch_shapes=[tok, tok],
        compiler_params=_params("arbitrary", "arbitrary"),
        name="sba",
    )(z, z, z, q_gain, k_gain)


XPAD = BF16_ROWS


def _bmm(a, b):
    return jnp.einsum('bij,bjk->bik', a, b, preferred_element_type=f32)


def _bmm_nt(a, b):
    return jnp.einsum('bid,bjd->bij', a, b, preferred_element_type=f32)


def _bmm_tn(a, b):
    return jnp.einsum('bck,bcv->bkv', a, b, preferred_element_type=f32)


def _unit_lower_inverse(l_mat, row, col):
    def same_block(d):
        return _block_id(row, d) == _block_id(col, d)

    eye = (row == col).astype(f32)
    l8 = jnp.where(same_block(8), l_mat, 0.0)
    p = l8.astype(bf16)
    x = (eye - l8).astype(bf16)
    p2 = _bmm(p, p)
    x = _bmm(x, (eye + p2).astype(bf16)).astype(bf16)
    p2 = p2.astype(bf16)
    p4 = _bmm(p2, p2)
    x = _bmm(x, (eye + p4).astype(bf16)).astype(bf16)
    d = 8
    while d < CHUNK:
        off = jnp.where(same_block(2 * d) & jnp.logical_not(same_block(d)), l_mat, 0.0).astype(bf16)
        x = _bmm(x, (eye - _bmm(off, x)).astype(bf16)).astype(bf16)
        d *= 2
    return x


SCAN_ROWS = HEAD_DIM + CHUNK


def _gdn_kernel(zq_ref, zk_ref, zv_ref, zg_ref, gb_ref, cwq_ref, cwk_ref, cwv_ref, ng_ref, o_ref,
                xq_ref, xk_ref, xv_ref, stage_ref, lin_ref, off_ref, eg_ref):
    t = zq_ref.shape[0]
    head0 = pl.program_id(1) * HEADS_PER_STEP
    grp = GDN_GROUP
    nc = grp // CHUNK
    head_cols = [slice(j * HEAD_DIM, (j + 1) * HEAD_DIM) for j in range(HEADS_PER_STEP)]

    zeros_pad = jnp.zeros((XPAD, STEP_WIDTH), bf16)
    for src, dst in ((zq_ref, xq_ref), (zk_ref, xk_ref), (zv_ref, xv_ref)):
        dst[pl.ds(0, XPAD), :] = zeros_pad
        dst[pl.ds(XPAD, t), :] = src[...]

    row = lax.broadcasted_iota(jnp.int32, (CHUNK, CHUNK), 0)
    col = lax.broadcasted_iota(jnp.int32, (CHUNK, CHUNK), 1)
    incl = row >= col
    strict = row > col
    tri_ones = jnp.concatenate([incl.astype(bf16), jnp.ones((CHUNK, CHUNK), bf16)], axis=0)
    tri_ones = jnp.broadcast_to(tri_ones, (nc, 2 * CHUNK, CHUNK))
    sel_k = jnp.bitwise_and(lax.broadcasted_iota(jnp.int32, (3 * LANES, LANES), 0), LANES - 1)
    sel_rows = (jnp.bitwise_and(lax.broadcasted_iota(jnp.int32, (SUBLANES, 3 * LANES), 1), LANES - 1)
                == head0 + lax.broadcasted_iota(jnp.int32, (SUBLANES, 3 * LANES), 0)).astype(bf16)
    sel_rows = jnp.broadcast_to(sel_rows, (nc, SUBLANES, 3 * LANES))

    def conv_silu(x_ref, cw_ref, r0, hc, slot):
        stage = stage_ref.at[slot]
        stage[...] = x_ref[pl.ds(r0, grp + XPAD), hc].astype(f32)
        acc = cw_ref[pl.ds(CONV_WIDTH - 1, 1), hc] * stage[pl.ds(XPAD, grp), :]
        for i in range(CONV_WIDTH - 1):
            acc += cw_ref[pl.ds(i, 1), hc] * stage[pl.ds(XPAD - (CONV_WIDTH - 1) + i, grp), :]
        return _silu(acc)

    def l2n(x):
        return x * lax.rsqrt(jnp.sum(x * x, axis=-1, keepdims=True) + EPS)

    def chunked(x):
        return x.reshape(nc, CHUNK, x.shape[-1])

    def prepare(it, carry):
        r0 = pl.multiple_of(it * grp, grp)
        gb3 = _split3(gb_ref[pl.ds(r0, grp), :])
        ct = _bmm(tri_ones, chunked(gb3))
        cum3 = _split3(_sum3(ct[:, :CHUNK], LANES).reshape(grp, LANES))
        tot3 = _split3(_sum3(ct[:, CHUNK:], LANES).reshape(grp, LANES))
        gc_rows = _bmm_nt(sel_rows, chunked(cum3))

        k16s, kb16s, q16s, rhs16s, diffs, qds, kd16s = [], [], [], [], [], [], []
        for j, hc in enumerate(head_cols):
            pick_g = (sel_k == head0 + j).astype(bf16)
            pick_b = (sel_k == head0 + j + N_HEADS).astype(bf16)
            gc = _dot(cum3, pick_g)
            g_tot = _dot(tot3, pick_g)
            beta = _dot(gb3, pick_b)
            q = l2n(conv_silu(xq_ref, cwq_ref, r0, hc, 3 * j)) * np.float32(HEAD_DIM ** -0.5)
            k = l2n(conv_silu(xk_ref, cwk_ref, r0, hc, 3 * j + 1))
            v = conv_silu(xv_ref, cwv_ref, r0, hc, 3 * j + 2)
            kb = k * beta
            e_gc = jnp.exp(gc)
            k16s.append(chunked(k.astype(bf16)))
            kb16s.append(chunked(kb.astype(bf16)))
            q16s.append(chunked(q.astype(bf16)))
            rhs16s.append(chunked(jnp.concatenate([v * beta, kb * e_gc], axis=-1).astype(bf16)))
            diffs.append(chunked(gc)[:, :, :CHUNK] - gc_rows[:, j:j + 1, :])
            qds.append(chunked(q * e_gc))
            kd16s.append(chunked((k * jnp.exp(g_tot - gc)).astype(bf16)))
            e0 = pl.multiple_of(it * (nc * SUBLANES), nc * SUBLANES)
            eg_ref[pl.ds(e0, nc * SUBLANES), hc] = (
                chunked(jnp.exp(g_tot))[:, :SUBLANES, :].reshape(nc * SUBLANES, HEAD_DIM))

        cat = lambda xs: jnp.concatenate(xs, axis=0)
        k16, diff = cat(k16s), cat(diffs)
        decay = jnp.where(incl, jnp.exp(jnp.where(incl, diff, 0.0)), 0.0)
        l_mat = jnp.where(strict, _bmm_nt(cat(kb16s), k16) * decay, 0.0)
        intra = (_bmm_nt(cat(q16s), k16) * decay).astype(bf16)
        t_inv = _unit_lower_inverse(l_mat, row, col)
        uw = _bmm(t_inv, cat(rhs16s)).astype(bf16)
        state_map = _bmm_tn(cat(kd16s), uw)
        out_map = _bmm(intra, uw)
        q_eff = cat(qds) - out_map[:, :, HEAD_DIM:]
        lin = jnp.concatenate([state_map[:, :, HEAD_DIM:], q_eff], axis=1).astype(bf16)
        off = jnp.concatenate([state_map[:, :, :HEAD_DIM], out_map[:, :, :HEAD_DIM]], axis=1)
        s0 = pl.multiple_of(it * (nc * SCAN_ROWS), nc * SCAN_ROWS)
        for j, hc in enumerate(head_cols):
            mine = slice(j * nc, (j + 1) * nc)
            lin_ref[pl.ds(s0, nc * SCAN_ROWS), hc] = lin[mine].reshape(nc * SCAN_ROWS, HEAD_DIM)
            off_ref[pl.ds(s0, nc * SCAN_ROWS), hc] = off[mine].reshape(nc * SCAN_ROWS, HEAD_DIM)
        return carry

    lax.fori_loop(0, t // grp, prepare, 0)

    def scan(n, states):
        s0 = pl.multiple_of(n * SCAN_ROWS, SCAN_ROWS)
        e0 = pl.multiple_of(n * SUBLANES, SUBLANES)
        r0 = pl.multiple_of(n * CHUNK, CHUNK)
        new_states = []
        for j, hc in enumerate(head_cols):
            state = states[j]
            prod = _dot(lin_ref[pl.ds(s0, SCAN_ROWS), hc], state.astype(bf16))
            off = off_ref[pl.ds(s0, SCAN_ROWS), hc]
            new_states.append(state * eg_ref[pl.ds(e0, 1), hc] + (off[:HEAD_DIM] - prod[:HEAD_DIM]))
            o = prod[HEAD_DIM:] + off[HEAD_DIM:]
            gate = zg_ref[pl.ds(r0, CHUNK), hc].astype(f32)
            o_ref[pl.ds(r0, CHUNK), hc] = (_rms_rows(o, ng_ref[...]) * _silu(gate)).astype(o_ref.dtype)
        return tuple(new_states)

    zero_state = jnp.zeros((HEAD_DIM, HEAD_DIM), f32)
    lax.fori_loop(0, t // CHUNK, scan, (zero_state,) * HEADS_PER_STEP)


def _gdn(z, gb, conv_w, norm_gain, batch, seq, l):
    m = z.shape[0]
    per_seg = WIDTH // STEP_WIDTH
    cq = COL_GDN_QKV // STEP_WIDTH
    cg = COL_GDN_GATE // STEP_WIDTH

    def head_spec(c):
        return pl.BlockSpec((seq, STEP_WIDTH), lambda b, h: (b, c + h))

    def conv_spec(seg):
        return pl.BlockSpec((None, CONV_WIDTH, STEP_WIDTH), lambda b, h: (l, 0, seg * per_seg + h))

    n_chunks = seq // CHUNK
    padded = pltpu.VMEM((seq + XPAD, STEP_WIDTH), bf16)
    return pl.pallas_call(
        _gdn_kernel,
        grid=(batch, per_seg),
        in_specs=[
            head_spec(cq), head_spec(cq + per_seg), head_spec(cq + 2 * per_seg), head_spec(cg),
            pl.BlockSpec((seq, LANES), lambda b, h: (b, 0)),
            conv_spec(0), conv_spec(1), conv_spec(2),
            pl.BlockSpec((None, 1, HEAD_DIM), lambda b, h: (l, 0, 0)),
        ],
        out_specs=pl.BlockSpec((seq, STEP_WIDTH), lambda b, h: (b, h)),
        out_shape=jax.ShapeDtypeStruct((m, WIDTH), bf16),
        scratch_shapes=[
            padded, padded, padded,
            pltpu.VMEM((3 * HEADS_PER_STEP, GDN_GROUP + XPAD, HEAD_DIM), f32),
            pltpu.VMEM((n_chunks * SCAN_ROWS, STEP_WIDTH), bf16),
            pltpu.VMEM((n_chunks * SCAN_ROWS, STEP_WIDTH), f32),
            pltpu.VMEM((n_chunks * SUBLANES, STEP_WIDTH), f32),
        ],
        compiler_params=_params("arbitrary", "arbitrary"),
        name="gdn",
    )(z, z, z, z, gb, conv_w, conv_w, conv_w, norm_gain)


def _rows(v):
    return v.astype(f32)[:, None, :]


def _pad_rows(v):
    return _rows(jnp.pad(v, ((0, 0), (0, LANES - v.shape[1]))))


def kernel(x, w_in, conv_w, a_log, dt_bias, gdn_norm_g, gmlp_ln_g, w_spatial, b_spatial, sba_q_g, sba_k_g, w_out_a, w_out_b, w_out_c, w_out, norm_mix_g, norm_mlp_g, w_ff1, w_ff2):
    batch, seq, d = x.shape
    depth = w_in.shape[0]
    m = batch * seq

    o_a = 3 * WIDTH
    o_rest = o_a + 2 * N_HEADS
    w_tail = w_in[:, :, o_rest:].astype(bf16)
    w_ab = jnp.pad(w_in[:, :, o_a:o_rest], ((0, 0), (0, 0), (0, LANES - 2 * N_HEADS))).astype(bf16)
    b_rep = jnp.broadcast_to(b_spatial[..., None], b_spatial.shape + (LANES,)).astype(f32)
    mix_g, mlp_g, gdn_g, ln_g = _rows(norm_mix_g), _rows(norm_mlp_g), _rows(gdn_norm_g), _rows(gmlp_ln_g)
    q_g, k_g = _rows(sba_q_g), _rows(sba_k_g)
    alog_rows, dt_rows = _pad_rows(a_log), _pad_rows(dt_bias)
    conv_w = conv_w.astype(f32)
    w_spatial = w_spatial.astype(f32)

    xf = x.reshape(m, d).astype(f32)
    for l in range(depth):
        z, gb = _inproj(xf, mix_g, w_in, w_tail, w_ab, alog_rows, dt_rows, l)
        oa = _gdn(z, gb, conv_w, gdn_g, batch, seq, l)
        ob = _gmlp(z, ln_g, w_spatial, b_rep, l)
        oc = _sba(z, q_g, k_g, batch, seq, l)
        y = _merge(oa, ob, oc, w_out_a, w_out_b, w_out_c, z, l)
        xf = _matmul_res(y, w_out, xf, l)
        h1 = _ffn_up(xf, mlp_g, w_ff1, l)
        xf = _matmul_res(h1, w_ff2, xf, l)
    return xf.reshape(batch, seq, d).astype(x.dtype)
```

```python
import jax
import jax.numpy as jnp
import numpy as np
from jax import lax
from jax.experimental import pallas as pl
from jax.experimental.pallas import tpu as pltpu

f32 = jnp.float32
bf16 = jnp.bfloat16

D_MODEL = 2048
CHUNK = 64
HEAD_DIM = 128
N_HEADS = 8
WIDTH = N_HEADS * HEAD_DIM
CONV_WIDTH = 4
GMLP_WIDTH = 1024
GMLP_GROUPS = 8
GMLP_BLOCK = 128
N_BRANCHES = 3
EPS = 1e-6
LOG2_E = float(np.log2(np.e))
UNDERFLOW_LOG2 = 160.0

LANES = 128
SUBLANES = 8
BF16_ROWS = 16
VMEM_LIMIT_BYTES = 60000 * 1024

COL_GDN_QKV = 0
COL_GDN_GATE = 3 * WIDTH
COL_GMLP_UV = COL_GDN_GATE + WIDTH
COL_SBA_QKV = COL_GMLP_UV + 2 * GMLP_WIDTH
COL_GATES = COL_SBA_QKV + 3 * WIDTH
N_MAIN = COL_GATES + N_BRANCHES * D_MODEL
ORIG_COL_AB = 3 * WIDTH
N_AB = 2 * N_HEADS
D_IN = N_MAIN + N_AB

PROJ_TM = 2048
PROJ_TN = 512
RES_TM = 2048
RES_TN = 1024
RES_TK = 1024
MERGE_TM = 512

HEADS_PER_STEP = 4
STEP_WIDTH = HEADS_PER_STEP * HEAD_DIM
GDN_GROUP = 512
SBA_TQ = 128
SBA_TK = 512
SBA_SUB = 128
GMLP_ROWS = 512


def _params(*sem):
    return pltpu.CompilerParams(dimension_semantics=sem, vmem_limit_bytes=VMEM_LIMIT_BYTES)


def _dot(a, b):
    return jnp.dot(a, b, preferred_element_type=f32)


def _dot_nt(a, b):
    return lax.dot_general(a, b, (((1,), (1,)), ((), ())), preferred_element_type=f32)


def _sigmoid(x):
    return 1.0 / (1.0 + jnp.exp(-x))


def _silu(x):
    return x * _sigmoid(x)


def _softplus(x):
    return jnp.maximum(x, 0.0) + jnp.log1p(jnp.exp(-jnp.abs(x)))


def _block_id(idx, size):
    return lax.shift_right_logical(idx, int(size).bit_length() - 1)


def _split3(x):
    hi = x.astype(bf16)
    r1 = x - hi.astype(f32)
    mid = r1.astype(bf16)
    lo = (r1 - mid.astype(f32)).astype(bf16)
    return jnp.concatenate([hi, mid, lo], axis=-1)


def _sum3(x, n):
    return x[..., :n] + x[..., n:2 * n] + x[..., 2 * n:3 * n]


def _rms_rows(x, gain):
    ms = jnp.mean(x * x, axis=-1, keepdims=True)
    return x * lax.rsqrt(ms + EPS) * gain


def _inproj_kernel(x_ref, g_ref, wt_ref, wabt_ref, alog_ref, dt_ref, z_ref, gb_ref, hn_ref):
    @pl.when(pl.program_id(1) == 0)
    def _():
        hn = _rms_rows(x_ref[...], g_ref[...]).astype(bf16)
        hn_ref[...] = hn
        ab = _dot_nt(hn, wabt_ref[...].astype(bf16))
        lane = lax.broadcasted_iota(jnp.int32, ab.shape, 1)
        g = -jnp.exp(alog_ref[...]) * _softplus(ab + dt_ref[...])
        gb_ref[...] = jnp.where(lane < N_HEADS, g, _sigmoid(ab))

    z_ref[...] = _dot_nt(hn_ref[...], wt_ref[...].astype(bf16)).astype(z_ref.dtype)


def _inproj(x, gain, wt, alog_row, dt_row, l):
    m = x.shape[0]
    n_direct = ORIG_COL_AB // PROJ_TN

    def w_rows(i, j):
        return pl.multiple_of(l * D_IN + j * PROJ_TN + jnp.where(j >= n_direct, N_AB, 0), SUBLANES), 0

    return pl.pallas_call(
        _inproj_kernel,
        grid=(m // PROJ_TM, N_MAIN // PROJ_TN),
        in_specs=[
            pl.BlockSpec((PROJ_TM, D_MODEL), lambda i, j: (i, 0), pipeline_mode=pl.Buffered(1)),
            pl.BlockSpec((None, 1, D_MODEL), lambda i, j: (l, 0, 0)),
            pl.BlockSpec((pl.Element(PROJ_TN), pl.Element(D_MODEL)), w_rows),
            pl.BlockSpec((pl.Element(LANES), pl.Element(D_MODEL)), lambda i, j: (l * D_IN + ORIG_COL_AB, 0)),
            pl.BlockSpec((None, 1, LANES), lambda i, j: (l, 0, 0)),
            pl.BlockSpec((None, 1, LANES), lambda i, j: (l, 0, 0)),
        ],
        out_specs=[
            pl.BlockSpec((PROJ_TM, PROJ_TN), lambda i, j: (i, j)),
            pl.BlockSpec((PROJ_TM, LANES), lambda i, j: (i, 0)),
        ],
        out_shape=[
            jax.ShapeDtypeStruct((m, N_MAIN), bf16),
            jax.ShapeDtypeStruct((m, LANES), f32),
        ],
        scratch_shapes=[pltpu.VMEM((PROJ_TM, D_MODEL), bf16)],
        compiler_params=_params("arbitrary", "arbitrary"),
        name="inproj",
    )(x, gain, wt, wt, alog_row, dt_row)


def _ffn_up_kernel(x_ref, g_ref, w_ref, h_ref, hn_ref):
    @pl.when(pl.program_id(1) == 0)
    def _():
        hn_ref[...] = _rms_rows(x_ref[...], g_ref[...]).astype(bf16)

    a = jnp.maximum(_dot(hn_ref[...], w_ref[...].astype(bf16)), 0.0)
    h_ref[...] = (a * a).astype(h_ref.dtype)


def _ffn_up(x, gain, w1, l):
    m, n = x.shape[0], w1.shape[-1]
    return pl.pallas_call(
        _ffn_up_kernel,
        grid=(m // PROJ_TM, n // PROJ_TN),
        in_specs=[
            pl.BlockSpec((PROJ_TM, D_MODEL), lambda i, j: (i, 0), pipeline_mode=pl.Buffered(1)),
            pl.BlockSpec((None, 1, D_MODEL), lambda i, j: (l, 0, 0)),
            pl.BlockSpec((None, D_MODEL, PROJ_TN), lambda i, j: (l, 0, j)),
        ],
        out_specs=pl.BlockSpec((PROJ_TM, PROJ_TN), lambda i, j: (i, j)),
        out_shape=jax.ShapeDtypeStruct((m, n), bf16),
        scratch_shapes=[pltpu.VMEM((PROJ_TM, D_MODEL), bf16)],
        compiler_params=_params("arbitrary", "arbitrary"),
        name="ffn_up",
    )(x, gain, w1)


def _matmul_res_kernel(a_ref, w_ref, r_ref, o_ref):
    d = _dot(a_ref[...], w_ref[...].astype(bf16))

    @pl.when(pl.program_id(2) == 0)
    def _():
        o_ref[...] = r_ref[...] + d

    @pl.when(pl.program_id(2) != 0)
    def _():
        o_ref[...] += d


def _matmul_res(a, w, res, l):
    m, k = a.shape
    n = w.shape[-1]
    return pl.pallas_call(
        _matmul_res_kernel,
        grid=(m // RES_TM, n // RES_TN, k // RES_TK),
        in_specs=[
            pl.BlockSpec((RES_TM, RES_TK), lambda i, j, kk: (i, kk)),
            pl.BlockSpec((None, RES_TK, RES_TN), lambda i, j, kk: (l, kk, j)),
            pl.BlockSpec((RES_TM, RES_TN), lambda i, j, kk: (i, j)),
        ],
        out_specs=pl.BlockSpec((RES_TM, RES_TN), lambda i, j, kk: (i, j)),
        out_shape=jax.ShapeDtypeStruct((m, n), f32),
        compiler_params=_params("arbitrary", "arbitrary", "arbitrary"),
        name="matmul_res",
    )(a, w, res)


MERGE_GATE_COLS = 1024
MERGE_GATE_BLOCKS = D_MODEL // MERGE_GATE_COLS


def _merge_kernel(oa_ref, ob_ref, oc_ref, wa_ref, wb_ref, wc_ref, *rest):
    gate_refs, y_ref = rest[:-1], rest[-1]
    branches = ((oa_ref, wa_ref), (ob_ref, wb_ref), (oc_ref, wc_ref))
    for c in range(MERGE_GATE_BLOCKS):
        cols = slice(c * MERGE_GATE_COLS, (c + 1) * MERGE_GATE_COLS)
        y = None
        for b, (o_ref, w_ref) in enumerate(branches):
            gate = _sigmoid(gate_refs[b * MERGE_GATE_BLOCKS + c][...].astype(f32))
            term = gate * _dot(o_ref[...], w_ref[:, cols].astype(bf16))
            y = term if y is None else y + term
        y_ref[:, cols] = y.astype(y_ref.dtype)


def _merge(oa, ob, oc, wa, wb, wc, z, l):
    m = oa.shape[0]
    gate0 = COL_GATES // MERGE_GATE_COLS
    branch_in = pl.BlockSpec((MERGE_TM, WIDTH), lambda i: (i, 0))
    branch_w = pl.BlockSpec((None, WIDTH, D_MODEL), lambda i: (l, 0, 0), pipeline_mode=pl.Buffered(1))
    gate_specs = [pl.BlockSpec((MERGE_TM, MERGE_GATE_COLS), lambda i, blk=gate0 + g: (i, blk))
                  for g in range(N_BRANCHES * MERGE_GATE_BLOCKS)]
    return pl.pallas_call(
        _merge_kernel,
        grid=(m // MERGE_TM,),
        in_specs=[branch_in, branch_in, branch_in, branch_w, branch_w, branch_w] + gate_specs,
        out_specs=pl.BlockSpec((MERGE_TM, D_MODEL), lambda i: (i, 0)),
        out_shape=jax.ShapeDtypeStruct((m, D_MODEL), bf16),
        compiler_params=_params("arbitrary"),
        name="merge",
    )(oa, ob, oc, wa, wb, wc, *([z] * (N_BRANCHES * MERGE_GATE_BLOCKS)))


def _gelu(x):
    return 0.5 * x * (1.0 + lax.erf(x * np.float32(np.sqrt(0.5))))


def _gmlp_kernel(zu_ref, zv_ref, g_ref, ws_ref, bs_ref, o_ref):
    u = _gelu(zu_ref[...].astype(f32))
    v = _gelu(zv_ref[...].astype(f32))
    mu = jnp.mean(v, axis=-1, keepdims=True)
    vc = v - mu
    var = jnp.mean(vc * vc, axis=-1, keepdims=True)
    vb = (vc * lax.rsqrt(var + EPS) * g_ref[...]).astype(bf16)

    t_chunk = _block_id(lax.broadcasted_iota(jnp.int32, (GMLP_BLOCK, GMLP_BLOCK), 0), CHUNK)
    s_chunk = _block_id(lax.broadcasted_iota(jnp.int32, (GMLP_BLOCK, GMLP_BLOCK), 1), CHUNK)
    causal = s_chunk <= t_chunk
    gdim = GMLP_WIDTH // GMLP_GROUPS
    for g in range(GMLP_GROUPS):
        ws = jnp.where(causal, ws_ref[g], 0.0).astype(bf16)
        cols = slice(g * gdim, (g + 1) * gdim)
        for blk in range(GMLP_ROWS // GMLP_BLOCK):
            rows = slice(blk * GMLP_BLOCK, (blk + 1) * GMLP_BLOCK)
            s = _dot(ws, vb[rows, cols]) + bs_ref[g]
            o_ref[rows, cols] = (u[rows, cols] * s).astype(o_ref.dtype)


def _gmlp(z, ln_gain, w_spatial, b_rep, l):
    m = z.shape[0]
    cu = COL_GMLP_UV // GMLP_WIDTH
    return pl.pallas_call(
        _gmlp_kernel,
        grid=(m // GMLP_ROWS,),
        in_specs=[
            pl.BlockSpec((GMLP_ROWS, GMLP_WIDTH), lambda i: (i, cu)),
            pl.BlockSpec((GMLP_ROWS, GMLP_WIDTH), lambda i: (i, cu + 1)),
            pl.BlockSpec((None, 1, GMLP_WIDTH), lambda i: (l, 0, 0)),
            pl.BlockSpec((None, GMLP_GROUPS, GMLP_BLOCK, GMLP_BLOCK), lambda i: (l, 0, 0, 0)),
            pl.BlockSpec((None, GMLP_GROUPS, GMLP_BLOCK, LANES), lambda i: (l, 0, 0, 0)),
        ],
        out_specs=pl.BlockSpec((GMLP_ROWS, GMLP_WIDTH), lambda i: (i, 0)),
        out_shape=jax.ShapeDtypeStruct((m, GMLP_WIDTH), bf16),
        compiler_params=_params("arbitrary"),
        name="gmlp",
    )(z, z, ln_gain, w_spatial, b_rep)


def _sba_kernel(q_ref, k_ref, v_ref, qg_ref, kg_ref, o_ref, qn_ref, kn_ref):
    t = q_ref.shape[0]
    head_cols = [slice(j * HEAD_DIM, (j + 1) * HEAD_DIM) for j in range(HEADS_PER_STEP)]
    for hc in head_cols:
        qn_ref[:, hc] = _rms_rows(q_ref[:, hc].astype(f32), qg_ref[...]).astype(bf16)
        kn_ref[:, hc] = _rms_rows(k_ref[:, hc].astype(f32), kg_ref[...]).astype(bf16)
    to_log2 = np.float32(HEAD_DIM ** -0.5 * LOG2_E)
    rows_all = HEADS_PER_STEP * SBA_TQ
    wr = lax.broadcasted_iota(jnp.int32, (2 * SBA_SUB, 2 * SBA_SUB), 0)
    wc = lax.broadcasted_iota(jnp.int32, (2 * SBA_SUB, 2 * SBA_SUB), 1)
    suffix_w = ((jnp.bitwise_and(wr, SBA_SUB - 1) > wc) | (wc >= SBA_SUB)).astype(bf16)

    below_diag = (lax.broadcasted_iota(jnp.int32, (rows_all, SBA_SUB), 1)
                  < jnp.bitwise_and(lax.broadcasted_iota(jnp.int32, (rows_all, SBA_SUB), 0), SBA_TQ - 1))

    def block(qbs, k0, n_cols, diagonal, state):
        acc, run = state
        z = jnp.concatenate([_dot_nt(qbs[j], kn_ref[pl.ds(k0, n_cols), head_cols[j]])
                             for j in range(HEADS_PER_STEP)], axis=0) * to_log2
        neg_abs = pltpu.bitcast(pltpu.bitcast(z, jnp.uint32) | jnp.uint32(0x80000000), f32)
        soft = jnp.log(1.0 + jnp.exp2(neg_abs)) * np.float32(LOG2_E)
        drop = jnp.maximum(z, 0.0) + soft
        log_beta = z - drop
        probs = []
        offs = run
        last = n_cols // SBA_SUB - 1
        for c in range(last, -1, -1):
            cols = slice(c * SBA_SUB, (c + 1) * SBA_SUB)
            masked = diagonal and c == last
            dc = drop[:, cols]
            if masked:
                dc = jnp.where(below_diag, dc, 0.0)
            hi = dc.astype(bf16)
            lo = (dc - hi.astype(f32)).astype(bf16)
            sums = _dot(jnp.concatenate([hi, lo], axis=-1), suffix_w)
            a = jnp.exp2(log_beta[:, cols] - sums[:, :SBA_SUB] - offs)
            if masked:
                a = jnp.where(below_diag, a, 0.0)
            probs.append(a.astype(bf16))
            offs = offs + sums[:, SBA_SUB:]
        a_blk = probs[0] if last == 0 else jnp.concatenate(probs[::-1], axis=-1)
        pv = jnp.concatenate(
            [_dot(a_blk[j * SBA_TQ:(j + 1) * SBA_TQ], v_ref[pl.ds(k0, n_cols), head_cols[j]])
             for j in range(HEADS_PER_STEP)], axis=0)
        return acc + pv, offs

    def key_block_row(s, carry):
        k_diag = pl.multiple_of(s * SBA_TK, SBA_TK)
        for p in range(SBA_TK // SBA_TQ):
            q0 = pl.multiple_of(k_diag + p * SBA_TQ, SBA_TQ)
            qbs = [qn_ref[pl.ds(q0, SBA_TQ), hc] for hc in head_cols]
            zeros = jnp.zeros((rows_all, HEAD_DIM), f32)
            state = block(qbs, k_diag, (p + 1) * SBA_SUB, True, (zeros, zeros))

            def more(c):
                it, _, run = c
                return (it < s) & (jnp.min(run) <= UNDERFLOW_LOG2)

            def left_block(c):
                it, acc, run = c
                k0 = pl.multiple_of((s - 1 - it) * SBA_TK, SBA_TK)
                acc, run = block(qbs, k0, SBA_TK, False, (acc, run))
                return it + 1, acc, run

            _, acc, _ = lax.while_loop(more, left_block, (jnp.int32(0),) + state)
            for j in range(HEADS_PER_STEP):
                o_ref[pl.ds(q0, SBA_TQ), head_cols[j]] = acc[j * SBA_TQ:(j + 1) * SBA_TQ].astype(o_ref.dtype)
        return carry

    lax.fori_loop(0, t // SBA_TK, key_block_row, 0)


def _sba(z, q_gain, k_gain, batch, seq, l):
    m = z.shape[0]
    c0 = COL_SBA_QKV // STEP_WIDTH
    per_seg = WIDTH // STEP_WIDTH

    def head_spec(seg):
        return pl.BlockSpec((seq, STEP_WIDTH), lambda b, h: (b, c0 + seg * per_seg + h))

    gain_spec = pl.BlockSpec((None, 1, HEAD_DIM), lambda b, h: (l, 0, 0))
    tok = pltpu.VMEM((seq, STEP_WIDTH), bf16)
    return pl.pallas_call(
        _sba_kernel,
        grid=(batch, per_seg),
        in_specs=[head_spec(0), head_spec(1), head_spec(2), gain_spec, gain_spec],
        out_specs=pl.BlockSpec((seq, STEP_WIDTH), lambda b, h: (b, h)),
        out_shape=jax.ShapeDtypeStruct((m, WIDTH), bf16),
        scratch_shapes=[tok, tok],
        compiler_params=_params("arbitrary", "arbitrary"),
        name="sba",
    )(z, z, z, q_gain, k_gain)


XPAD = BF16_ROWS


def _bmm(a, b):
    return jnp.einsum('bij,bjk->bik', a, b, preferred_element_type=f32)


def _bmm_nt(a, b):
    return jnp.einsum('bid,bjd->bij', a, b, preferred_element_type=f32)


def _bmm_tn(a, b):
    return jnp.einsum('bck,bcv->bkv', a, b, preferred_element_type=f32)


def _unit_lower_inverse(l_mat, row, col):
    def same_block(d):
        return _block_id(row, d) == _block_id(col, d)

    eye = (row == col).astype(f32)
    l8 = jnp.where(same_block(8), l_mat, 0.0)
    p = l8.astype(bf16)
    x = (eye - l8).astype(bf16)
    p2 = _bmm(p, p)
    x = _bmm(x, (eye + p2).astype(bf16)).astype(bf16)
    p2 = p2.astype(bf16)
    p4 = _bmm(p2, p2)
    x = _bmm(x, (eye + p4).astype(bf16)).astype(bf16)
    d = 8
    while d < CHUNK:
        off = jnp.where(same_block(2 * d) & jnp.logical_not(same_block(d)), l_mat, 0.0).astype(bf16)
        x = _bmm(x, (eye - _bmm(off, x)).astype(bf16)).astype(bf16)
        d *= 2
    return x


SCAN_ROWS = HEAD_DIM + CHUNK


def _gdn_kernel(zq_ref, zk_ref, zv_ref, zg_ref, gb_ref, cwq_ref, cwk_ref, cwv_ref, ng_ref, o_ref,
                xq_ref, xk_ref, xv_ref, stage_ref, lin_ref, off_ref, eg_ref):
    t = zq_ref.shape[0]
    head0 = pl.program_id(1) * HEADS_PER_STEP
    grp = GDN_GROUP
    nc = grp // CHUNK
    head_cols = [slice(j * HEAD_DIM, (j + 1) * HEAD_DIM) for j in range(HEADS_PER_STEP)]

    zeros_pad = jnp.zeros((XPAD, STEP_WIDTH), bf16)
    for src, dst in ((zq_ref, xq_ref), (zk_ref, xk_ref), (zv_ref, xv_ref)):
        dst[pl.ds(0, XPAD), :] = zeros_pad
        dst[pl.ds(XPAD, t), :] = src[...]

    row = lax.broadcasted_iota(jnp.int32, (CHUNK, CHUNK), 0)
    col = lax.broadcasted_iota(jnp.int32, (CHUNK, CHUNK), 1)
    incl = row >= col
    strict = row > col
    tri_ones = jnp.concatenate([incl.astype(bf16), jnp.ones((CHUNK, CHUNK), bf16)], axis=0)
    tri_ones = jnp.broadcast_to(tri_ones, (nc, 2 * CHUNK, CHUNK))
    sel_k = jnp.bitwise_and(lax.broadcasted_iota(jnp.int32, (3 * LANES, LANES), 0), LANES - 1)
    sel_rows = (jnp.bitwise_and(lax.broadcasted_iota(jnp.int32, (SUBLANES, 3 * LANES), 1), LANES - 1)
                == head0 + lax.broadcasted_iota(jnp.int32, (SUBLANES, 3 * LANES), 0)).astype(bf16)
    sel_rows = jnp.broadcast_to(sel_rows, (nc, SUBLANES, 3 * LANES))

    def conv_silu(x_ref, cw_ref, r0, hc, slot):
        stage = stage_ref.at[slot]
        stage[...] = x_ref[pl.ds(r0, grp + XPAD), hc].astype(f32)
        acc = cw_ref[pl.ds(CONV_WIDTH - 1, 1), hc] * stage[pl.ds(XPAD, grp), :]
        for i in range(CONV_WIDTH - 1):
            acc += cw_ref[pl.ds(i, 1), hc] * stage[pl.ds(XPAD - (CONV_WIDTH - 1) + i, grp), :]
        return _silu(acc)

    def l2n(x):
        return x * lax.rsqrt(jnp.sum(x * x, axis=-1, keepdims=True) + EPS)

    def chunked(x):
        return x.reshape(nc, CHUNK, x.shape[-1])

    def prepare(it, carry):
        r0 = pl.multiple_of(it * grp, grp)
        gb3 = _split3(gb_ref[pl.ds(r0, grp), :])
        ct = _bmm(tri_ones, chunked(gb3))
        cum3 = _split3(_sum3(ct[:, :CHUNK], LANES).reshape(grp, LANES))
        tot3 = _split3(_sum3(ct[:, CHUNK:], LANES).reshape(grp, LANES))
        gc_rows = _bmm_nt(sel_rows, chunked(cum3))

        k16s, kb16s, q16s, rhs16s, diffs, qds, kd16s = [], [], [], [], [], [], []
        for j, hc in enumerate(head_cols):
            pick_g = (sel_k == head0 + j).astype(bf16)
            pick_b = (sel_k == head0 + j + N_HEADS).astype(bf16)
            gc = _dot(cum3, pick_g)
            g_tot = _dot(tot3, pick_g)
            beta = _dot(gb3, pick_b)
            q = l2n(conv_silu(xq_ref, cwq_ref, r0, hc, 3 * j)) * np.float32(HEAD_DIM ** -0.5)
            k = l2n(conv_silu(xk_ref, cwk_ref, r0, hc, 3 * j + 1))
            v = conv_silu(xv_ref, cwv_ref, r0, hc, 3 * j + 2)
            kb = k * beta
            e_gc = jnp.exp(gc)
            k16s.append(chunked(k.astype(bf16)))
            kb16s.append(chunked(kb.astype(bf16)))
            q16s.append(chunked(q.astype(bf16)))
            rhs16s.append(chunked(jnp.concatenate([v * beta, kb * e_gc], axis=-1).astype(bf16)))
            diffs.append(chunked(gc)[:, :, :CHUNK] - gc_rows[:, j:j + 1, :])
            qds.append(chunked(q * e_gc))
            kd16s.append(chunked((k * jnp.exp(g_tot - gc)).astype(bf16)))
            e0 = pl.multiple_of(it * (nc * SUBLANES), nc * SUBLANES)
            eg_ref[pl.ds(e0, nc * SUBLANES), hc] = (
                chunked(jnp.exp(g_tot))[:, :SUBLANES, :].reshape(nc * SUBLANES, HEAD_DIM))

        cat = lambda xs: jnp.concatenate(xs, axis=0)
        k16, diff = cat(k16s), cat(diffs)
        decay = jnp.where(incl, jnp.exp(jnp.where(incl, diff, 0.0)), 0.0)
        l_mat = jnp.where(strict, _bmm_nt(cat(kb16s), k16) * decay, 0.0)
        intra = (_bmm_nt(cat(q16s), k16) * decay).astype(bf16)
        t_inv = _unit_lower_inverse(l_mat, row, col)
        uw = _bmm(t_inv, cat(rhs16s)).astype(bf16)
        state_map = _bmm_tn(cat(kd16s), uw)
        out_map = _bmm(intra, uw)
        q_eff = cat(qds) - out_map[:, :, HEAD_DIM:]
        lin = jnp.concatenate([state_map[:, :, HEAD_DIM:], q_eff], axis=1).astype(bf16)
        off = jnp.concatenate([state_map[:, :, :HEAD_DIM], out_map[:, :, :HEAD_DIM]], axis=1)
        s0 = pl.multiple_of(it * (nc * SCAN_ROWS), nc * SCAN_ROWS)
        for j, hc in enumerate(head_cols):
            mine = slice(j * nc, (j + 1) * nc)
            lin_ref[pl.ds(s0, nc * SCAN_ROWS), hc] = lin[mine].reshape(nc * SCAN_ROWS, HEAD_DIM)
            off_ref[pl.ds(s0, nc * SCAN_ROWS), hc] = off[mine].reshape(nc * SCAN_ROWS, HEAD_DIM)
        return carry

    lax.fori_loop(0, t // grp, prepare, 0)

    def scan(n, states):
        s0 = pl.multiple_of(n * SCAN_ROWS, SCAN_ROWS)
        e0 = pl.multiple_of(n * SUBLANES, SUBLANES)
        r0 = pl.multiple_of(n * CHUNK, CHUNK)
        new_states = []
        for j, hc in enumerate(head_cols):
            state = states[j]
            prod = _dot(lin_ref[pl.ds(s0, SCAN_ROWS), hc], state.astype(bf16))
            off = off_ref[pl.ds(s0, SCAN_ROWS), hc]
            new_states.append(state * eg_ref[pl.ds(e0, 1), hc] + (off[:HEAD_DIM] - prod[:HEAD_DIM]))
            o = prod[HEAD_DIM:] + off[HEAD_DIM:]
            gate = zg_ref[pl.ds(r0, CHUNK), hc].astype(f32)
            o_ref[pl.ds(r0, CHUNK), hc] = (_rms_rows(o, ng_ref[...]) * _silu(gate)).astype(o_ref.dtype)
        return tuple(new_states)

    zero_state = jnp.zeros((HEAD_DIM, HEAD_DIM), f32)
    lax.fori_loop(0, t // CHUNK, scan, (zero_state,) * HEADS_PER_STEP)


def _gdn(z, gb, conv_w, norm_gain, batch, seq, l):
    m = z.shape[0]
    per_seg = WIDTH // STEP_WIDTH
    cq = COL_GDN_QKV // STEP_WIDTH
    cg = COL_GDN_GATE // STEP_WIDTH

    def head_spec(c):
        return pl.BlockSpec((seq, STEP_WIDTH), lambda b, h: (b, c + h))

    def conv_spec(seg):
        return pl.BlockSpec((None, CONV_WIDTH, STEP_WIDTH), lambda b, h: (l, 0, seg * per_seg + h))

    n_chunks = seq // CHUNK
    padded = pltpu.VMEM((seq + XPAD, STEP_WIDTH), bf16)
    return pl.pallas_call(
        _gdn_kernel,
        grid=(batch, per_seg),
        in_specs=[
            head_spec(cq), head_spec(cq + per_seg), head_spec(cq + 2 * per_seg), head_spec(cg),
            pl.BlockSpec((seq, LANES), lambda b, h: (b, 0)),
            conv_spec(0), conv_spec(1), conv_spec(2),
            pl.BlockSpec((None, 1, HEAD_DIM), lambda b, h: (l, 0, 0)),
        ],
        out_specs=pl.BlockSpec((seq, STEP_WIDTH), lambda b, h: (b, h)),
        out_shape=jax.ShapeDtypeStruct((m, WIDTH), bf16),
        scratch_shapes=[
            padded, padded, padded,
            pltpu.VMEM((3 * HEADS_PER_STEP, GDN_GROUP + XPAD, HEAD_DIM), f32),
            pltpu.VMEM((n_chunks * SCAN_ROWS, STEP_WIDTH), bf16),
            pltpu.VMEM((n_chunks * SCAN_ROWS, STEP_WIDTH), f32),
            pltpu.VMEM((n_chunks * SUBLANES, STEP_WIDTH), f32),
        ],
        compiler_params=_params("arbitrary", "arbitrary"),
        name="gdn",
    )(z, z, z, z, gb, conv_w, conv_w, conv_w, norm_gain)


def _rows(v):
    return v.astype(f32)[:, None, :]


def _pad_rows(v):
    return _rows(jnp.pad(v, ((0, 0), (0, LANES - v.shape[1]))))


def kernel(x, w_in, conv_w, a_log, dt_bias, gdn_norm_g, gmlp_ln_g, w_spatial, b_spatial, sba_q_g, sba_k_g, w_out_a, w_out_b, w_out_c, w_out, norm_mix_g, norm_mlp_g, w_ff1, w_ff2):
    batch, seq, d = x.shape
    depth = w_in.shape[0]
    m = batch * seq

    wt = jnp.swapaxes(w_in, 1, 2).reshape(depth * D_IN, d)
    b_rep = jnp.broadcast_to(b_spatial[..., None], b_spatial.shape + (LANES,)).astype(f32)
    mix_g, mlp_g, gdn_g, ln_g = _rows(norm_mix_g), _rows(norm_mlp_g), _rows(gdn_norm_g), _rows(gmlp_ln_g)
    q_g, k_g = _rows(sba_q_g), _rows(sba_k_g)
    alog_rows, dt_rows = _pad_rows(a_log), _pad_rows(dt_bias)
    conv_w = conv_w.astype(f32)
    w_spatial = w_spatial.astype(f32)

    xf = x.reshape(m, d).astype(f32)
    for l in range(depth):
        z, gb = _inproj(xf, mix_g, wt, alog_rows, dt_rows, l)
        oa = _gdn(z, gb, conv_w, gdn_g, batch, seq, l)
        ob = _gmlp(z, ln_g, w_spatial, b_rep, l)
        oc = _sba(z, q_g, k_g, batch, seq, l)
        y = _merge(oa, ob, oc, w_out_a, w_out_b, w_out_c, z, l)
        xf = _matmul_res(y, w_out, xf, l)
        h1 = _ffn_up(xf, mlp_g, w_ff1, l)
        xf = _matmul_res(h1, w_ff2, xf, l)
    return xf.reshape(batch, seq, d).astype(x.dtype)
```

```python
import jax
import jax.numpy as jnp
import numpy as np
from jax import lax
from jax.experimental import pallas as pl
from jax.experimental.pallas import tpu as pltpu

f32 = jnp.float32
bf16 = jnp.bfloat16

D_MODEL = 2048
CHUNK = 64
HEAD_DIM = 128
N_HEADS = 8
WIDTH = N_HEADS * HEAD_DIM
CONV_WIDTH = 4
GMLP_WIDTH = 1024
GMLP_GROUPS = 8
GMLP_BLOCK = 128
N_BRANCHES = 3
EPS = 1e-6
LOG2_E = float(np.log2(np.e))
UNDERFLOW_LOG2 = 160.0

LANES = 128
SUBLANES = 8
BF16_ROWS = 16
VMEM_LIMIT_BYTES = 60000 * 1024

COL_GDN_QKV = 0
COL_GDN_GATE = 3 * WIDTH
COL_GMLP_UV = COL_GDN_GATE + WIDTH
COL_SBA_QKV = COL_GMLP_UV + 2 * GMLP_WIDTH
COL_GATES = COL_SBA_QKV + 3 * WIDTH
N_MAIN = COL_GATES + N_BRANCHES * D_MODEL
ORIG_COL_AB = 3 * WIDTH
N_AB = 2 * N_HEADS
D_IN = N_MAIN + N_AB

PROJ_TM = 2048
PROJ_TN = 512
RES_TM = 2048
RES_TN = 1024
RES_TK = 1024
MERGE_TM = 512

HEADS_PER_STEP = 4
STEP_WIDTH = HEADS_PER_STEP * HEAD_DIM
GDN_GROUP = 512
SBA_TQ = 128
SBA_TK = 512
SBA_SUB = 128
GMLP_ROWS = 512


def _params(*sem):
    return pltpu.CompilerParams(dimension_semantics=sem, vmem_limit_bytes=VMEM_LIMIT_BYTES)


def _dot(a, b):
    return jnp.dot(a, b, preferred_element_type=f32)


def _dot_nt(a, b):
    return lax.dot_general(a, b, (((1,), (1,)), ((), ())), preferred_element_type=f32)


def _sigmoid(x):
    return lax.logistic(x)


def _silu(x):
    return x * _sigmoid(x)


def _softplus(x):
    return jnp.maximum(x, 0.0) + jnp.log1p(jnp.exp(-jnp.abs(x)))


def _block_id(idx, size):
    return lax.shift_right_logical(idx, int(size).bit_length() - 1)


def _split3(x):
    hi = x.astype(bf16)
    r1 = x - hi.astype(f32)
    mid = r1.astype(bf16)
    lo = (r1 - mid.astype(f32)).astype(bf16)
    return jnp.concatenate([hi, mid, lo], axis=-1)


def _sum3(x, n):
    return x[..., :n] + x[..., n:2 * n] + x[..., 2 * n:3 * n]


def _rms_rows(x, gain):
    ms = jnp.mean(x * x, axis=-1, keepdims=True)
    return x * lax.rsqrt(ms + EPS) * gain


NORM_ROWS = 256


def _for_row_chunks(n_rows, body):
    def step(c, carry):
        body(pl.ds(pl.multiple_of(c * NORM_ROWS, NORM_ROWS), NORM_ROWS))
        return carry

    lax.fori_loop(0, n_rows // NORM_ROWS, step, 0)


def _inproj_kernel(x_ref, g_ref, wt_ref, wabt_ref, alog_ref, dt_ref, z_ref, gb_ref, hn_ref):
    @pl.when(pl.program_id(1) == 0)
    def _():
        wab = wabt_ref[...].astype(bf16)

        def rows_step(rows):
            hn = _rms_rows(x_ref[rows, :], g_ref[...]).astype(bf16)
            hn_ref[rows, :] = hn
            ab = _dot_nt(hn, wab)
            lane = lax.broadcasted_iota(jnp.int32, ab.shape, 1)
            g = -jnp.exp(alog_ref[...]) * _softplus(ab + dt_ref[...])
            gb_ref[rows, :] = jnp.where(lane < N_HEADS, g, _sigmoid(ab))

        _for_row_chunks(x_ref.shape[0], rows_step)

    z_ref[...] = _dot_nt(hn_ref[...], wt_ref[...].astype(bf16)).astype(z_ref.dtype)


def _inproj(x, gain, wt, alog_row, dt_row, l):
    m = x.shape[0]
    n_direct = ORIG_COL_AB // PROJ_TN

    def w_rows(i, j):
        return pl.multiple_of(l * D_IN + j * PROJ_TN + jnp.where(j >= n_direct, N_AB, 0), SUBLANES), 0

    return pl.pallas_call(
        _inproj_kernel,
        grid=(m // PROJ_TM, N_MAIN // PROJ_TN),
        in_specs=[
            pl.BlockSpec((PROJ_TM, D_MODEL), lambda i, j: (i, 0), pipeline_mode=pl.Buffered(1)),
            pl.BlockSpec((None, 1, D_MODEL), lambda i, j: (l, 0, 0)),
            pl.BlockSpec((pl.Element(PROJ_TN), pl.Element(D_MODEL)), w_rows),
            pl.BlockSpec((pl.Element(LANES), pl.Element(D_MODEL)), lambda i, j: (l * D_IN + ORIG_COL_AB, 0),
                         pipeline_mode=pl.Buffered(1)),
            pl.BlockSpec((None, 1, LANES), lambda i, j: (l, 0, 0)),
            pl.BlockSpec((None, 1, LANES), lambda i, j: (l, 0, 0)),
        ],
        out_specs=[
            pl.BlockSpec((PROJ_TM, PROJ_TN), lambda i, j: (i, j)),
            pl.BlockSpec((PROJ_TM, LANES), lambda i, j: (i, 0)),
        ],
        out_shape=[
            jax.ShapeDtypeStruct((m, N_MAIN), bf16),
            jax.ShapeDtypeStruct((m, LANES), f32),
        ],
        scratch_shapes=[pltpu.VMEM((PROJ_TM, D_MODEL), bf16)],
        compiler_params=_params("arbitrary", "arbitrary"),
        name="inproj",
    )(x, gain, wt, wt, alog_row, dt_row)


def _ffn_up_kernel(x_ref, g_ref, w_ref, h_ref, hn_ref):
    @pl.when(pl.program_id(1) == 0)
    def _():
        def rows_step(rows):
            hn_ref[rows, :] = _rms_rows(x_ref[rows, :], g_ref[...]).astype(bf16)

        _for_row_chunks(x_ref.shape[0], rows_step)

    a = jnp.maximum(_dot(hn_ref[...], w_ref[...].astype(bf16)), 0.0)
    h_ref[...] = (a * a).astype(h_ref.dtype)


def _ffn_up(x, gain, w1, l):
    m, n = x.shape[0], w1.shape[-1]
    return pl.pallas_call(
        _ffn_up_kernel,
        grid=(m // PROJ_TM, n // PROJ_TN),
        in_specs=[
            pl.BlockSpec((PROJ_TM, D_MODEL), lambda i, j: (i, 0), pipeline_mode=pl.Buffered(1)),
            pl.BlockSpec((None, 1, D_MODEL), lambda i, j: (l, 0, 0)),
            pl.BlockSpec((None, D_MODEL, PROJ_TN), lambda i, j: (l, 0, j)),
        ],
        out_specs=pl.BlockSpec((PROJ_TM, PROJ_TN), lambda i, j: (i, j)),
        out_shape=jax.ShapeDtypeStruct((m, n), bf16),
        scratch_shapes=[pltpu.VMEM((PROJ_TM, D_MODEL), bf16)],
        compiler_params=_params("arbitrary", "arbitrary"),
        name="ffn_up",
    )(x, gain, w1)


def _matmul_res_kernel(a_ref, w_ref, r_ref, o_ref):
    d = _dot(a_ref[...], w_ref[...].astype(bf16))

    @pl.when(pl.program_id(2) == 0)
    def _():
        o_ref[...] = r_ref[...] + d

    @pl.when(pl.program_id(2) != 0)
    def _():
        o_ref[...] += d


def _matmul_res(a, w, res, l):
    m, k = a.shape
    n = w.shape[-1]
    return pl.pallas_call(
        _matmul_res_kernel,
        grid=(m // RES_TM, n // RES_TN, k // RES_TK),
        in_specs=[
            pl.BlockSpec((RES_TM, RES_TK), lambda i, j, kk: (i, kk)),
            pl.BlockSpec((None, RES_TK, RES_TN), lambda i, j, kk: (l, kk, j)),
            pl.BlockSpec((RES_TM, RES_TN), lambda i, j, kk: (i, j)),
        ],
        out_specs=pl.BlockSpec((RES_TM, RES_TN), lambda i, j, kk: (i, j)),
        out_shape=jax.ShapeDtypeStruct((m, n), f32),
        compiler_params=_params("arbitrary", "arbitrary", "arbitrary"),
        name="matmul_res",
    )(a, w, res)


MERGE_GATE_COLS = 1024
MERGE_GATE_BLOCKS = D_MODEL // MERGE_GATE_COLS


def _merge_kernel(oa_ref, ob_ref, oc_ref, wa_ref, wb_ref, wc_ref, *rest):
    gate_refs, y_ref = rest[:-1], rest[-1]
    branches = ((oa_ref, wa_ref), (ob_ref, wb_ref), (oc_ref, wc_ref))
    for c in range(MERGE_GATE_BLOCKS):
        cols = slice(c * MERGE_GATE_COLS, (c + 1) * MERGE_GATE_COLS)
        y = None
        for b, (o_ref, w_ref) in enumerate(branches):
            gate = _sigmoid(gate_refs[b * MERGE_GATE_BLOCKS + c][...].astype(f32))
            term = gate * _dot(o_ref[...], w_ref[:, cols].astype(bf16))
            y = term if y is None else y + term
        y_ref[:, cols] = y.astype(y_ref.dtype)


def _merge(oa, ob, oc, wa, wb, wc, z, l):
    m = oa.shape[0]
    gate0 = COL_GATES // MERGE_GATE_COLS
    branch_in = pl.BlockSpec((MERGE_TM, WIDTH), lambda i: (i, 0))
    branch_w = pl.BlockSpec((None, WIDTH, D_MODEL), lambda i: (l, 0, 0), pipeline_mode=pl.Buffered(1))
    gate_specs = [pl.BlockSpec((MERGE_TM, MERGE_GATE_COLS), lambda i, blk=gate0 + g: (i, blk))
                  for g in range(N_BRANCHES * MERGE_GATE_BLOCKS)]
    return pl.pallas_call(
        _merge_kernel,
        grid=(m // MERGE_TM,),
        in_specs=[branch_in, branch_in, branch_in, branch_w, branch_w, branch_w] + gate_specs,
        out_specs=pl.BlockSpec((MERGE_TM, D_MODEL), lambda i: (i, 0)),
        out_shape=jax.ShapeDtypeStruct((m, D_MODEL), bf16),
        compiler_params=_params("arbitrary"),
        name="merge",
    )(oa, ob, oc, wa, wb, wc, *([z] * (N_BRANCHES * MERGE_GATE_BLOCKS)))


def _gelu(x):
    return 0.5 * x * (1.0 + lax.erf(x * np.float32(np.sqrt(0.5))))


def _gmlp_kernel(zu_ref, zv_ref, g_ref, ws_ref, bs_ref, o_ref):
    u = _gelu(zu_ref[...].astype(f32))
    v = _gelu(zv_ref[...].astype(f32))
    mu = jnp.mean(v, axis=-1, keepdims=True)
    vc = v - mu
    var = jnp.mean(vc * vc, axis=-1, keepdims=True)
    vb = (vc * lax.rsqrt(var + EPS) * g_ref[...]).astype(bf16)

    t_chunk = _block_id(lax.broadcasted_iota(jnp.int32, (GMLP_BLOCK, GMLP_BLOCK), 0), CHUNK)
    s_chunk = _block_id(lax.broadcasted_iota(jnp.int32, (GMLP_BLOCK, GMLP_BLOCK), 1), CHUNK)
    causal = s_chunk <= t_chunk
    gdim = GMLP_WIDTH // GMLP_GROUPS
    for g in range(GMLP_GROUPS):
        ws = jnp.where(causal, ws_ref[g], 0.0).astype(bf16)
        cols = slice(g * gdim, (g + 1) * gdim)
        for blk in range(GMLP_ROWS // GMLP_BLOCK):
            rows = slice(blk * GMLP_BLOCK, (blk + 1) * GMLP_BLOCK)
            s = _dot(ws, vb[rows, cols]) + bs_ref[g]
            o_ref[rows, cols] = (u[rows, cols] * s).astype(o_ref.dtype)


def _gmlp(z, ln_gain, w_spatial, b_rep, l):
    m = z.shape[0]
    cu = COL_GMLP_UV // GMLP_WIDTH
    return pl.pallas_call(
        _gmlp_kernel,
        grid=(m // GMLP_ROWS,),
        in_specs=[
            pl.BlockSpec((GMLP_ROWS, GMLP_WIDTH), lambda i: (i, cu)),
            pl.BlockSpec((GMLP_ROWS, GMLP_WIDTH), lambda i: (i, cu + 1)),
            pl.BlockSpec((None, 1, GMLP_WIDTH), lambda i: (l, 0, 0)),
            pl.BlockSpec((None, GMLP_GROUPS, GMLP_BLOCK, GMLP_BLOCK), lambda i: (l, 0, 0, 0)),
            pl.BlockSpec((None, GMLP_GROUPS, GMLP_BLOCK, LANES), lambda i: (l, 0, 0, 0)),
        ],
        out_specs=pl.BlockSpec((GMLP_ROWS, GMLP_WIDTH), lambda i: (i, 0)),
        out_shape=jax.ShapeDtypeStruct((m, GMLP_WIDTH), bf16),
        compiler_params=_params("arbitrary"),
        name="gmlp",
    )(z, z, ln_gain, w_spatial, b_rep)


def _sba_kernel(q_ref, k_ref, v_ref, qg_ref, kg_ref, o_ref, qn_ref, kn_ref):
    t = q_ref.shape[0]
    head_cols = [slice(j * HEAD_DIM, (j + 1) * HEAD_DIM) for j in range(HEADS_PER_STEP)]
    for hc in head_cols:
        qn_ref[:, hc] = _rms_rows(q_ref[:, hc].astype(f32), qg_ref[...]).astype(bf16)
        kn_ref[:, hc] = _rms_rows(k_ref[:, hc].astype(f32), kg_ref[...]).astype(bf16)
    to_log2 = np.float32(HEAD_DIM ** -0.5 * LOG2_E)
    rows_all = HEADS_PER_STEP * SBA_TQ
    wr = lax.broadcasted_iota(jnp.int32, (2 * SBA_SUB, 2 * SBA_SUB), 0)
    wc = lax.broadcasted_iota(jnp.int32, (2 * SBA_SUB, 2 * SBA_SUB), 1)
    suffix_w = ((jnp.bitwise_and(wr, SBA_SUB - 1) > wc) | (wc >= SBA_SUB)).astype(bf16)

    below_diag = (lax.broadcasted_iota(jnp.int32, (rows_all, SBA_SUB), 1)
                  < jnp.bitwise_and(lax.broadcasted_iota(jnp.int32, (rows_all, SBA_SUB), 0), SBA_TQ - 1))

    def block(qbs, k0, n_cols, diagonal, state):
        acc, run = state
        z = jnp.concatenate([_dot_nt(qbs[j], kn_ref[pl.ds(k0, n_cols), head_cols[j]])
                             for j in range(HEADS_PER_STEP)], axis=0) * to_log2
        neg_abs = pltpu.bitcast(pltpu.bitcast(z, jnp.uint32) | jnp.uint32(0x80000000), f32)
        soft = jnp.log(1.0 + jnp.exp2(neg_abs)) * np.float32(LOG2_E)
        drop = jnp.maximum(z, 0.0) + soft
        log_beta = z - drop
        probs = []
        offs = run
        last = n_cols // SBA_SUB - 1
        for c in range(last, -1, -1):
            cols = slice(c * SBA_SUB, (c + 1) * SBA_SUB)
            masked = diagonal and c == last
            dc = drop[:, cols]
            if masked:
                dc = jnp.where(below_diag, dc, 0.0)
            hi = dc.astype(bf16)
            lo = (dc - hi.astype(f32)).astype(bf16)
            sums = _dot(jnp.concatenate([hi, lo], axis=-1), suffix_w)
            a = jnp.exp2(log_beta[:, cols] - sums[:, :SBA_SUB] - offs)
            if masked:
                a = jnp.where(below_diag, a, 0.0)
            probs.append(a.astype(bf16))
            offs = offs + sums[:, SBA_SUB:]
        a_blk = probs[0] if last == 0 else jnp.concatenate(probs[::-1], axis=-1)
        pv = jnp.concatenate(
            [_dot(a_blk[j * SBA_TQ:(j + 1) * SBA_TQ], v_ref[pl.ds(k0, n_cols), head_cols[j]])
             for j in range(HEADS_PER_STEP)], axis=0)
        return acc + pv, offs

    def key_block_row(s, carry):
        k_diag = pl.multiple_of(s * SBA_TK, SBA_TK)
        for p in range(SBA_TK // SBA_TQ):
            q0 = pl.multiple_of(k_diag + p * SBA_TQ, SBA_TQ)
            qbs = [qn_ref[pl.ds(q0, SBA_TQ), hc] for hc in head_cols]
            zeros = jnp.zeros((rows_all, HEAD_DIM), f32)
            state = block(qbs, k_diag, (p + 1) * SBA_SUB, True, (zeros, zeros))

            def more(c):
                it, _, run = c
                return (it < s) & (jnp.min(run) <= UNDERFLOW_LOG2)

            def left_block(c):
                it, acc, run = c
                k0 = pl.multiple_of((s - 1 - it) * SBA_TK, SBA_TK)
                acc, run = block(qbs, k0, SBA_TK, False, (acc, run))
                return it + 1, acc, run

            _, acc, _ = lax.while_loop(more, left_block, (jnp.int32(0),) + state)
            for j in range(HEADS_PER_STEP):
                o_ref[pl.ds(q0, SBA_TQ), head_cols[j]] = acc[j * SBA_TQ:(j + 1) * SBA_TQ].astype(o_ref.dtype)
        return carry

    lax.fori_loop(0, t // SBA_TK, key_block_row, 0)


def _sba(z, q_gain, k_gain, batch, seq, l):
    m = z.shape[0]
    c0 = COL_SBA_QKV // STEP_WIDTH
    per_seg = WIDTH // STEP_WIDTH

    def head_spec(seg):
        return pl.BlockSpec((seq, STEP_WIDTH), lambda b, h: (b, c0 + seg * per_seg + h))

    gain_spec = pl.BlockSpec((None, 1, HEAD_DIM), lambda b, h: (l, 0, 0))
    tok = pltpu.VMEM((seq, STEP_WIDTH), bf16)
    return pl.pallas_call(
        _sba_kernel,
        grid=(batch, per_seg),
        in_specs=[head_spec(0), head_spec(1), head_spec(2), gain_spec, gain_spec],
        out_specs=pl.BlockSpec((seq, STEP_WIDTH), lambda b, h: (b, h)),
        out_shape=jax.ShapeDtypeStruct((m, WIDTH), bf16),
        scratch_shapes=[tok, tok],
        compiler_params=_params("arbitrary", "arbitrary"),
        name="sba",
    )(z, z, z, q_gain, k_gain)


XPAD = BF16_ROWS


def _bmm(a, b):
    return jnp.einsum('bij,bjk->bik', a, b, preferred_element_type=f32)


def _bmm_nt(a, b):
    return jnp.einsum('bid,bjd->bij', a, b, preferred_element_type=f32)


def _bmm_tn(a, b):
    return jnp.einsum('bck,bcv->bkv', a, b, preferred_element_type=f32)


def _unit_lower_inverse(l_mat, row, col):
    def same_block(d):
        return _block_id(row, d) == _block_id(col, d)

    eye = (row == col).astype(f32)
    l8 = jnp.where(same_block(8), l_mat, 0.0)
    p = l8.astype(bf16)
    x = (eye - l8).astype(bf16)
    p2 = _bmm(p, p)
    x = _bmm(x, (eye + p2).astype(bf16)).astype(bf16)
    p2 = p2.astype(bf16)
    p4 = _bmm(p2, p2)
    x = _bmm(x, (eye + p4).astype(bf16)).astype(bf16)
    d = 8
    while d < CHUNK:
        off = jnp.where(same_block(2 * d) & jnp.logical_not(same_block(d)), l_mat, 0.0).astype(bf16)
        x = _bmm(x, (eye - _bmm(off, x)).astype(bf16)).astype(bf16)
        d *= 2
    return x


SCAN_ROWS = HEAD_DIM + CHUNK


def _gdn_kernel(zq_ref, zk_ref, zv_ref, zg_ref, gb_ref, cwq_ref, cwk_ref, cwv_ref, ng_ref, o_ref,
                xq_ref, xk_ref, xv_ref, stage_ref, lin_ref, off_ref, eg_ref):
    t = zq_ref.shape[0]
    head0 = pl.program_id(1) * HEADS_PER_STEP
    grp = GDN_GROUP
    nc = grp // CHUNK
    head_cols = [slice(j * HEAD_DIM, (j + 1) * HEAD_DIM) for j in range(HEADS_PER_STEP)]

    zeros_pad = jnp.zeros((XPAD, STEP_WIDTH), bf16)
    for src, dst in ((zq_ref, xq_ref), (zk_ref, xk_ref), (zv_ref, xv_ref)):
        dst[pl.ds(0, XPAD), :] = zeros_pad
        dst[pl.ds(XPAD, t), :] = src[...]

    row = lax.broadcasted_iota(jnp.int32, (CHUNK, CHUNK), 0)
    col = lax.broadcasted_iota(jnp.int32, (CHUNK, CHUNK), 1)
    incl = row >= col
    strict = row > col
    tri_ones = jnp.concatenate([incl.astype(bf16), jnp.ones((CHUNK, CHUNK), bf16)], axis=0)
    tri_ones = jnp.broadcast_to(tri_ones, (nc, 2 * CHUNK, CHUNK))
    sel_src = jnp.bitwise_and(lax.broadcasted_iota(jnp.int32, (3 * LANES, STEP_WIDTH), 0), LANES - 1)
    sel_head = head0 + _block_id(lax.broadcasted_iota(jnp.int32, (3 * LANES, STEP_WIDTH), 1), HEAD_DIM)
    pick_g = (sel_src == sel_head).astype(bf16)
    pick_b = (sel_src == sel_head + N_HEADS).astype(bf16)
    sel_rows = (jnp.bitwise_and(lax.broadcasted_iota(jnp.int32, (SUBLANES, 3 * LANES), 1), LANES - 1)
                == head0 + lax.broadcasted_iota(jnp.int32, (SUBLANES, 3 * LANES), 0)).astype(bf16)
    sel_rows = jnp.broadcast_to(sel_rows, (nc, SUBLANES, 3 * LANES))

    def conv_silu(x_ref, cw_ref, r0, hc, slot):
        stage = stage_ref.at[slot]
        stage[...] = x_ref[pl.ds(r0, grp + XPAD), hc].astype(f32)
        acc = cw_ref[pl.ds(CONV_WIDTH - 1, 1), hc] * stage[pl.ds(XPAD, grp), :]
        for i in range(CONV_WIDTH - 1):
            acc += cw_ref[pl.ds(i, 1), hc] * stage[pl.ds(XPAD - (CONV_WIDTH - 1) + i, grp), :]
        return _silu(acc)

    def l2n(x):
        return x * lax.rsqrt(jnp.sum(x * x, axis=-1, keepdims=True) + EPS)

    def chunked(x):
        return x.reshape(nc, CHUNK, x.shape[-1])

    def prepare(it, carry):
        r0 = pl.multiple_of(it * grp, grp)
        gb3 = _split3(gb_ref[pl.ds(r0, grp), :])
        ct = _bmm(tri_ones, chunked(gb3))
        cum3 = _split3(_sum3(ct[:, :CHUNK], LANES).reshape(grp, LANES))
        tot3 = _split3(_sum3(ct[:, CHUNK:], LANES).reshape(grp, LANES))
        gc_rows = _bmm_nt(sel_rows, chunked(cum3))

        gc_all = _dot(cum3, pick_g)
        g_tot_all = _dot(tot3, pick_g)
        beta_all = _dot(gb3, pick_b)

        k16s, kb16s, q16s, rhs16s, diffs, qds, kd16s = [], [], [], [], [], [], []
        for j, hc in enumerate(head_cols):
            gc, g_tot, beta = gc_all[:, hc], g_tot_all[:, hc], beta_all[:, hc]
            q = l2n(conv_silu(xq_ref, cwq_ref, r0, hc, 3 * j)) * np.float32(HEAD_DIM ** -0.5)
            k = l2n(conv_silu(xk_ref, cwk_ref, r0, hc, 3 * j + 1))
            v = conv_silu(xv_ref, cwv_ref, r0, hc, 3 * j + 2)
            kb = k * beta
            e_gc = jnp.exp(gc)
            k16s.append(chunked(k.astype(bf16)))
            kb16s.append(chunked(kb.astype(bf16)))
            q16s.append(chunked(q.astype(bf16)))
            rhs16s.append(chunked(jnp.concatenate([v * beta, kb * e_gc], axis=-1).astype(bf16)))
            diffs.append(chunked(gc)[:, :, :CHUNK] - gc_rows[:, j:j + 1, :])
            qds.append(chunked(q * e_gc))
            kd16s.append(chunked((k * jnp.exp(g_tot - gc)).astype(bf16)))
            e0 = pl.multiple_of(it * (nc * SUBLANES), nc * SUBLANES)
            eg_ref[pl.ds(e0, nc * SUBLANES), hc] = (
                chunked(jnp.exp(g_tot))[:, :SUBLANES, :].reshape(nc * SUBLANES, HEAD_DIM))

        cat = lambda xs: jnp.concatenate(xs, axis=0)
        k16, diff = cat(k16s), cat(diffs)
        decay = jnp.where(incl, jnp.exp(jnp.where(incl, diff, 0.0)), 0.0)
        l_mat = jnp.where(strict, _bmm_nt(cat(kb16s), k16) * decay, 0.0)
        intra = (_bmm_nt(cat(q16s), k16) * decay).astype(bf16)
        t_inv = _unit_lower_inverse(l_mat, row, col)
        uw = _bmm(t_inv, cat(rhs16s)).astype(bf16)
        state_map = _bmm_tn(cat(kd16s), uw)
        out_map = _bmm(intra, uw)
        q_eff = cat(qds) - out_map[:, :, HEAD_DIM:]
        lin = jnp.concatenate([state_map[:, :, HEAD_DIM:], q_eff], axis=1).astype(bf16)
        off = jnp.concatenate([state_map[:, :, :HEAD_DIM], out_map[:, :, :HEAD_DIM]], axis=1)
        s0 = pl.multiple_of(it * (nc * SCAN_ROWS), nc * SCAN_ROWS)
        for j, hc in enumerate(head_cols):
            mine = slice(j * nc, (j + 1) * nc)
            lin_ref[pl.ds(s0, nc * SCAN_ROWS), hc] = lin[mine].reshape(nc * SCAN_ROWS, HEAD_DIM)
            off_ref[pl.ds(s0, nc * SCAN_ROWS), hc] = off[mine].reshape(nc * SCAN_ROWS, HEAD_DIM)
        return carry

    lax.fori_loop(0, t // grp, prepare, 0)

    def scan(n, states):
        s0 = pl.multiple_of(n * SCAN_ROWS, SCAN_ROWS)
        e0 = pl.multiple_of(n * SUBLANES, SUBLANES)
        r0 = pl.multiple_of(n * CHUNK, CHUNK)
        new_states = []
        for j, hc in enumerate(head_cols):
            state = states[j]
            prod = _dot(lin_ref[pl.ds(s0, SCAN_ROWS), hc], state.astype(bf16))
            off = off_ref[pl.ds(s0, SCAN_ROWS), hc]
            new_states.append(state * eg_ref[pl.ds(e0, 1), hc] + (off[:HEAD_DIM] - prod[:HEAD_DIM]))
            o = prod[HEAD_DIM:] + off[HEAD_DIM:]
            gate = zg_ref[pl.ds(r0, CHUNK), hc].astype(f32)
            o_ref[pl.ds(r0, CHUNK), hc] = (_rms_rows(o, ng_ref[...]) * _silu(gate)).astype(o_ref.dtype)
        return tuple(new_states)

    zero_state = jnp.zeros((HEAD_DIM, HEAD_DIM), f32)
    lax.fori_loop(0, t // CHUNK, scan, (zero_state,) * HEADS_PER_STEP)


def _gdn(z, gb, conv_w, norm_gain, batch, seq, l):
    m = z.shape[0]
    per_seg = WIDTH // STEP_WIDTH
    cq = COL_GDN_QKV // STEP_WIDTH
    cg = COL_GDN_GATE // STEP_WIDTH

    def head_spec(c):
        return pl.BlockSpec((seq, STEP_WIDTH), lambda b, h: (b, c + h))

    def conv_spec(seg):
        return pl.BlockSpec((None, CONV_WIDTH, STEP_WIDTH), lambda b, h: (l, 0, seg * per_seg + h))

    n_chunks = seq // CHUNK
    padded = pltpu.VMEM((seq + XPAD, STEP_WIDTH), bf16)
    return pl.pallas_call(
        _gdn_kernel,
        grid=(batch, per_seg),
        in_specs=[
            head_spec(cq), head_spec(cq + per_seg), head_spec(cq + 2 * per_seg), head_spec(cg),
            pl.BlockSpec((seq, LANES), lambda b, h: (b, 0)),
            conv_spec(0), conv_spec(1), conv_spec(2),
            pl.BlockSpec((None, 1, HEAD_DIM), lambda b, h: (l, 0, 0)),
        ],
        out_specs=pl.BlockSpec((seq, STEP_WIDTH), lambda b, h: (b, h)),
        out_shape=jax.ShapeDtypeStruct((m, WIDTH), bf16),
        scratch_shapes=[
            padded, padded, padded,
            pltpu.VMEM((3 * HEADS_PER_STEP, GDN_GROUP + XPAD, HEAD_DIM), f32),
            pltpu.VMEM((n_chunks * SCAN_ROWS, STEP_WIDTH), bf16),
            pltpu.VMEM((n_chunks * SCAN_ROWS, STEP_WIDTH), f32),
            pltpu.VMEM((n_chunks * SUBLANES, STEP_WIDTH), f32),
        ],
        compiler_params=_params("arbitrary", "arbitrary"),
        name="gdn",
    )(z, z, z, z, gb, conv_w, conv_w, conv_w, norm_gain)


def _rows(v):
    return v.astype(f32)[:, None, :]


def _pad_rows(v):
    return _rows(jnp.pad(v, ((0, 0), (0, LANES - v.shape[1]))))


def kernel(x, w_in, conv_w, a_log, dt_bias, gdn_norm_g, gmlp_ln_g, w_spatial, b_spatial, sba_q_g, sba_k_g, w_out_a, w_out_b, w_out_c, w_out, norm_mix_g, norm_mlp_g, w_ff1, w_ff2):
    batch, seq, d = x.shape
    depth = w_in.shape[0]
    m = batch * seq

    wt = jnp.swapaxes(w_in, 1, 2).reshape(depth * D_IN, d)
    b_rep = jnp.broadcast_to(b_spatial[..., None], b_spatial.shape + (LANES,)).astype(f32)
    mix_g, mlp_g, gdn_g, ln_g = _rows(norm_mix_g), _rows(norm_mlp_g), _rows(gdn_norm_g), _rows(gmlp_ln_g)
    q_g, k_g = _rows(sba_q_g), _rows(sba_k_g)
    alog_rows, dt_rows = _pad_rows(a_log), _pad_rows(dt_bias)
    conv_w = conv_w.astype(f32)
    w_spatial = w_spatial.astype(f32)

    xf = x.reshape(m, d).astype(f32)
    for l in range(depth):
        z, gb = _inproj(xf, mix_g, wt, alog_rows, dt_rows, l)
        oa = _gdn(z, gb, conv_w, gdn_g, batch, seq, l)
        ob = _gmlp(z, ln_g, w_spatial, b_rep, l)
        oc = _sba(z, q_g, k_g, batch, seq, l)
        y = _merge(oa, ob, oc, w_out_a, w_out_b, w_out_c, z, l)
        xf = _matmul_res(y, w_out, xf, l)
        h1 = _ffn_up(xf, mlp_g, w_ff1, l)
        xf = _matmul_res(h1, w_ff2, xf, l)
    return xf.reshape(batch, seq, d).astype(x.dtype)
```

```python
import jax
import jax.numpy as jnp
import numpy as np
from jax import lax
from jax.experimental import pallas as pl
from jax.experimental.pallas import tpu as pltpu

f32 = jnp.float32
bf16 = jnp.bfloat16

D_MODEL = 2048
CHUNK = 64
HEAD_DIM = 128
N_HEADS = 8
WIDTH = N_HEADS * HEAD_DIM
CONV_WIDTH = 4
GMLP_WIDTH = 1024
GMLP_GROUPS = 8
GMLP_BLOCK = 128
N_BRANCHES = 3
EPS = 1e-6
LOG2_E = float(np.log2(np.e))
UNDERFLOW_LOG2 = 160.0

LANES = 128
SUBLANES = 8
BF16_ROWS = 16
VMEM_LIMIT_BYTES = 60000 * 1024

COL_GDN_QKV = 0
COL_GDN_GATE = 3 * WIDTH
COL_GMLP_UV = COL_GDN_GATE + WIDTH
COL_SBA_QKV = COL_GMLP_UV + 2 * GMLP_WIDTH
COL_GATES = COL_SBA_QKV + 3 * WIDTH
N_MAIN = COL_GATES + N_BRANCHES * D_MODEL
ORIG_COL_AB = 3 * WIDTH
N_AB = 2 * N_HEADS
D_IN = N_MAIN + N_AB

PROJ_TM = 2048
PROJ_TN = 512
RES_TM = 2048
RES_TN = 1024
RES_TK = 1024
MERGE_TM = 512

HEADS_PER_STEP = 4
STEP_WIDTH = HEADS_PER_STEP * HEAD_DIM
GDN_GROUP = 512
SBA_TQ = 128
SBA_TK = 512
SBA_LEFT_TK = 256
SBA_SUB = 128
GMLP_ROWS = 512


def _params(*sem):
    return pltpu.CompilerParams(dimension_semantics=sem, vmem_limit_bytes=VMEM_LIMIT_BYTES)


def _dot(a, b):
    return jnp.dot(a, b, preferred_element_type=f32)


def _dot_nt(a, b):
    return lax.dot_general(a, b, (((1,), (1,)), ((), ())), preferred_element_type=f32)


def _sigmoid(x):
    return lax.logistic(x)


def _silu(x):
    return x * _sigmoid(x)


def _softplus(x):
    return jnp.maximum(x, 0.0) + jnp.log1p(jnp.exp(-jnp.abs(x)))


def _block_id(idx, size):
    return lax.shift_right_logical(idx, int(size).bit_length() - 1)


def _split3(x):
    hi = x.astype(bf16)
    r1 = x - hi.astype(f32)
    mid = r1.astype(bf16)
    lo = (r1 - mid.astype(f32)).astype(bf16)
    return jnp.concatenate([hi, mid, lo], axis=-1)


def _sum3(x, n):
    return x[..., :n] + x[..., n:2 * n] + x[..., 2 * n:3 * n]


def _rms_rows(x, gain):
    ms = jnp.mean(x * x, axis=-1, keepdims=True)
    return x * lax.rsqrt(ms + EPS) * gain


NORM_ROWS = 256


def _for_row_chunks(n_rows, body):
    def step(c, carry):
        body(pl.ds(pl.multiple_of(c * NORM_ROWS, NORM_ROWS), NORM_ROWS))
        return carry

    lax.fori_loop(0, n_rows // NORM_ROWS, step, 0)


def _inproj_kernel(x_ref, g_ref, wt_ref, wabt_ref, alog_ref, dt_ref, z_ref, gb_ref, hn_ref):
    @pl.when(pl.program_id(1) == 0)
    def _():
        wab = wabt_ref[...].astype(bf16)

        def rows_step(rows):
            hn = _rms_rows(x_ref[rows, :], g_ref[...]).astype(bf16)
            hn_ref[rows, :] = hn
            ab = _dot_nt(hn, wab)
            lane = lax.broadcasted_iota(jnp.int32, ab.shape, 1)
            g = -jnp.exp(alog_ref[...]) * _softplus(ab + dt_ref[...])
            gb_ref[rows, :] = jnp.where(lane < N_HEADS, g, _sigmoid(ab))

        _for_row_chunks(x_ref.shape[0], rows_step)

    z_ref[...] = _dot_nt(hn_ref[...], wt_ref[...].astype(bf16)).astype(z_ref.dtype)


def _inproj(x, gain, wt, alog_row, dt_row, l):
    m = x.shape[0]
    n_direct = ORIG_COL_AB // PROJ_TN

    def w_rows(i, j):
        return pl.multiple_of(l * D_IN + j * PROJ_TN + jnp.where(j >= n_direct, N_AB, 0), SUBLANES), 0

    return pl.pallas_call(
        _inproj_kernel,
        grid=(m // PROJ_TM, N_MAIN // PROJ_TN),
        in_specs=[
            pl.BlockSpec((PROJ_TM, D_MODEL), lambda i, j: (i, 0), pipeline_mode=pl.Buffered(1)),
            pl.BlockSpec((None, 1, D_MODEL), lambda i, j: (l, 0, 0)),
            pl.BlockSpec((pl.Element(PROJ_TN), pl.Element(D_MODEL)), w_rows),
            pl.BlockSpec((pl.Element(LANES), pl.Element(D_MODEL)), lambda i, j: (l * D_IN + ORIG_COL_AB, 0),
                         pipeline_mode=pl.Buffered(1)),
            pl.BlockSpec((None, 1, LANES), lambda i, j: (l, 0, 0)),
            pl.BlockSpec((None, 1, LANES), lambda i, j: (l, 0, 0)),
        ],
        out_specs=[
            pl.BlockSpec((PROJ_TM, PROJ_TN), lambda i, j: (i, j)),
            pl.BlockSpec((PROJ_TM, LANES), lambda i, j: (i, 0)),
        ],
        out_shape=[
            jax.ShapeDtypeStruct((m, N_MAIN), bf16),
            jax.ShapeDtypeStruct((m, LANES), f32),
        ],
        scratch_shapes=[pltpu.VMEM((PROJ_TM, D_MODEL), bf16)],
        compiler_params=_params("arbitrary", "arbitrary"),
        name="inproj",
    )(x, gain, wt, wt, alog_row, dt_row)


def _ffn_up_kernel(x_ref, g_ref, w_ref, h_ref, hn_ref):
    @pl.when(pl.program_id(1) == 0)
    def _():
        def rows_step(rows):
            hn_ref[rows, :] = _rms_rows(x_ref[rows, :], g_ref[...]).astype(bf16)

        _for_row_chunks(x_ref.shape[0], rows_step)

    a = jnp.maximum(_dot(hn_ref[...], w_ref[...].astype(bf16)), 0.0)
    h_ref[...] = (a * a).astype(h_ref.dtype)


def _ffn_up(x, gain, w1, l):
    m, n = x.shape[0], w1.shape[-1]
    return pl.pallas_call(
        _ffn_up_kernel,
        grid=(m // PROJ_TM, n // PROJ_TN),
        in_specs=[
            pl.BlockSpec((PROJ_TM, D_MODEL), lambda i, j: (i, 0), pipeline_mode=pl.Buffered(1)),
            pl.BlockSpec((None, 1, D_MODEL), lambda i, j: (l, 0, 0)),
            pl.BlockSpec((None, D_MODEL, PROJ_TN), lambda i, j: (l, 0, j)),
        ],
        out_specs=pl.BlockSpec((PROJ_TM, PROJ_TN), lambda i, j: (i, j)),
        out_shape=jax.ShapeDtypeStruct((m, n), bf16),
        scratch_shapes=[pltpu.VMEM((PROJ_TM, D_MODEL), bf16)],
        compiler_params=_params("arbitrary", "arbitrary"),
        name="ffn_up",
    )(x, gain, w1)


def _matmul_res_kernel(a_ref, w_ref, r_ref, o_ref):
    d = _dot(a_ref[...], w_ref[...].astype(bf16))

    @pl.when(pl.program_id(2) == 0)
    def _():
        o_ref[...] = r_ref[...] + d

    @pl.when(pl.program_id(2) != 0)
    def _():
        o_ref[...] += d


def _matmul_res(a, w, res, l):
    m, k = a.shape
    n = w.shape[-1]
    tm, tk = (RES_TM // 2, 2 * RES_TK) if k > D_MODEL else (RES_TM, RES_TK)
    return pl.pallas_call(
        _matmul_res_kernel,
        grid=(m // tm, n // RES_TN, k // tk),
        in_specs=[
            pl.BlockSpec((tm, tk), lambda i, j, kk: (i, kk)),
            pl.BlockSpec((None, tk, RES_TN), lambda i, j, kk: (l, kk, j)),
            pl.BlockSpec((tm, RES_TN), lambda i, j, kk: (i, j)),
        ],
        out_specs=pl.BlockSpec((tm, RES_TN), lambda i, j, kk: (i, j)),
        out_shape=jax.ShapeDtypeStruct((m, n), f32),
        compiler_params=_params("arbitrary", "arbitrary", "arbitrary"),
        name="matmul_res",
    )(a, w, res)


MERGE_GATE_COLS = 1024
MERGE_GATE_BLOCKS = D_MODEL // MERGE_GATE_COLS


def _merge_kernel(oa_ref, ob_ref, oc_ref, wa_ref, wb_ref, wc_ref, *rest):
    gate_refs, y_ref = rest[:-1], rest[-1]
    branches = ((oa_ref, wa_ref), (ob_ref, wb_ref), (oc_ref, wc_ref))
    for c in range(MERGE_GATE_BLOCKS):
        cols = slice(c * MERGE_GATE_COLS, (c + 1) * MERGE_GATE_COLS)
        y = None
        for b, (o_ref, w_ref) in enumerate(branches):
            gate = _sigmoid(gate_refs[b * MERGE_GATE_BLOCKS + c][...].astype(f32))
            term = gate * _dot(o_ref[...], w_ref[:, cols].astype(bf16))
            y = term if y is None else y + term
        y_ref[:, cols] = y.astype(y_ref.dtype)


def _merge(oa, ob, oc, wa, wb, wc, z, l):
    m = oa.shape[0]
    gate0 = COL_GATES // MERGE_GATE_COLS
    branch_in = pl.BlockSpec((MERGE_TM, WIDTH), lambda i: (i, 0))
    branch_w = pl.BlockSpec((None, WIDTH, D_MODEL), lambda i: (l, 0, 0), pipeline_mode=pl.Buffered(1))
    gate_specs = [pl.BlockSpec((MERGE_TM, MERGE_GATE_COLS), lambda i, blk=gate0 + g: (i, blk))
                  for g in range(N_BRANCHES * MERGE_GATE_BLOCKS)]
    return pl.pallas_call(
        _merge_kernel,
        grid=(m // MERGE_TM,),
        in_specs=[branch_in, branch_in, branch_in, branch_w, branch_w, branch_w] + gate_specs,
        out_specs=pl.BlockSpec((MERGE_TM, D_MODEL), lambda i: (i, 0)),
        out_shape=jax.ShapeDtypeStruct((m, D_MODEL), bf16),
        compiler_params=_params("arbitrary"),
        name="merge",
    )(oa, ob, oc, wa, wb, wc, *([z] * (N_BRANCHES * MERGE_GATE_BLOCKS)))


def _gelu(x):
    return 0.5 * x * (1.0 + lax.erf(x * np.float32(np.sqrt(0.5))))


def _gmlp_kernel(zu_ref, zv_ref, g_ref, ws_ref, bs_ref, o_ref):
    u = _gelu(zu_ref[...].astype(f32))
    v = _gelu(zv_ref[...].astype(f32))
    mu = jnp.mean(v, axis=-1, keepdims=True)
    vc = v - mu
    var = jnp.mean(vc * vc, axis=-1, keepdims=True)
    vb = (vc * lax.rsqrt(var + EPS) * g_ref[...]).astype(bf16)

    t_chunk = _block_id(lax.broadcasted_iota(jnp.int32, (GMLP_BLOCK, GMLP_BLOCK), 0), CHUNK)
    s_chunk = _block_id(lax.broadcasted_iota(jnp.int32, (GMLP_BLOCK, GMLP_BLOCK), 1), CHUNK)
    causal = s_chunk <= t_chunk
    gdim = GMLP_WIDTH // GMLP_GROUPS
    for g in range(GMLP_GROUPS):
        ws = jnp.where(causal, ws_ref[g], 0.0).astype(bf16)
        cols = slice(g * gdim, (g + 1) * gdim)
        for blk in range(GMLP_ROWS // GMLP_BLOCK):
            rows = slice(blk * GMLP_BLOCK, (blk + 1) * GMLP_BLOCK)
            s = _dot(ws, vb[rows, cols]) + bs_ref[g]
            o_ref[rows, cols] = (u[rows, cols] * s).astype(o_ref.dtype)


def _gmlp(z, ln_gain, w_spatial, b_rep, l):
    m = z.shape[0]
    cu = COL_GMLP_UV // GMLP_WIDTH
    return pl.pallas_call(
        _gmlp_kernel,
        grid=(m // GMLP_ROWS,),
        in_specs=[
            pl.BlockSpec((GMLP_ROWS, GMLP_WIDTH), lambda i: (i, cu)),
            pl.BlockSpec((GMLP_ROWS, GMLP_WIDTH), lambda i: (i, cu + 1)),
            pl.BlockSpec((None, 1, GMLP_WIDTH), lambda i: (l, 0, 0)),
            pl.BlockSpec((None, GMLP_GROUPS, GMLP_BLOCK, GMLP_BLOCK), lambda i: (l, 0, 0, 0)),
            pl.BlockSpec((None, GMLP_GROUPS, GMLP_BLOCK, LANES), lambda i: (l, 0, 0, 0)),
        ],
        out_specs=pl.BlockSpec((GMLP_ROWS, GMLP_WIDTH), lambda i: (i, 0)),
        out_shape=jax.ShapeDtypeStruct((m, GMLP_WIDTH), bf16),
        compiler_params=_params("arbitrary"),
        name="gmlp",
    )(z, z, ln_gain, w_spatial, b_rep)


def _sba_kernel(q_ref, k_ref, v_ref, qg_ref, kg_ref, o_ref, qn_ref, kn_ref):
    t = q_ref.shape[0]
    head_cols = [slice(j * HEAD_DIM, (j + 1) * HEAD_DIM) for j in range(HEADS_PER_STEP)]
    for hc in head_cols:
        qn_ref[:, hc] = _rms_rows(q_ref[:, hc].astype(f32), qg_ref[...]).astype(bf16)
        kn_ref[:, hc] = _rms_rows(k_ref[:, hc].astype(f32), kg_ref[...]).astype(bf16)
    to_log2 = np.float32(HEAD_DIM ** -0.5 * LOG2_E)
    rows_all = HEADS_PER_STEP * SBA_TQ
    wr = lax.broadcasted_iota(jnp.int32, (2 * SBA_SUB, 2 * SBA_SUB), 0)
    wc = lax.broadcasted_iota(jnp.int32, (2 * SBA_SUB, 2 * SBA_SUB), 1)
    suffix_w = ((jnp.bitwise_and(wr, SBA_SUB - 1) > wc) | (wc >= SBA_SUB)).astype(bf16)

    below_diag = (lax.broadcasted_iota(jnp.int32, (rows_all, SBA_SUB), 1)
                  < jnp.bitwise_and(lax.broadcasted_iota(jnp.int32, (rows_all, SBA_SUB), 0), SBA_TQ - 1))

    def block(qbs, k0, n_cols, diagonal, state):
        acc, run = state
        z = jnp.concatenate([_dot_nt(qbs[j], kn_ref[pl.ds(k0, n_cols), head_cols[j]])
                             for j in range(HEADS_PER_STEP)], axis=0) * to_log2
        neg_abs = pltpu.bitcast(pltpu.bitcast(z, jnp.uint32) | jnp.uint32(0x80000000), f32)
        soft = jnp.log(1.0 + jnp.exp2(neg_abs)) * np.float32(LOG2_E)
        drop = jnp.maximum(z, 0.0) + soft
        log_beta = z - drop
        probs = []
        offs = run
        last = n_cols // SBA_SUB - 1
        for c in range(last, -1, -1):
            cols = slice(c * SBA_SUB, (c + 1) * SBA_SUB)
            masked = diagonal and c == last
            dc = drop[:, cols]
            if masked:
                dc = jnp.where(below_diag, dc, 0.0)
            hi = dc.astype(bf16)
            lo = (dc - hi.astype(f32)).astype(bf16)
            sums = _dot(jnp.concatenate([hi, lo], axis=-1), suffix_w)
            a = jnp.exp2(log_beta[:, cols] - sums[:, :SBA_SUB] - offs)
            if masked:
                a = jnp.where(below_diag, a, 0.0)
            probs.append(a.astype(bf16))
            offs = offs + sums[:, SBA_SUB:]
        a_blk = probs[0] if last == 0 else jnp.concatenate(probs[::-1], axis=-1)
        pv = jnp.concatenate(
            [_dot(a_blk[j * SBA_TQ:(j + 1) * SBA_TQ], v_ref[pl.ds(k0, n_cols), head_cols[j]])
             for j in range(HEADS_PER_STEP)], axis=0)
        return acc + pv, offs

    def key_block_row(s, carry):
        k_diag = pl.multiple_of(s * SBA_TK, SBA_TK)
        n_q = SBA_TK // SBA_TQ
        q0s = [pl.multiple_of(k_diag + p * SBA_TQ, SBA_TQ) for p in range(n_q)]
        qbss = [[qn_ref[pl.ds(q0, SBA_TQ), hc] for hc in head_cols] for q0 in q0s]
        zeros = jnp.zeros((rows_all, HEAD_DIM), f32)
        states = [block(qbss[p], k_diag, (p + 1) * SBA_SUB, True, (zeros, zeros)) for p in range(n_q)]
        for p in range(n_q):
            q0, qbs, state = q0s[p], qbss[p], states[p]

            n_left = s * (SBA_TK // SBA_LEFT_TK)

            def more(c):
                it, _, run = c
                return (it < n_left) & (jnp.min(run) <= UNDERFLOW_LOG2)

            def left_block(c):
                it, acc, run = c
                k0 = pl.multiple_of((n_left - 1 - it) * SBA_LEFT_TK, SBA_LEFT_TK)
                acc, run = block(qbs, k0, SBA_LEFT_TK, False, (acc, run))
                return it + 1, acc, run

            _, acc, _ = lax.while_loop(more, left_block, (jnp.int32(0),) + state)
            for j in range(HEADS_PER_STEP):
                o_ref[pl.ds(q0, SBA_TQ), head_cols[j]] = acc[j * SBA_TQ:(j + 1) * SBA_TQ].astype(o_ref.dtype)
        return carry

    lax.fori_loop(0, t // SBA_TK, key_block_row, 0)


def _sba(z, q_gain, k_gain, batch, seq, l):
    m = z.shape[0]
    c0 = COL_SBA_QKV // STEP_WIDTH
    per_seg = WIDTH // STEP_WIDTH

    def head_spec(seg):
        return pl.BlockSpec((seq, STEP_WIDTH), lambda b, h: (b, c0 + seg * per_seg + h))

    gain_spec = pl.BlockSpec((None, 1, HEAD_DIM), lambda b, h: (l, 0, 0))
    tok = pltpu.VMEM((seq, STEP_WIDTH), bf16)
    return pl.pallas_call(
        _sba_kernel,
        grid=(batch, per_seg),
        in_specs=[head_spec(0), head_spec(1), head_spec(2), gain_spec, gain_spec],
        out_specs=pl.BlockSpec((seq, STEP_WIDTH), lambda b, h: (b, h)),
        out_shape=jax.ShapeDtypeStruct((m, WIDTH), bf16),
        scratch_shapes=[tok, tok],
        compiler_params=_params("arbitrary", "arbitrary"),
        name="sba",
    )(z, z, z, q_gain, k_gain)


XPAD = BF16_ROWS


def _bmm(a, b):
    return jnp.einsum('bij,bjk->bik', a, b, preferred_element_type=f32)


def _bmm_nt(a, b):
    return jnp.einsum('bid,bjd->bij', a, b, preferred_element_type=f32)


def _bmm_tn(a, b):
    return jnp.einsum('bck,bcv->bkv', a, b, preferred_element_type=f32)


def _unit_lower_inverse(l_mat, row, col):
    def same_block(d):
        return _block_id(row, d) == _block_id(col, d)

    eye = (row == col).astype(f32)
    l8 = jnp.where(same_block(8), l_mat, 0.0)
    p = l8.astype(bf16)
    x = (eye - l8).astype(bf16)
    p2 = _bmm(p, p)
    x = _bmm(x, (eye + p2).astype(bf16)).astype(bf16)
    p2 = p2.astype(bf16)
    p4 = _bmm(p2, p2)
    x = _bmm(x, (eye + p4).astype(bf16)).astype(bf16)
    d = 8
    while d < CHUNK:
        off = jnp.where(same_block(2 * d) & jnp.logical_not(same_block(d)), l_mat, 0.0).astype(bf16)
        x = _bmm(x, (eye - _bmm(off, x)).astype(bf16)).astype(bf16)
        d *= 2
    return x


SCAN_ROWS = HEAD_DIM + CHUNK


def _gdn_kernel(zq_ref, zk_ref, zv_ref, zg_ref, gb_ref, cwq_ref, cwk_ref, cwv_ref, ng_ref, o_ref,
                xq_ref, xk_ref, xv_ref, stage_ref, lin_ref, off_ref, eg_ref):
    t = zq_ref.shape[0]
    head0 = pl.program_id(1) * HEADS_PER_STEP
    grp = GDN_GROUP
    nc = grp // CHUNK
    head_cols = [slice(j * HEAD_DIM, (j + 1) * HEAD_DIM) for j in range(HEADS_PER_STEP)]

    zeros_pad = jnp.zeros((XPAD, STEP_WIDTH), bf16)
    for src, dst in ((zq_ref, xq_ref), (zk_ref, xk_ref), (zv_ref, xv_ref)):
        dst[pl.ds(0, XPAD), :] = zeros_pad
        dst[pl.ds(XPAD, t), :] = src[...]

    row = lax.broadcasted_iota(jnp.int32, (CHUNK, CHUNK), 0)
    col = lax.broadcasted_iota(jnp.int32, (CHUNK, CHUNK), 1)
    incl = row >= col
    strict = row > col
    tri_ones = jnp.concatenate([incl.astype(bf16), jnp.ones((CHUNK, CHUNK), bf16)], axis=0)
    tri_ones = jnp.broadcast_to(tri_ones, (nc, 2 * CHUNK, CHUNK))
    sel_src = jnp.bitwise_and(lax.broadcasted_iota(jnp.int32, (3 * LANES, STEP_WIDTH), 0), LANES - 1)
    sel_head = head0 + _block_id(lax.broadcasted_iota(jnp.int32, (3 * LANES, STEP_WIDTH), 1), HEAD_DIM)
    pick_g = (sel_src == sel_head).astype(bf16)
    pick_b = (sel_src == sel_head + N_HEADS).astype(bf16)
    sel_rows = (jnp.bitwise_and(lax.broadcasted_iota(jnp.int32, (SUBLANES, 3 * LANES), 1), LANES - 1)
                == head0 + lax.broadcasted_iota(jnp.int32, (SUBLANES, 3 * LANES), 0)).astype(bf16)
    sel_rows = jnp.broadcast_to(sel_rows, (nc, SUBLANES, 3 * LANES))

    def conv_silu(x_ref, cw_ref, r0, hc, slot):
        stage = stage_ref.at[slot]
        stage[...] = x_ref[pl.ds(r0, grp + XPAD), hc].astype(f32)
        acc = cw_ref[pl.ds(CONV_WIDTH - 1, 1), hc] * stage[pl.ds(XPAD, grp), :]
        for i in range(CONV_WIDTH - 1):
            acc += cw_ref[pl.ds(i, 1), hc] * stage[pl.ds(XPAD - (CONV_WIDTH - 1) + i, grp), :]
        return _silu(acc)

    def l2n(x):
        return x * lax.rsqrt(jnp.sum(x * x, axis=-1, keepdims=True) + EPS)

    def chunked(x):
        return x.reshape(nc, CHUNK, x.shape[-1])

    def prepare(it, carry):
        r0 = pl.multiple_of(it * grp, grp)
        gb3 = _split3(gb_ref[pl.ds(r0, grp), :])
        ct = _bmm(tri_ones, chunked(gb3))
        cum3 = _split3(_sum3(ct[:, :CHUNK], LANES).reshape(grp, LANES))
        tot3 = _split3(_sum3(ct[:, CHUNK:], LANES).reshape(grp, LANES))
        gc_rows = _bmm_nt(sel_rows, chunked(cum3))

        gc_all = _dot(cum3, pick_g)
        g_tot_all = _dot(tot3, pick_g)
        beta_all = _dot(gb3, pick_b)

        k16s, kb16s, q16s, rhs16s, diffs, qds, kd16s = [], [], [], [], [], [], []
        for j, hc in enumerate(head_cols):
            gc, g_tot, beta = gc_all[:, hc], g_tot_all[:, hc], beta_all[:, hc]
            q = l2n(conv_silu(xq_ref, cwq_ref, r0, hc, 3 * j)) * np.float32(HEAD_DIM ** -0.5)
            k = l2n(conv_silu(xk_ref, cwk_ref, r0, hc, 3 * j + 1))
            v = conv_silu(xv_ref, cwv_ref, r0, hc, 3 * j + 2)
            kb = k * beta
            e_gc = jnp.exp(gc)
            k16s.append(chunked(k.astype(bf16)))
            kb16s.append(chunked(kb.astype(bf16)))
            q16s.append(chunked(q.astype(bf16)))
            rhs16s.append(chunked(jnp.concatenate([v * beta, kb * e_gc], axis=-1).astype(bf16)))
            diffs.append(chunked(gc)[:, :, :CHUNK] - gc_rows[:, j:j + 1, :])
            qds.append(chunked(q * e_gc))
            kd16s.append(chunked((k * jnp.exp(g_tot - gc)).astype(bf16)))
            e0 = pl.multiple_of(it * (nc * SUBLANES), nc * SUBLANES)
            eg_ref[pl.ds(e0, nc * SUBLANES), hc] = (
                chunked(jnp.exp(g_tot))[:, :SUBLANES, :].reshape(nc * SUBLANES, HEAD_DIM))

        cat = lambda xs: jnp.concatenate(xs, axis=0)
        k16, diff = cat(k16s), cat(diffs)
        decay = jnp.where(incl, jnp.exp(jnp.where(incl, diff, 0.0)), 0.0)
        l_mat = jnp.where(strict, _bmm_nt(cat(kb16s), k16) * decay, 0.0)
        intra = (_bmm_nt(cat(q16s), k16) * decay).astype(bf16)
        t_inv = _unit_lower_inverse(l_mat, row, col)
        uw = _bmm(t_inv, cat(rhs16s)).astype(bf16)
        state_map = _bmm_tn(cat(kd16s), uw)
        out_map = _bmm(intra, uw)
        q_eff = cat(qds) - out_map[:, :, HEAD_DIM:]
        lin = jnp.concatenate([state_map[:, :, HEAD_DIM:], q_eff], axis=1).astype(bf16)
        off = jnp.concatenate([state_map[:, :, :HEAD_DIM], out_map[:, :, :HEAD_DIM]], axis=1)
        s0 = pl.multiple_of(it * (nc * SCAN_ROWS), nc * SCAN_ROWS)
        for j, hc in enumerate(head_cols):
            mine = slice(j * nc, (j + 1) * nc)
            lin_ref[pl.ds(s0, nc * SCAN_ROWS), hc] = lin[mine].reshape(nc * SCAN_ROWS, HEAD_DIM)
            off_ref[pl.ds(s0, nc * SCAN_ROWS), hc] = off[mine].reshape(nc * SCAN_ROWS, HEAD_DIM)
        return carry

    lax.fori_loop(0, t // grp, prepare, 0)

    def scan(n, states):
        s0 = pl.multiple_of(n * SCAN_ROWS, SCAN_ROWS)
        e0 = pl.multiple_of(n * SUBLANES, SUBLANES)
        r0 = pl.multiple_of(n * CHUNK, CHUNK)
        new_states = []
        for j, hc in enumerate(head_cols):
            state = states[j]
            prod = _dot(lin_ref[pl.ds(s0, SCAN_ROWS), hc], state.astype(bf16))
            off = off_ref[pl.ds(s0, SCAN_ROWS), hc]
            new_states.append(state * eg_ref[pl.ds(e0, 1), hc] + (off[:HEAD_DIM] - prod[:HEAD_DIM]))
            o = prod[HEAD_DIM:] + off[HEAD_DIM:]
            gate = zg_ref[pl.ds(r0, CHUNK), hc].astype(f32)
            o_ref[pl.ds(r0, CHUNK), hc] = (_rms_rows(o, ng_ref[...]) * _silu(gate)).astype(o_ref.dtype)
        return tuple(new_states)

    zero_state = jnp.zeros((HEAD_DIM, HEAD_DIM), f32)
    lax.fori_loop(0, t // CHUNK, scan, (zero_state,) * HEADS_PER_STEP)


def _gdn(z, gb, conv_w, norm_gain, batch, seq, l):
    m = z.shape[0]
    per_seg = WIDTH // STEP_WIDTH
    cq = COL_GDN_QKV // STEP_WIDTH
    cg = COL_GDN_GATE // STEP_WIDTH

    def head_spec(c):
        return pl.BlockSpec((seq, STEP_WIDTH), lambda b, h: (b, c + h))

    def conv_spec(seg):
        return pl.BlockSpec((None, CONV_WIDTH, STEP_WIDTH), lambda b, h: (l, 0, seg * per_seg + h))

    n_chunks = seq // CHUNK
    padded = pltpu.VMEM((seq + XPAD, STEP_WIDTH), bf16)
    return pl.pallas_call(
        _gdn_kernel,
        grid=(batch, per_seg),
        in_specs=[
            head_spec(cq), head_spec(cq + per_seg), head_spec(cq + 2 * per_seg), head_spec(cg),
            pl.BlockSpec((seq, LANES), lambda b, h: (b, 0)),
            conv_spec(0), conv_spec(1), conv_spec(2),
            pl.BlockSpec((None, 1, HEAD_DIM), lambda b, h: (l, 0, 0)),
        ],
        out_specs=pl.BlockSpec((seq, STEP_WIDTH), lambda b, h: (b, h)),
        out_shape=jax.ShapeDtypeStruct((m, WIDTH), bf16),
        scratch_shapes=[
            padded, padded, padded,
            pltpu.VMEM((3 * HEADS_PER_STEP, GDN_GROUP + XPAD, HEAD_DIM), f32),
            pltpu.VMEM((n_chunks * SCAN_ROWS, STEP_WIDTH), bf16),
            pltpu.VMEM((n_chunks * SCAN_ROWS, STEP_WIDTH), f32),
            pltpu.VMEM((n_chunks * SUBLANES, STEP_WIDTH), f32),
        ],
        compiler_params=_params("arbitrary", "arbitrary"),
        name="gdn",
    )(z, z, z, z, gb, conv_w, conv_w, conv_w, norm_gain)


def _rows(v):
    return v.astype(f32)[:, None, :]


def _pad_rows(v):
    return _rows(jnp.pad(v, ((0, 0), (0, LANES - v.shape[1]))))


def kernel(x, w_in, conv_w, a_log, dt_bias, gdn_norm_g, gmlp_ln_g, w_spatial, b_spatial, sba_q_g, sba_k_g, w_out_a, w_out_b, w_out_c, w_out, norm_mix_g, norm_mlp_g, w_ff1, w_ff2):
    batch, seq, d = x.shape
    depth = w_in.shape[0]
    m = batch * seq

    wt = jnp.swapaxes(w_in, 1, 2).reshape(depth * D_IN, d)
    b_rep = jnp.broadcast_to(b_spatial[..., None], b_spatial.shape + (LANES,)).astype(f32)
    mix_g, mlp_g, gdn_g, ln_g = _rows(norm_mix_g), _rows(norm_mlp_g), _rows(gdn_norm_g), _rows(gmlp_ln_g)
    q_g, k_g = _rows(sba_q_g), _rows(sba_k_g)
    alog_rows, dt_rows = _pad_rows(a_log), _pad_rows(dt_bias)
    conv_w = conv_w.astype(f32)
    w_spatial = w_spatial.astype(f32)

    xf = x.reshape(m, d).astype(f32)
    for l in range(depth):
        z, gb = _inproj(xf, mix_g, wt, alog_rows, dt_rows, l)
        oa = _gdn(z, gb, conv_w, gdn_g, batch, seq, l)
        ob = _gmlp(z, ln_g, w_spatial, b_rep, l)
        oc = _sba(z, q_g, k_g, batch, seq, l)
        y = _merge(oa, ob, oc, w_out_a, w_out_b, w_out_c, z, l)
        xf = _matmul_res(y, w_out, xf, l)
        h1 = _ffn_up(xf, mlp_g, w_ff1, l)
        xf = _matmul_res(h1, w_ff2, xf, l)
    return xf.reshape(batch, seq, d).astype(x.dtype)
```

```python
import jax
import jax.numpy as jnp
import numpy as np
from jax import lax
from jax.experimental import pallas as pl
from jax.experimental.pallas import tpu as pltpu

f32 = jnp.float32
bf16 = jnp.bfloat16

D_MODEL = 2048
CHUNK = 64
HEAD_DIM = 128
N_HEADS = 8
WIDTH = N_HEADS * HEAD_DIM
CONV_WIDTH = 4
GMLP_WIDTH = 1024
GMLP_GROUPS = 8
GMLP_BLOCK = 128
N_BRANCHES = 3
EPS = 1e-6
LOG2_E = float(np.log2(np.e))
UNDERFLOW_LOG2 = 160.0

LANES = 128
SUBLANES = 8
BF16_ROWS = 16
VMEM_LIMIT_BYTES = 60000 * 1024

COL_GDN_QKV = 0
COL_GDN_GATE = 3 * WIDTH
COL_GMLP_UV = COL_GDN_GATE + WIDTH
COL_SBA_QKV = COL_GMLP_UV + 2 * GMLP_WIDTH
COL_GATES = COL_SBA_QKV + 3 * WIDTH
N_MAIN = COL_GATES + N_BRANCHES * D_MODEL
ORIG_COL_AB = 3 * WIDTH
N_AB = 2 * N_HEADS
D_IN = N_MAIN + N_AB

PROJ_TM = 2048
PROJ_TN = 512
RES_TM = 2048
RES_TN = 1024
RES_TK = 1024
MERGE_TM = 512

HEADS_PER_STEP = 4
STEP_WIDTH = HEADS_PER_STEP * HEAD_DIM
GDN_GROUP = 512
SBA_TQ = 128
SBA_TK = 512
SBA_LEFT_TK = 256
SBA_SUB = 128
GMLP_ROWS = 512


def _params(*sem):
    return pltpu.CompilerParams(dimension_semantics=sem, vmem_limit_bytes=VMEM_LIMIT_BYTES)


def _dot(a, b):
    return jnp.dot(a, b, preferred_element_type=f32)


def _dot_nt(a, b):
    return lax.dot_general(a, b, (((1,), (1,)), ((), ())), preferred_element_type=f32)


def _sigmoid(x):
    return lax.logistic(x)


def _silu(x):
    return x * _sigmoid(x)


def _softplus(x):
    return jnp.maximum(x, 0.0) + jnp.log1p(jnp.exp(-jnp.abs(x)))


def _block_id(idx, size):
    return lax.shift_right_logical(idx, int(size).bit_length() - 1)


def _split3(x):
    hi = x.astype(bf16)
    r1 = x - hi.astype(f32)
    mid = r1.astype(bf16)
    lo = (r1 - mid.astype(f32)).astype(bf16)
    return jnp.concatenate([hi, mid, lo], axis=-1)


def _sum3(x, n):
    return x[..., :n] + x[..., n:2 * n] + x[..., 2 * n:3 * n]


def _rms_rows(x, gain):
    ms = jnp.mean(x * x, axis=-1, keepdims=True)
    return x * lax.rsqrt(ms + EPS) * gain


NORM_ROWS = 256


def _for_row_chunks(n_rows, body):
    def step(c, carry):
        body(pl.ds(pl.multiple_of(c * NORM_ROWS, NORM_ROWS), NORM_ROWS))
        return carry

    lax.fori_loop(0, n_rows // NORM_ROWS, step, 0)


def _inproj_kernel(x_ref, g_ref, wt_ref, wabt_ref, alog_ref, dt_ref, z_ref, gb_ref, hn_ref):
    @pl.when(pl.program_id(1) == 0)
    def _():
        wab = wabt_ref[...].astype(bf16)

        def rows_step(rows):
            hn = _rms_rows(x_ref[rows, :], g_ref[...]).astype(bf16)
            hn_ref[rows, :] = hn
            ab = _dot_nt(hn, wab)
            lane = lax.broadcasted_iota(jnp.int32, ab.shape, 1)
            g = -jnp.exp(alog_ref[...]) * _softplus(ab + dt_ref[...])
            gb_ref[rows, :] = jnp.where(lane < N_HEADS, g, _sigmoid(ab))

        _for_row_chunks(x_ref.shape[0], rows_step)

    z_ref[...] = _dot_nt(hn_ref[...], wt_ref[...].astype(bf16)).astype(z_ref.dtype)


def _inproj(x, gain, wt, alog_row, dt_row, l):
    m = x.shape[0]
    n_direct = ORIG_COL_AB // PROJ_TN

    def w_rows(i, j):
        return pl.multiple_of(l * D_IN + j * PROJ_TN + jnp.where(j >= n_direct, N_AB, 0), SUBLANES), 0

    return pl.pallas_call(
        _inproj_kernel,
        grid=(m // PROJ_TM, N_MAIN // PROJ_TN),
        in_specs=[
            pl.BlockSpec((PROJ_TM, D_MODEL), lambda i, j: (i, 0), pipeline_mode=pl.Buffered(1)),
            pl.BlockSpec((None, 1, D_MODEL), lambda i, j: (l, 0, 0)),
            pl.BlockSpec((pl.Element(PROJ_TN), pl.Element(D_MODEL)), w_rows),
            pl.BlockSpec((pl.Element(LANES), pl.Element(D_MODEL)), lambda i, j: (l * D_IN + ORIG_COL_AB, 0),
                         pipeline_mode=pl.Buffered(1)),
            pl.BlockSpec((None, 1, LANES), lambda i, j: (l, 0, 0)),
            pl.BlockSpec((None, 1, LANES), lambda i, j: (l, 0, 0)),
        ],
        out_specs=[
            pl.BlockSpec((PROJ_TM, PROJ_TN), lambda i, j: (i, j)),
            pl.BlockSpec((PROJ_TM, LANES), lambda i, j: (i, 0)),
        ],
        out_shape=[
            jax.ShapeDtypeStruct((m, N_MAIN), bf16),
            jax.ShapeDtypeStruct((m, LANES), f32),
        ],
        scratch_shapes=[pltpu.VMEM((PROJ_TM, D_MODEL), bf16)],
        compiler_params=_params("arbitrary", "arbitrary"),
        name="inproj",
    )(x, gain, wt, wt, alog_row, dt_row)


def _ffn_up_kernel(x_ref, g_ref, w_ref, h_ref, hn_ref):
    @pl.when(pl.program_id(1) == 0)
    def _():
        def rows_step(rows):
            hn_ref[rows, :] = _rms_rows(x_ref[rows, :], g_ref[...]).astype(bf16)

        _for_row_chunks(x_ref.shape[0], rows_step)

    a = jnp.maximum(_dot(hn_ref[...], w_ref[...].astype(bf16)), 0.0)
    h_ref[...] = (a * a).astype(h_ref.dtype)


def _ffn_up(x, gain, w1, l):
    m, n = x.shape[0], w1.shape[-1]
    return pl.pallas_call(
        _ffn_up_kernel,
        grid=(m // PROJ_TM, n // PROJ_TN),
        in_specs=[
            pl.BlockSpec((PROJ_TM, D_MODEL), lambda i, j: (i, 0), pipeline_mode=pl.Buffered(1)),
            pl.BlockSpec((None, 1, D_MODEL), lambda i, j: (l, 0, 0)),
            pl.BlockSpec((None, D_MODEL, PROJ_TN), lambda i, j: (l, 0, j)),
        ],
        out_specs=pl.BlockSpec((PROJ_TM, PROJ_TN), lambda i, j: (i, j)),
        out_shape=jax.ShapeDtypeStruct((m, n), bf16),
        scratch_shapes=[pltpu.VMEM((PROJ_TM, D_MODEL), bf16)],
        compiler_params=_params("arbitrary", "arbitrary"),
        name="ffn_up",
    )(x, gain, w1)


def _matmul_res_kernel(a_ref, w_ref, r_ref, o_ref):
    d = _dot(a_ref[...], w_ref[...].astype(bf16))

    @pl.when(pl.program_id(2) == 0)
    def _():
        o_ref[...] = r_ref[...] + d

    @pl.when(pl.program_id(2) != 0)
    def _():
        o_ref[...] += d


def _matmul_res(a, w, res, l):
    m, k = a.shape
    n = w.shape[-1]
    tm, tk = (RES_TM // 2, 2 * RES_TK) if k > D_MODEL else (RES_TM, RES_TK)
    return pl.pallas_call(
        _matmul_res_kernel,
        grid=(m // tm, n // RES_TN, k // tk),
        in_specs=[
            pl.BlockSpec((tm, tk), lambda i, j, kk: (i, kk)),
            pl.BlockSpec((None, tk, RES_TN), lambda i, j, kk: (l, kk, j)),
            pl.BlockSpec((tm, RES_TN), lambda i, j, kk: (i, j)),
        ],
        out_specs=pl.BlockSpec((tm, RES_TN), lambda i, j, kk: (i, j)),
        out_shape=jax.ShapeDtypeStruct((m, n), f32),
        compiler_params=_params("arbitrary", "arbitrary", "arbitrary"),
        name="matmul_res",
    )(a, w, res)


MERGE_GATE_COLS = 1024
MERGE_GATE_BLOCKS = D_MODEL // MERGE_GATE_COLS


def _merge_kernel(oa_ref, ob_ref, oc_ref, wa_ref, wb_ref, wc_ref, *rest):
    gate_refs, y_ref = rest[:-1], rest[-1]
    branches = ((oa_ref, wa_ref), (ob_ref, wb_ref), (oc_ref, wc_ref))
    for c in range(MERGE_GATE_BLOCKS):
        cols = slice(c * MERGE_GATE_COLS, (c + 1) * MERGE_GATE_COLS)
        y = None
        for b, (o_ref, w_ref) in enumerate(branches):
            gate = _sigmoid(gate_refs[b * MERGE_GATE_BLOCKS + c][...].astype(f32))
            term = gate * _dot(o_ref[...], w_ref[:, cols].astype(bf16))
            y = term if y is None else y + term
        y_ref[:, cols] = y.astype(y_ref.dtype)


def _merge(oa, ob, oc, wa, wb, wc, z, l):
    m = oa.shape[0]
    gate0 = COL_GATES // MERGE_GATE_COLS
    branch_in = pl.BlockSpec((MERGE_TM, WIDTH), lambda i: (i, 0))
    branch_w = pl.BlockSpec((None, WIDTH, D_MODEL), lambda i: (l, 0, 0), pipeline_mode=pl.Buffered(1))
    gate_specs = [pl.BlockSpec((MERGE_TM, MERGE_GATE_COLS), lambda i, blk=gate0 + g: (i, blk))
                  for g in range(N_BRANCHES * MERGE_GATE_BLOCKS)]
    return pl.pallas_call(
        _merge_kernel,
        grid=(m // MERGE_TM,),
        in_specs=[branch_in, branch_in, branch_in, branch_w, branch_w, branch_w] + gate_specs,
        out_specs=pl.BlockSpec((MERGE_TM, D_MODEL), lambda i: (i, 0)),
        out_shape=jax.ShapeDtypeStruct((m, D_MODEL), bf16),
        compiler_params=_params("arbitrary"),
        name="merge",
    )(oa, ob, oc, wa, wb, wc, *([z] * (N_BRANCHES * MERGE_GATE_BLOCKS)))


def _gelu(x):
    return 0.5 * x * (1.0 + lax.erf(x * np.float32(np.sqrt(0.5))))


def _gmlp_kernel(zu_ref, zv_ref, g_ref, ws_ref, bs_ref, o_ref):
    u = _gelu(zu_ref[...].astype(f32))
    v = _gelu(zv_ref[...].astype(f32))
    mu = jnp.mean(v, axis=-1, keepdims=True)
    vc = v - mu
    var = jnp.mean(vc * vc, axis=-1, keepdims=True)
    vb = (vc * lax.rsqrt(var + EPS) * g_ref[...]).astype(bf16)

    t_chunk = _block_id(lax.broadcasted_iota(jnp.int32, (GMLP_BLOCK, GMLP_BLOCK), 0), CHUNK)
    s_chunk = _block_id(lax.broadcasted_iota(jnp.int32, (GMLP_BLOCK, GMLP_BLOCK), 1), CHUNK)
    causal = s_chunk <= t_chunk
    gdim = GMLP_WIDTH // GMLP_GROUPS
    for g in range(GMLP_GROUPS):
        ws = jnp.where(causal, ws_ref[g], 0.0).astype(bf16)
        cols = slice(g * gdim, (g + 1) * gdim)
        for blk in range(GMLP_ROWS // GMLP_BLOCK):
            rows = slice(blk * GMLP_BLOCK, (blk + 1) * GMLP_BLOCK)
            s = _dot(ws, vb[rows, cols]) + bs_ref[g]
            o_ref[rows, cols] = (u[rows, cols] * s).astype(o_ref.dtype)


def _gmlp(z, ln_gain, w_spatial, b_rep, l):
    m = z.shape[0]
    cu = COL_GMLP_UV // GMLP_WIDTH
    return pl.pallas_call(
        _gmlp_kernel,
        grid=(m // GMLP_ROWS,),
        in_specs=[
            pl.BlockSpec((GMLP_ROWS, GMLP_WIDTH), lambda i: (i, cu)),
            pl.BlockSpec((GMLP_ROWS, GMLP_WIDTH), lambda i: (i, cu + 1)),
            pl.BlockSpec((None, 1, GMLP_WIDTH), lambda i: (l, 0, 0)),
            pl.BlockSpec((None, GMLP_GROUPS, GMLP_BLOCK, GMLP_BLOCK), lambda i: (l, 0, 0, 0)),
            pl.BlockSpec((None, GMLP_GROUPS, GMLP_BLOCK, LANES), lambda i: (l, 0, 0, 0)),
        ],
        out_specs=pl.BlockSpec((GMLP_ROWS, GMLP_WIDTH), lambda i: (i, 0)),
        out_shape=jax.ShapeDtypeStruct((m, GMLP_WIDTH), bf16),
        compiler_params=_params("arbitrary"),
        name="gmlp",
    )(z, z, ln_gain, w_spatial, b_rep)


def _sba_kernel(q_ref, k_ref, v_ref, qg_ref, kg_ref, o_ref, qn_ref, kn_ref):
    t = q_ref.shape[0]
    head_cols = [slice(j * HEAD_DIM, (j + 1) * HEAD_DIM) for j in range(HEADS_PER_STEP)]
    for hc in head_cols:
        qn_ref[:, hc] = _rms_rows(q_ref[:, hc].astype(f32), qg_ref[...]).astype(bf16)
        kn_ref[:, hc] = _rms_rows(k_ref[:, hc].astype(f32), kg_ref[...]).astype(bf16)
    to_log2 = np.float32(HEAD_DIM ** -0.5 * LOG2_E)
    rows_all = HEADS_PER_STEP * SBA_TQ
    wr = lax.broadcasted_iota(jnp.int32, (2 * SBA_SUB, 2 * SBA_SUB), 0)
    wc = lax.broadcasted_iota(jnp.int32, (2 * SBA_SUB, 2 * SBA_SUB), 1)
    suffix_w = ((jnp.bitwise_and(wr, SBA_SUB - 1) > wc) | (wc >= SBA_SUB)).astype(bf16)

    below_diag = (lax.broadcasted_iota(jnp.int32, (rows_all, SBA_SUB), 1)
                  < jnp.bitwise_and(lax.broadcasted_iota(jnp.int32, (rows_all, SBA_SUB), 0), SBA_TQ - 1))

    def block(qbs, k0, n_cols, diagonal, state):
        acc, run = state
        z = jnp.concatenate([_dot_nt(qbs[j], kn_ref[pl.ds(k0, n_cols), head_cols[j]])
                             for j in range(HEADS_PER_STEP)], axis=0) * to_log2
        neg_abs = pltpu.bitcast(pltpu.bitcast(z, jnp.uint32) | jnp.uint32(0x80000000), f32)
        soft = jnp.log(1.0 + jnp.exp2(neg_abs)) * np.float32(LOG2_E)
        drop = jnp.maximum(z, 0.0) + soft
        log_beta = z - drop
        probs = []
        offs = run
        last = n_cols // SBA_SUB - 1
        for c in range(last, -1, -1):
            cols = slice(c * SBA_SUB, (c + 1) * SBA_SUB)
            masked = diagonal and c == last
            dc = drop[:, cols]
            if masked:
                dc = jnp.where(below_diag, dc, 0.0)
            hi = dc.astype(bf16)
            lo = (dc - hi.astype(f32)).astype(bf16)
            sums = _dot(jnp.concatenate([hi, lo], axis=-1), suffix_w)
            a = jnp.exp2(log_beta[:, cols] - sums[:, :SBA_SUB] - offs)
            if masked:
                a = jnp.where(below_diag, a, 0.0)
            probs.append(a.astype(bf16))
            offs = offs + sums[:, SBA_SUB:]
        a_blk = probs[0] if last == 0 else jnp.concatenate(probs[::-1], axis=-1)
        pv = jnp.concatenate(
            [_dot(a_blk[j * SBA_TQ:(j + 1) * SBA_TQ], v_ref[pl.ds(k0, n_cols), head_cols[j]])
             for j in range(HEADS_PER_STEP)], axis=0)
        return acc + pv, offs

    def key_block_row(s, carry):
        k_diag = pl.multiple_of(s * SBA_TK, SBA_TK)
        n_q = SBA_TK // SBA_TQ
        q0s = [pl.multiple_of(k_diag + p * SBA_TQ, SBA_TQ) for p in range(n_q)]
        qbss = [[qn_ref[pl.ds(q0, SBA_TQ), hc] for hc in head_cols] for q0 in q0s]
        zeros = jnp.zeros((rows_all, HEAD_DIM), f32)
        states = [block(qbss[p], k_diag, (p + 1) * SBA_SUB, True, (zeros, zeros)) for p in range(n_q)]
        for p in range(n_q):
            q0, qbs, state = q0s[p], qbss[p], states[p]

            n_left = s * (SBA_TK // SBA_LEFT_TK)

            def more(c):
                it, _, run = c
                return (it < n_left) & (jnp.min(run) <= UNDERFLOW_LOG2)

            def left_block(c):
                it, acc, run = c
                k0 = pl.multiple_of((n_left - 1 - it) * SBA_LEFT_TK, SBA_LEFT_TK)
                acc, run = block(qbs, k0, SBA_LEFT_TK, False, (acc, run))
                return it + 1, acc, run

            _, acc, _ = lax.while_loop(more, left_block, (jnp.int32(0),) + state)
            for j in range(HEADS_PER_STEP):
                o_ref[pl.ds(q0, SBA_TQ), head_cols[j]] = acc[j * SBA_TQ:(j + 1) * SBA_TQ].astype(o_ref.dtype)
        return carry

    lax.fori_loop(0, t // SBA_TK, key_block_row, 0)


def _sba(z, q_gain, k_gain, batch, seq, l):
    m = z.shape[0]
    c0 = COL_SBA_QKV // STEP_WIDTH
    per_seg = WIDTH // STEP_WIDTH

    def head_spec(seg):
        return pl.BlockSpec((seq, STEP_WIDTH), lambda b, h: (b, c0 + seg * per_seg + h))

    gain_spec = pl.BlockSpec((None, 1, HEAD_DIM), lambda b, h: (l, 0, 0))
    tok = pltpu.VMEM((seq, STEP_WIDTH), bf16)
    return pl.pallas_call(
        _sba_kernel,
        grid=(batch, per_seg),
        in_specs=[head_spec(0), head_spec(1), head_spec(2), gain_spec, gain_spec],
        out_specs=pl.BlockSpec((seq, STEP_WIDTH), lambda b, h: (b, h)),
        out_shape=jax.ShapeDtypeStruct((m, WIDTH), bf16),
        scratch_shapes=[tok, tok],
        compiler_params=_params("arbitrary", "arbitrary"),
        name="sba",
    )(z, z, z, q_gain, k_gain)


XPAD = BF16_ROWS


def _bmm(a, b):
    return jnp.einsum('bij,bjk->bik', a, b, preferred_element_type=f32)


def _bmm_nt(a, b):
    return jnp.einsum('bid,bjd->bij', a, b, preferred_element_type=f32)


def _bmm_tn(a, b):
    return jnp.einsum('bck,bcv->bkv', a, b, preferred_element_type=f32)


def _unit_lower_inverse(l_mat, row, col):
    def same_block(d):
        return _block_id(row, d) == _block_id(col, d)

    eye = (row == col).astype(f32)
    l8 = jnp.where(same_block(8), l_mat, 0.0)
    p = l8.astype(bf16)
    x = (eye - l8).astype(bf16)
    p2 = _bmm(p, p)
    x = _bmm(x, (eye + p2).astype(bf16)).astype(bf16)
    p2 = p2.astype(bf16)
    p4 = _bmm(p2, p2)
    x = _bmm(x, (eye + p4).astype(bf16)).astype(bf16)
    d = 8
    while d < CHUNK:
        off = jnp.where(same_block(2 * d) & jnp.logical_not(same_block(d)), l_mat, 0.0).astype(bf16)
        x = _bmm(x, (eye - _bmm(off, x)).astype(bf16)).astype(bf16)
        d *= 2
    return x


SCAN_ROWS = HEAD_DIM + CHUNK


def _gdn_kernel(zq_ref, zk_ref, zv_ref, zg_ref, gb_ref, cwq_ref, cwk_ref, cwv_ref, ng_ref, o_ref,
                xq_ref, xk_ref, xv_ref, stage_ref, lin_ref, off_ref, eg_ref):
    t = zq_ref.shape[0]
    head0 = pl.program_id(1) * HEADS_PER_STEP
    grp = GDN_GROUP
    nc = grp // CHUNK
    head_cols = [slice(j * HEAD_DIM, (j + 1) * HEAD_DIM) for j in range(HEADS_PER_STEP)]

    zeros_pad = jnp.zeros((XPAD, STEP_WIDTH), bf16)
    for src, dst in ((zq_ref, xq_ref), (zk_ref, xk_ref), (zv_ref, xv_ref)):
        dst[pl.ds(0, XPAD), :] = zeros_pad
        dst[pl.ds(XPAD, t), :] = src[...]

    row = lax.broadcasted_iota(jnp.int32, (CHUNK, CHUNK), 0)
    col = lax.broadcasted_iota(jnp.int32, (CHUNK, CHUNK), 1)
    incl = row >= col
    strict = row > col
    tri_ones = jnp.concatenate([incl.astype(bf16), jnp.ones((CHUNK, CHUNK), bf16)], axis=0)
    tri_ones = jnp.broadcast_to(tri_ones, (nc, 2 * CHUNK, CHUNK))
    sel_src = jnp.bitwise_and(lax.broadcasted_iota(jnp.int32, (3 * LANES, STEP_WIDTH), 0), LANES - 1)
    sel_head = head0 + _block_id(lax.broadcasted_iota(jnp.int32, (3 * LANES, STEP_WIDTH), 1), HEAD_DIM)
    pick_g = (sel_src == sel_head).astype(bf16)
    pick_b = (sel_src == sel_head + N_HEADS).astype(bf16)
    sel_rows = (jnp.bitwise_and(lax.broadcasted_iota(jnp.int32, (SUBLANES, 3 * LANES), 1), LANES - 1)
                == head0 + lax.broadcasted_iota(jnp.int32, (SUBLANES, 3 * LANES), 0)).astype(bf16)
    sel_rows = jnp.broadcast_to(sel_rows, (nc, SUBLANES, 3 * LANES))

    def conv_silu(x_ref, cw_ref, r0, hc, slot):
        stage = stage_ref.at[slot]
        stage[...] = x_ref[pl.ds(r0, grp + XPAD), hc].astype(f32)
        acc = cw_ref[pl.ds(CONV_WIDTH - 1, 1), hc] * stage[pl.ds(XPAD, grp), :]
        for i in range(CONV_WIDTH - 1):
            acc += cw_ref[pl.ds(i, 1), hc] * stage[pl.ds(XPAD - (CONV_WIDTH - 1) + i, grp), :]
        return _silu(acc)

    def l2n(x):
        return x * lax.rsqrt(jnp.sum(x * x, axis=-1, keepdims=True) + EPS)

    def chunked(x):
        return x.reshape(nc, CHUNK, x.shape[-1])

    def slot_of(g):
        return jnp.bitwise_and(g, 1)

    def prepare(it, carry):
        r0 = pl.multiple_of(it * grp, grp)
        gb3 = _split3(gb_ref[pl.ds(r0, grp), :])
        ct = _bmm(tri_ones, chunked(gb3))
        cum3 = _split3(_sum3(ct[:, :CHUNK], LANES).reshape(grp, LANES))
        tot3 = _split3(_sum3(ct[:, CHUNK:], LANES).reshape(grp, LANES))
        gc_rows = _bmm_nt(sel_rows, chunked(cum3))

        gc_all = _dot(cum3, pick_g)
        g_tot_all = _dot(tot3, pick_g)
        beta_all = _dot(gb3, pick_b)

        k16s, kb16s, q16s, rhs16s, diffs, qds, kd16s = [], [], [], [], [], [], []
        for j, hc in enumerate(head_cols):
            gc, g_tot, beta = gc_all[:, hc], g_tot_all[:, hc], beta_all[:, hc]
            q = l2n(conv_silu(xq_ref, cwq_ref, r0, hc, 3 * j)) * np.float32(HEAD_DIM ** -0.5)
            k = l2n(conv_silu(xk_ref, cwk_ref, r0, hc, 3 * j + 1))
            v = conv_silu(xv_ref, cwv_ref, r0, hc, 3 * j + 2)
            kb = k * beta
            e_gc = jnp.exp(gc)
            k16s.append(chunked(k.astype(bf16)))
            kb16s.append(chunked(kb.astype(bf16)))
            q16s.append(chunked(q.astype(bf16)))
            rhs16s.append(chunked(jnp.concatenate([v * beta, kb * e_gc], axis=-1).astype(bf16)))
            diffs.append(chunked(gc)[:, :, :CHUNK] - gc_rows[:, j:j + 1, :])
            qds.append(chunked(q * e_gc))
            kd16s.append(chunked((k * jnp.exp(g_tot - gc)).astype(bf16)))
            e0 = pl.multiple_of(slot_of(it) * (nc * SUBLANES), nc * SUBLANES)
            eg_ref[pl.ds(e0, nc * SUBLANES), hc] = (
                chunked(jnp.exp(g_tot))[:, :SUBLANES, :].reshape(nc * SUBLANES, HEAD_DIM))

        cat = lambda xs: jnp.concatenate(xs, axis=0)
        k16, diff = cat(k16s), cat(diffs)
        decay = jnp.where(incl, jnp.exp(jnp.where(incl, diff, 0.0)), 0.0)
        l_mat = jnp.where(strict, _bmm_nt(cat(kb16s), k16) * decay, 0.0)
        intra = (_bmm_nt(cat(q16s), k16) * decay).astype(bf16)
        t_inv = _unit_lower_inverse(l_mat, row, col)
        uw = _bmm(t_inv, cat(rhs16s)).astype(bf16)
        state_map = _bmm_tn(cat(kd16s), uw)
        out_map = _bmm(intra, uw)
        q_eff = cat(qds) - out_map[:, :, HEAD_DIM:]
        lin = jnp.concatenate([state_map[:, :, HEAD_DIM:], q_eff], axis=1).astype(bf16)
        off = jnp.concatenate([state_map[:, :, :HEAD_DIM], out_map[:, :, :HEAD_DIM]], axis=1)
        s0 = pl.multiple_of(slot_of(it) * (nc * SCAN_ROWS), nc * SCAN_ROWS)
        for j, hc in enumerate(head_cols):
            mine = slice(j * nc, (j + 1) * nc)
            lin_ref[pl.ds(s0, nc * SCAN_ROWS), hc] = lin[mine].reshape(nc * SCAN_ROWS, HEAD_DIM)
            off_ref[pl.ds(s0, nc * SCAN_ROWS), hc] = off[mine].reshape(nc * SCAN_ROWS, HEAD_DIM)
        return carry

    def scan(slot_chunk, out_chunk, states):
        s0 = pl.multiple_of(slot_chunk * SCAN_ROWS, SCAN_ROWS)
        e0 = pl.multiple_of(slot_chunk * SUBLANES, SUBLANES)
        r0 = pl.multiple_of(out_chunk * CHUNK, CHUNK)
        new_states = []
        for j, hc in enumerate(head_cols):
            state = states[j]
            prod = _dot(lin_ref[pl.ds(s0, SCAN_ROWS), hc], state.astype(bf16))
            off = off_ref[pl.ds(s0, SCAN_ROWS), hc]
            new_states.append(state * eg_ref[pl.ds(e0, 1), hc] + (off[:HEAD_DIM] - prod[:HEAD_DIM]))
            o = prod[HEAD_DIM:] + off[HEAD_DIM:]
            gate = zg_ref[pl.ds(r0, CHUNK), hc].astype(f32)
            o_ref[pl.ds(r0, CHUNK), hc] = (_rms_rows(o, ng_ref[...]) * _silu(gate)).astype(o_ref.dtype)
        return tuple(new_states)

    def scan_group(slot, out_group, states):
        for c in range(nc):
            states = scan(slot * nc + c, out_group * nc + c, states)
        return states

    def step(g, states):
        states = scan_group(slot_of(g + 1), jnp.maximum(g - 1, 0), states)
        prepare(g, 0)
        return states

    n_groups = t // grp
    lin_ref[pl.ds(nc * SCAN_ROWS, nc * SCAN_ROWS), :] = jnp.zeros((nc * SCAN_ROWS, STEP_WIDTH), bf16)
    off_ref[pl.ds(nc * SCAN_ROWS, nc * SCAN_ROWS), :] = jnp.zeros((nc * SCAN_ROWS, STEP_WIDTH), f32)
    eg_ref[pl.ds(nc * SUBLANES, nc * SUBLANES), :] = jnp.zeros((nc * SUBLANES, STEP_WIDTH), f32)
    zero_state = jnp.zeros((HEAD_DIM, HEAD_DIM), f32)
    states = lax.fori_loop(0, n_groups, step, (zero_state,) * HEADS_PER_STEP)
    scan_group(jnp.int32((n_groups - 1) % 2), jnp.int32(n_groups - 1), states)


def _gdn(z, gb, conv_w, norm_gain, batch, seq, l):
    m = z.shape[0]
    per_seg = WIDTH // STEP_WIDTH
    cq = COL_GDN_QKV // STEP_WIDTH
    cg = COL_GDN_GATE // STEP_WIDTH

    def head_spec(c):
        return pl.BlockSpec((seq, STEP_WIDTH), lambda b, h: (b, c + h))

    def conv_spec(seg):
        return pl.BlockSpec((None, CONV_WIDTH, STEP_WIDTH), lambda b, h: (l, 0, seg * per_seg + h))

    ring_chunks = 2 * GDN_GROUP // CHUNK
    padded = pltpu.VMEM((seq + XPAD, STEP_WIDTH), bf16)
    return pl.pallas_call(
        _gdn_kernel,
        grid=(batch, per_seg),
        in_specs=[
            head_spec(cq), head_spec(cq + per_seg), head_spec(cq + 2 * per_seg), head_spec(cg),
            pl.BlockSpec((seq, LANES), lambda b, h: (b, 0)),
            conv_spec(0), conv_spec(1), conv_spec(2),
            pl.BlockSpec((None, 1, HEAD_DIM), lambda b, h: (l, 0, 0)),
        ],
        out_specs=pl.BlockSpec((seq, STEP_WIDTH), lambda b, h: (b, h)),
        out_shape=jax.ShapeDtypeStruct((m, WIDTH), bf16),
        scratch_shapes=[
            padded, padded, padded,
            pltpu.VMEM((3 * HEADS_PER_STEP, GDN_GROUP + XPAD, HEAD_DIM), f32),
            pltpu.VMEM((ring_chunks * SCAN_ROWS, STEP_WIDTH), bf16),
            pltpu.VMEM((ring_chunks * SCAN_ROWS, STEP_WIDTH), f32),
            pltpu.VMEM((ring_chunks * SUBLANES, STEP_WIDTH), f32),
        ],
        compiler_params=_params("arbitrary", "arbitrary"),
        name="gdn",
    )(z, z, z, z, gb, conv_w, conv_w, conv_w, norm_gain)


def _rows(v):
    return v.astype(f32)[:, None, :]


def _pad_rows(v):
    return _rows(jnp.pad(v, ((0, 0), (0, LANES - v.shape[1]))))


def kernel(x, w_in, conv_w, a_log, dt_bias, gdn_norm_g, gmlp_ln_g, w_spatial, b_spatial, sba_q_g, sba_k_g, w_out_a, w_out_b, w_out_c, w_out, norm_mix_g, norm_mlp_g, w_ff1, w_ff2):
    batch, seq, d = x.shape
    depth = w_in.shape[0]
    m = batch * seq

    wt = jnp.swapaxes(w_in, 1, 2).reshape(depth * D_IN, d)
    b_rep = jnp.broadcast_to(b_spatial[..., None], b_spatial.shape + (LANES,)).astype(f32)
    mix_g, mlp_g, gdn_g, ln_g = _rows(norm_mix_g), _rows(norm_mlp_g), _rows(gdn_norm_g), _rows(gmlp_ln_g)
    q_g, k_g = _rows(sba_q_g), _rows(sba_k_g)
    alog_rows, dt_rows = _pad_rows(a_log), _pad_rows(dt_bias)
    conv_w = conv_w.astype(f32)
    w_spatial = w_spatial.astype(f32)

    xf = x.reshape(m, d).astype(f32)
    for l in range(depth):
        z, gb = _inproj(xf, mix_g, wt, alog_rows, dt_rows, l)
        oa = _gdn(z, gb, conv_w, gdn_g, batch, seq, l)
        ob = _gmlp(z, ln_g, w_spatial, b_rep, l)
        oc = _sba(z, q_g, k_g, batch, seq, l)
        y = _merge(oa, ob, oc, w_out_a, w_out_b, w_out_c, z, l)
        xf = _matmul_res(y, w_out, xf, l)
        h1 = _ffn_up(xf, mlp_g, w_ff1, l)
        xf = _matmul_res(h1, w_ff2, xf, l)
    return xf.reshape(batch, seq, d).astype(x.dtype)
```

```python
import jax
import jax.numpy as jnp
import numpy as np
from jax import lax
from jax.experimental import pallas as pl
from jax.experimental.pallas import tpu as pltpu

f32 = jnp.float32
bf16 = jnp.bfloat16

D_MODEL = 2048
CHUNK = 64
HEAD_DIM = 128
N_HEADS = 8
WIDTH = N_HEADS * HEAD_DIM
CONV_WIDTH = 4
GMLP_WIDTH = 1024
GMLP_GROUPS = 8
GMLP_BLOCK = 128
N_BRANCHES = 3
EPS = 1e-6
LOG2_E = float(np.log2(np.e))
UNDERFLOW_LOG2 = 160.0

LANES = 128
SUBLANES = 8
BF16_ROWS = 16
VMEM_LIMIT_BYTES = 60000 * 1024

COL_GDN_QKV = 0
COL_GDN_GATE = 3 * WIDTH
COL_GMLP_UV = COL_GDN_GATE + WIDTH
COL_SBA_QKV = COL_GMLP_UV + 2 * GMLP_WIDTH
COL_GATES = COL_SBA_QKV + 3 * WIDTH
N_MAIN = COL_GATES + N_BRANCHES * D_MODEL
ORIG_COL_AB = 3 * WIDTH
N_AB = 2 * N_HEADS
D_IN = N_MAIN + N_AB

PROJ_TM = 2048
PROJ_TN = 512
RES_TM = 2048
RES_TN = 1024
RES_TK = 1024
MERGE_TM = 512

HEADS_PER_STEP = 4
STEP_WIDTH = HEADS_PER_STEP * HEAD_DIM
GDN_GROUP = 512
SBA_TQ = 128
SBA_TK = 512
SBA_LEFT_TK = 256
SBA_EAGER_LEFT = 2
SBA_SUB = 128
GMLP_ROWS = 512


def _params(*sem):
    return pltpu.CompilerParams(dimension_semantics=sem, vmem_limit_bytes=VMEM_LIMIT_BYTES)


def _dot(a, b):
    return jnp.dot(a, b, preferred_element_type=f32)


def _dot_nt(a, b):
    return lax.dot_general(a, b, (((1,), (1,)), ((), ())), preferred_element_type=f32)


def _sigmoid(x):
    return lax.logistic(x)


def _silu(x):
    return x * _sigmoid(x)


def _softplus(x):
    return jnp.maximum(x, 0.0) + jnp.log1p(jnp.exp(-jnp.abs(x)))


def _block_id(idx, size):
    return lax.shift_right_logical(idx, int(size).bit_length() - 1)


def _split3(x):
    hi = x.astype(bf16)
    r1 = x - hi.astype(f32)
    mid = r1.astype(bf16)
    lo = (r1 - mid.astype(f32)).astype(bf16)
    return jnp.concatenate([hi, mid, lo], axis=-1)


def _sum3(x, n):
    return x[..., :n] + x[..., n:2 * n] + x[..., 2 * n:3 * n]


def _rms_rows(x, gain):
    ms = jnp.mean(x * x, axis=-1, keepdims=True)
    return x * lax.rsqrt(ms + EPS) * gain


NORM_ROWS = 256


def _for_row_chunks(n_rows, body):
    def step(c, carry):
        body(pl.ds(pl.multiple_of(c * NORM_ROWS, NORM_ROWS), NORM_ROWS))
        return carry

    lax.fori_loop(0, n_rows // NORM_ROWS, step, 0)


def _inproj_kernel(x_ref, g_ref, wt_ref, wabt_ref, alog_ref, dt_ref, z_ref, gb_ref, hn_ref):
    @pl.when(pl.program_id(1) == 0)
    def _():
        wab = wabt_ref[...].astype(bf16)

        def rows_step(rows):
            hn = _rms_rows(x_ref[rows, :], g_ref[...]).astype(bf16)
            hn_ref[rows, :] = hn
            ab = _dot_nt(hn, wab)
            lane = lax.broadcasted_iota(jnp.int32, ab.shape, 1)
            g = -jnp.exp(alog_ref[...]) * _softplus(ab + dt_ref[...])
            gb_ref[rows, :] = jnp.where(lane < N_HEADS, g, _sigmoid(ab))

        _for_row_chunks(x_ref.shape[0], rows_step)

    z_ref[...] = _dot_nt(hn_ref[...], wt_ref[...].astype(bf16)).astype(z_ref.dtype)


def _inproj(x, gain, wt, alog_row, dt_row, l):
    m = x.shape[0]
    n_direct = ORIG_COL_AB // PROJ_TN

    def w_rows(i, j):
        return pl.multiple_of(l * D_IN + j * PROJ_TN + jnp.where(j >= n_direct, N_AB, 0), SUBLANES), 0

    return pl.pallas_call(
        _inproj_kernel,
        grid=(m // PROJ_TM, N_MAIN // PROJ_TN),
        in_specs=[
            pl.BlockSpec((PROJ_TM, D_MODEL), lambda i, j: (i, 0), pipeline_mode=pl.Buffered(1)),
            pl.BlockSpec((None, 1, D_MODEL), lambda i, j: (l, 0, 0)),
            pl.BlockSpec((pl.Element(PROJ_TN), pl.Element(D_MODEL)), w_rows),
            pl.BlockSpec((pl.Element(LANES), pl.Element(D_MODEL)), lambda i, j: (l * D_IN + ORIG_COL_AB, 0),
                         pipeline_mode=pl.Buffered(1)),
            pl.BlockSpec((None, 1, LANES), lambda i, j: (l, 0, 0)),
            pl.BlockSpec((None, 1, LANES), lambda i, j: (l, 0, 0)),
        ],
        out_specs=[
            pl.BlockSpec((PROJ_TM, PROJ_TN), lambda i, j: (i, j)),
            pl.BlockSpec((PROJ_TM, LANES), lambda i, j: (i, 0)),
        ],
        out_shape=[
            jax.ShapeDtypeStruct((m, N_MAIN), bf16),
            jax.ShapeDtypeStruct((m, LANES), f32),
        ],
        scratch_shapes=[pltpu.VMEM((PROJ_TM, D_MODEL), bf16)],
        compiler_params=_params("arbitrary", "arbitrary"),
        name="inproj",
    )(x, gain, wt, wt, alog_row, dt_row)


def _ffn_up_kernel(x_ref, g_ref, w_ref, h_ref, hn_ref):
    @pl.when(pl.program_id(1) == 0)
    def _():
        def rows_step(rows):
            hn_ref[rows, :] = _rms_rows(x_ref[rows, :], g_ref[...]).astype(bf16)

        _for_row_chunks(x_ref.shape[0], rows_step)

    a = jnp.maximum(_dot(hn_ref[...], w_ref[...].astype(bf16)), 0.0)
    h_ref[...] = (a * a).astype(h_ref.dtype)


def _ffn_up(x, gain, w1, l):
    m, n = x.shape[0], w1.shape[-1]
    return pl.pallas_call(
        _ffn_up_kernel,
        grid=(m // PROJ_TM, n // PROJ_TN),
        in_specs=[
            pl.BlockSpec((PROJ_TM, D_MODEL), lambda i, j: (i, 0), pipeline_mode=pl.Buffered(1)),
            pl.BlockSpec((None, 1, D_MODEL), lambda i, j: (l, 0, 0)),
            pl.BlockSpec((None, D_MODEL, PROJ_TN), lambda i, j: (l, 0, j)),
        ],
        out_specs=pl.BlockSpec((PROJ_TM, PROJ_TN), lambda i, j: (i, j)),
        out_shape=jax.ShapeDtypeStruct((m, n), bf16),
        scratch_shapes=[pltpu.VMEM((PROJ_TM, D_MODEL), bf16)],
        compiler_params=_params("arbitrary", "arbitrary"),
        name="ffn_up",
    )(x, gain, w1)


def _matmul_res_kernel(a_ref, w_ref, r_ref, o_ref):
    d = _dot(a_ref[...], w_ref[...].astype(bf16))

    @pl.when(pl.program_id(2) == 0)
    def _():
        o_ref[...] = r_ref[...] + d

    @pl.when(pl.program_id(2) != 0)
    def _():
        o_ref[...] += d


def _matmul_res(a, w, res, l):
    m, k = a.shape
    n = w.shape[-1]
    tm, tk = (RES_TM // 2, 2 * RES_TK) if k > D_MODEL else (RES_TM, RES_TK)
    return pl.pallas_call(
        _matmul_res_kernel,
        grid=(m // tm, n // RES_TN, k // tk),
        in_specs=[
            pl.BlockSpec((tm, tk), lambda i, j, kk: (i, kk)),
            pl.BlockSpec((None, tk, RES_TN), lambda i, j, kk: (l, kk, j)),
            pl.BlockSpec((tm, RES_TN), lambda i, j, kk: (i, j)),
        ],
        out_specs=pl.BlockSpec((tm, RES_TN), lambda i, j, kk: (i, j)),
        out_shape=jax.ShapeDtypeStruct((m, n), f32),
        compiler_params=_params("arbitrary", "arbitrary", "arbitrary"),
        name="matmul_res",
    )(a, w, res)


MERGE_GATE_COLS = 1024
MERGE_GATE_BLOCKS = D_MODEL // MERGE_GATE_COLS


def _merge_kernel(oa_ref, ob_ref, oc_ref, wa_ref, wb_ref, wc_ref, *rest):
    gate_refs, y_ref = rest[:-1], rest[-1]
    branches = ((oa_ref, wa_ref), (ob_ref, wb_ref), (oc_ref, wc_ref))
    for c in range(MERGE_GATE_BLOCKS):
        cols = slice(c * MERGE_GATE_COLS, (c + 1) * MERGE_GATE_COLS)
        y = None
        for b, (o_ref, w_ref) in enumerate(branches):
            gate = _sigmoid(gate_refs[b * MERGE_GATE_BLOCKS + c][...].astype(f32))
            term = gate * _dot(o_ref[...], w_ref[:, cols].astype(bf16))
            y = term if y is None else y + term
        y_ref[:, cols] = y.astype(y_ref.dtype)


def _merge(oa, ob, oc, wa, wb, wc, z, l):
    m = oa.shape[0]
    gate0 = COL_GATES // MERGE_GATE_COLS
    branch_in = pl.BlockSpec((MERGE_TM, WIDTH), lambda i: (i, 0))
    branch_w = pl.BlockSpec((None, WIDTH, D_MODEL), lambda i: (l, 0, 0), pipeline_mode=pl.Buffered(1))
    gate_specs = [pl.BlockSpec((MERGE_TM, MERGE_GATE_COLS), lambda i, blk=gate0 + g: (i, blk))
                  for g in range(N_BRANCHES * MERGE_GATE_BLOCKS)]
    return pl.pallas_call(
        _merge_kernel,
        grid=(m // MERGE_TM,),
        in_specs=[branch_in, branch_in, branch_in, branch_w, branch_w, branch_w] + gate_specs,
        out_specs=pl.BlockSpec((MERGE_TM, D_MODEL), lambda i: (i, 0)),
        out_shape=jax.ShapeDtypeStruct((m, D_MODEL), bf16),
        compiler_params=_params("arbitrary"),
        name="merge",
    )(oa, ob, oc, wa, wb, wc, *([z] * (N_BRANCHES * MERGE_GATE_BLOCKS)))


def _gelu(x):
    return 0.5 * x * (1.0 + lax.erf(x * np.float32(np.sqrt(0.5))))


def _gmlp_kernel(zu_ref, zv_ref, g_ref, ws_ref, bs_ref, o_ref):
    u = _gelu(zu_ref[...].astype(f32))
    v = _gelu(zv_ref[...].astype(f32))
    mu = jnp.mean(v, axis=-1, keepdims=True)
    vc = v - mu
    var = jnp.mean(vc * vc, axis=-1, keepdims=True)
    vb = (vc * lax.rsqrt(var + EPS) * g_ref[...]).astype(bf16)

    t_chunk = _block_id(lax.broadcasted_iota(jnp.int32, (GMLP_BLOCK, GMLP_BLOCK), 0), CHUNK)
    s_chunk = _block_id(lax.broadcasted_iota(jnp.int32, (GMLP_BLOCK, GMLP_BLOCK), 1), CHUNK)
    causal = s_chunk <= t_chunk
    gdim = GMLP_WIDTH // GMLP_GROUPS
    for g in range(GMLP_GROUPS):
        ws = jnp.where(causal, ws_ref[g], 0.0).astype(bf16)
        cols = slice(g * gdim, (g + 1) * gdim)
        for blk in range(GMLP_ROWS // GMLP_BLOCK):
            rows = slice(blk * GMLP_BLOCK, (blk + 1) * GMLP_BLOCK)
            s = _dot(ws, vb[rows, cols]) + bs_ref[g]
            o_ref[rows, cols] = (u[rows, cols] * s).astype(o_ref.dtype)


def _gmlp(z, ln_gain, w_spatial, b_rep, l):
    m = z.shape[0]
    cu = COL_GMLP_UV // GMLP_WIDTH
    return pl.pallas_call(
        _gmlp_kernel,
        grid=(m // GMLP_ROWS,),
        in_specs=[
            pl.BlockSpec((GMLP_ROWS, GMLP_WIDTH), lambda i: (i, cu)),
            pl.BlockSpec((GMLP_ROWS, GMLP_WIDTH), lambda i: (i, cu + 1)),
            pl.BlockSpec((None, 1, GMLP_WIDTH), lambda i: (l, 0, 0)),
            pl.BlockSpec((None, GMLP_GROUPS, GMLP_BLOCK, GMLP_BLOCK), lambda i: (l, 0, 0, 0)),
            pl.BlockSpec((None, GMLP_GROUPS, GMLP_BLOCK, LANES), lambda i: (l, 0, 0, 0)),
        ],
        out_specs=pl.BlockSpec((GMLP_ROWS, GMLP_WIDTH), lambda i: (i, 0)),
        out_shape=jax.ShapeDtypeStruct((m, GMLP_WIDTH), bf16),
        compiler_params=_params("arbitrary"),
        name="gmlp",
    )(z, z, ln_gain, w_spatial, b_rep)


def _sba_kernel(q_ref, k_ref, v_ref, qg_ref, kg_ref, o_ref, qn_ref, kn_ref):
    t = q_ref.shape[0]
    head_cols = [slice(j * HEAD_DIM, (j + 1) * HEAD_DIM) for j in range(HEADS_PER_STEP)]
    for hc in head_cols:
        qn_ref[:, hc] = _rms_rows(q_ref[:, hc].astype(f32), qg_ref[...]).astype(bf16)
        kn_ref[:, hc] = _rms_rows(k_ref[:, hc].astype(f32), kg_ref[...]).astype(bf16)
    to_log2 = np.float32(HEAD_DIM ** -0.5 * LOG2_E)
    rows_all = HEADS_PER_STEP * SBA_TQ
    wr = lax.broadcasted_iota(jnp.int32, (2 * SBA_SUB, 2 * SBA_SUB), 0)
    wc = lax.broadcasted_iota(jnp.int32, (2 * SBA_SUB, 2 * SBA_SUB), 1)
    suffix_w = ((jnp.bitwise_and(wr, SBA_SUB - 1) > wc) | (wc >= SBA_SUB)).astype(bf16)

    below_diag = (lax.broadcasted_iota(jnp.int32, (rows_all, SBA_SUB), 1)
                  < jnp.bitwise_and(lax.broadcasted_iota(jnp.int32, (rows_all, SBA_SUB), 0), SBA_TQ - 1))

    def block(qbs, k0, n_cols, diagonal, state):
        acc, run = state
        z = jnp.concatenate([_dot_nt(qbs[j], kn_ref[pl.ds(k0, n_cols), head_cols[j]])
                             for j in range(HEADS_PER_STEP)], axis=0) * to_log2
        neg_abs = pltpu.bitcast(pltpu.bitcast(z, jnp.uint32) | jnp.uint32(0x80000000), f32)
        soft = jnp.log(1.0 + jnp.exp2(neg_abs)) * np.float32(LOG2_E)
        drop = jnp.maximum(z, 0.0) + soft
        log_beta = z - drop
        probs = []
        offs = run
        last = n_cols // SBA_SUB - 1
        for c in range(last, -1, -1):
            cols = slice(c * SBA_SUB, (c + 1) * SBA_SUB)
            masked = diagonal and c == last
            dc = drop[:, cols]
            if masked:
                dc = jnp.where(below_diag, dc, 0.0)
            hi = dc.astype(bf16)
            lo = (dc - hi.astype(f32)).astype(bf16)
            sums = _dot(jnp.concatenate([hi, lo], axis=-1), suffix_w)
            a = jnp.exp2(log_beta[:, cols] - sums[:, :SBA_SUB] - offs)
            if masked:
                a = jnp.where(below_diag, a, 0.0)
            probs.append(a.astype(bf16))
            offs = offs + sums[:, SBA_SUB:]
        a_blk = probs[0] if last == 0 else jnp.concatenate(probs[::-1], axis=-1)
        pv = jnp.concatenate(
            [_dot(a_blk[j * SBA_TQ:(j + 1) * SBA_TQ], v_ref[pl.ds(k0, n_cols), head_cols[j]])
             for j in range(HEADS_PER_STEP)], axis=0)
        return acc + pv, offs

    def key_block_row(s, has_left):
        k_diag = pl.multiple_of(s * SBA_TK, SBA_TK)
        n_q = SBA_TK // SBA_TQ
        q0s = [pl.multiple_of(k_diag + p * SBA_TQ, SBA_TQ) for p in range(n_q)]
        qbss = [[qn_ref[pl.ds(q0, SBA_TQ), hc] for hc in head_cols] for q0 in q0s]
        zeros = jnp.zeros((rows_all, HEAD_DIM), f32)
        states = [block(qbss[p], k_diag, (p + 1) * SBA_SUB, True, (zeros, zeros)) for p in range(n_q)]
        taken = [0] * n_q
        if has_left:
            k_first = pl.multiple_of(k_diag - SBA_LEFT_TK, SBA_LEFT_TK)
            for p in range(SBA_EAGER_LEFT):
                states[p] = block(qbss[p], k_first, SBA_LEFT_TK, False, states[p])
                taken[p] = 1
        for p in range(n_q):
            q0, qbs, state = q0s[p], qbss[p], states[p]

            n_left = s * (SBA_TK // SBA_LEFT_TK)

            def more(c):
                it, _, run = c
                return (it < n_left) & (jnp.min(run) <= UNDERFLOW_LOG2)

            def left_block(c):
                it, acc, run = c
                k0 = pl.multiple_of((n_left - 1 - it) * SBA_LEFT_TK, SBA_LEFT_TK)
                acc, run = block(qbs, k0, SBA_LEFT_TK, False, (acc, run))
                return it + 1, acc, run

            _, acc, _ = lax.while_loop(more, left_block, (jnp.int32(taken[p]),) + state)
            for j in range(HEADS_PER_STEP):
                o_ref[pl.ds(q0, SBA_TQ), head_cols[j]] = acc[j * SBA_TQ:(j + 1) * SBA_TQ].astype(o_ref.dtype)

    key_block_row(jnp.int32(0), False)

    def later_row(s, carry):
        key_block_row(s, True)
        return carry

    lax.fori_loop(1, t // SBA_TK, later_row, 0)


def _sba(z, q_gain, k_gain, batch, seq, l):
    m = z.shape[0]
    c0 = COL_SBA_QKV // STEP_WIDTH
    per_seg = WIDTH // STEP_WIDTH

    def head_spec(seg):
        return pl.BlockSpec((seq, STEP_WIDTH), lambda b, h: (b, c0 + seg * per_seg + h))

    gain_spec = pl.BlockSpec((None, 1, HEAD_DIM), lambda b, h: (l, 0, 0))
    tok = pltpu.VMEM((seq, STEP_WIDTH), bf16)
    return pl.pallas_call(
        _sba_kernel,
        grid=(batch, per_seg),
        in_specs=[head_spec(0), head_spec(1), head_spec(2), gain_spec, gain_spec],
        out_specs=pl.BlockSpec((seq, STEP_WIDTH), lambda b, h: (b, h)),
        out_shape=jax.ShapeDtypeStruct((m, WIDTH), bf16),
        scratch_shapes=[tok, tok],
        compiler_params=_params("arbitrary", "arbitrary"),
        name="sba",
    )(z, z, z, q_gain, k_gain)


XPAD = BF16_ROWS


def _bmm(a, b):
    return jnp.einsum('bij,bjk->bik', a, b, preferred_element_type=f32)


def _bmm_nt(a, b):
    return jnp.einsum('bid,bjd->bij', a, b, preferred_element_type=f32)


def _bmm_tn(a, b):
    return jnp.einsum('bck,bcv->bkv', a, b, preferred_element_type=f32)


def _unit_lower_inverse(l_mat, row, col):
    def same_block(d):
        return _block_id(row, d) == _block_id(col, d)

    eye = (row == col).astype(f32)
    l8 = jnp.where(same_block(8), l_mat, 0.0)
    p = l8.astype(bf16)
    x = (eye - l8).astype(bf16)
    p2 = _bmm(p, p)
    x = _bmm(x, (eye + p2).astype(bf16)).astype(bf16)
    p2 = p2.astype(bf16)
    p4 = _bmm(p2, p2)
    x = _bmm(x, (eye + p4).astype(bf16)).astype(bf16)
    d = 8
    while d < CHUNK:
        off = jnp.where(same_block(2 * d) & jnp.logical_not(same_block(d)), l_mat, 0.0).astype(bf16)
        x = _bmm(x, (eye - _bmm(off, x)).astype(bf16)).astype(bf16)
        d *= 2
    return x


SCAN_ROWS = HEAD_DIM + CHUNK


def _gdn_kernel(zq_ref, zk_ref, zv_ref, zg_ref, gb_ref, cwq_ref, cwk_ref, cwv_ref, ng_ref, o_ref,
                xq_ref, xk_ref, xv_ref, stage_ref, lin_ref, off_ref, eg_ref):
    t = zq_ref.shape[0]
    head0 = pl.program_id(1) * HEADS_PER_STEP
    grp = GDN_GROUP
    nc = grp // CHUNK
    head_cols = [slice(j * HEAD_DIM, (j + 1) * HEAD_DIM) for j in range(HEADS_PER_STEP)]

    zeros_pad = jnp.zeros((XPAD, STEP_WIDTH), bf16)
    for src, dst in ((zq_ref, xq_ref), (zk_ref, xk_ref), (zv_ref, xv_ref)):
        dst[pl.ds(0, XPAD), :] = zeros_pad
        dst[pl.ds(XPAD, t), :] = src[...]

    row = lax.broadcasted_iota(jnp.int32, (CHUNK, CHUNK), 0)
    col = lax.broadcasted_iota(jnp.int32, (CHUNK, CHUNK), 1)
    incl = row >= col
    strict = row > col
    tri_ones = jnp.concatenate([incl.astype(bf16), jnp.ones((CHUNK, CHUNK), bf16)], axis=0)
    tri_ones = jnp.broadcast_to(tri_ones, (nc, 2 * CHUNK, CHUNK))
    sel_src = jnp.bitwise_and(lax.broadcasted_iota(jnp.int32, (3 * LANES, STEP_WIDTH), 0), LANES - 1)
    sel_head = head0 + _block_id(lax.broadcasted_iota(jnp.int32, (3 * LANES, STEP_WIDTH), 1), HEAD_DIM)
    pick_g = (sel_src == sel_head).astype(bf16)
    pick_b = (sel_src == sel_head + N_HEADS).astype(bf16)
    sel_rows = (jnp.bitwise_and(lax.broadcasted_iota(jnp.int32, (SUBLANES, 3 * LANES), 1), LANES - 1)
                == head0 + lax.broadcasted_iota(jnp.int32, (SUBLANES, 3 * LANES), 0)).astype(bf16)
    sel_rows = jnp.broadcast_to(sel_rows, (nc, SUBLANES, 3 * LANES))

    def conv_silu(x_ref, cw_ref, r0, hc, slot):
        stage = stage_ref.at[slot]
        stage[...] = x_ref[pl.ds(r0, grp + XPAD), hc].astype(f32)
        acc = cw_ref[pl.ds(CONV_WIDTH - 1, 1), hc] * stage[pl.ds(XPAD, grp), :]
        for i in range(CONV_WIDTH - 1):
            acc += cw_ref[pl.ds(i, 1), hc] * stage[pl.ds(XPAD - (CONV_WIDTH - 1) + i, grp), :]
        return _silu(acc)

    def l2n(x):
        return x * lax.rsqrt(jnp.sum(x * x, axis=-1, keepdims=True) + EPS)

    def chunked(x):
        return x.reshape(nc, CHUNK, x.shape[-1])

    def slot_of(g):
        return jnp.bitwise_and(g, 1)

    def prepare(it, carry):
        r0 = pl.multiple_of(it * grp, grp)
        gb3 = _split3(gb_ref[pl.ds(r0, grp), :])
        ct = _bmm(tri_ones, chunked(gb3))
        cum3 = _split3(_sum3(ct[:, :CHUNK], LANES).reshape(grp, LANES))
        tot3 = _split3(_sum3(ct[:, CHUNK:], LANES).reshape(grp, LANES))
        gc_rows = _bmm_nt(sel_rows, chunked(cum3))

        gc_all = _dot(cum3, pick_g)
        g_tot_all = _dot(tot3, pick_g)
        beta_all = _dot(gb3, pick_b)

        k16s, kb16s, q16s, rhs16s, diffs, qds, kd16s = [], [], [], [], [], [], []
        for j, hc in enumerate(head_cols):
            gc, g_tot, beta = gc_all[:, hc], g_tot_all[:, hc], beta_all[:, hc]
            q = l2n(conv_silu(xq_ref, cwq_ref, r0, hc, 3 * j)) * np.float32(HEAD_DIM ** -0.5)
            k = l2n(conv_silu(xk_ref, cwk_ref, r0, hc, 3 * j + 1))
            v = conv_silu(xv_ref, cwv_ref, r0, hc, 3 * j + 2)
            kb = k * beta
            e_gc = jnp.exp(gc)
            k16s.append(chunked(k.astype(bf16)))
            kb16s.append(chunked(kb.astype(bf16)))
            q16s.append(chunked(q.astype(bf16)))
            rhs16s.append(chunked(jnp.concatenate([v * beta, kb * e_gc], axis=-1).astype(bf16)))
            diffs.append(chunked(gc)[:, :, :CHUNK] - gc_rows[:, j:j + 1, :])
            qds.append(chunked(q * e_gc))
            kd16s.append(chunked((k * jnp.exp(g_tot - gc)).astype(bf16)))
            e0 = pl.multiple_of(slot_of(it) * (nc * SUBLANES), nc * SUBLANES)
            eg_ref[pl.ds(e0, nc * SUBLANES), hc] = (
                chunked(jnp.exp(g_tot))[:, :SUBLANES, :].reshape(nc * SUBLANES, HEAD_DIM))

        cat = lambda xs: jnp.concatenate(xs, axis=0)
        k16, diff = cat(k16s), cat(diffs)
        decay = jnp.where(incl, jnp.exp(jnp.where(incl, diff, 0.0)), 0.0)
        l_mat = jnp.where(strict, _bmm_nt(cat(kb16s), k16) * decay, 0.0)
        intra = (_bmm_nt(cat(q16s), k16) * decay).astype(bf16)
        t_inv = _unit_lower_inverse(l_mat, row, col)
        uw = _bmm(t_inv, cat(rhs16s)).astype(bf16)
        state_map = _bmm_tn(cat(kd16s), uw)
        out_map = _bmm(intra, uw)
        q_eff = cat(qds) - out_map[:, :, HEAD_DIM:]
        lin = jnp.concatenate([state_map[:, :, HEAD_DIM:], q_eff], axis=1).astype(bf16)
        off = jnp.concatenate([state_map[:, :, :HEAD_DIM], out_map[:, :, :HEAD_DIM]], axis=1)
        s0 = pl.multiple_of(slot_of(it) * (nc * SCAN_ROWS), nc * SCAN_ROWS)
        for j, hc in enumerate(head_cols):
            mine = slice(j * nc, (j + 1) * nc)
            lin_ref[pl.ds(s0, nc * SCAN_ROWS), hc] = lin[mine].reshape(nc * SCAN_ROWS, HEAD_DIM)
            off_ref[pl.ds(s0, nc * SCAN_ROWS), hc] = off[mine].reshape(nc * SCAN_ROWS, HEAD_DIM)
        return carry

    def scan(slot_chunk, out_chunk, states):
        s0 = pl.multiple_of(slot_chunk * SCAN_ROWS, SCAN_ROWS)
        e0 = pl.multiple_of(slot_chunk * SUBLANES, SUBLANES)
        r0 = pl.multiple_of(out_chunk * CHUNK, CHUNK)
        new_states = []
        for j, hc in enumerate(head_cols):
            state = states[j]
            prod = _dot(lin_ref[pl.ds(s0, SCAN_ROWS), hc], state.astype(bf16))
            off = off_ref[pl.ds(s0, SCAN_ROWS), hc]
            new_states.append(state * eg_ref[pl.ds(e0, 1), hc] + (off[:HEAD_DIM] - prod[:HEAD_DIM]))
            o = prod[HEAD_DIM:] + off[HEAD_DIM:]
            gate = zg_ref[pl.ds(r0, CHUNK), hc].astype(f32)
            o_ref[pl.ds(r0, CHUNK), hc] = (_rms_rows(o, ng_ref[...]) * _silu(gate)).astype(o_ref.dtype)
        return tuple(new_states)

    def scan_group(slot, out_group, states):
        for c in range(nc):
            states = scan(slot * nc + c, out_group * nc + c, states)
        return states

    def step(g, states):
        states = scan_group(slot_of(g + 1), jnp.maximum(g - 1, 0), states)
        prepare(g, 0)
        return states

    n_groups = t // grp
    lin_ref[pl.ds(nc * SCAN_ROWS, nc * SCAN_ROWS), :] = jnp.zeros((nc * SCAN_ROWS, STEP_WIDTH), bf16)
    off_ref[pl.ds(nc * SCAN_ROWS, nc * SCAN_ROWS), :] = jnp.zeros((nc * SCAN_ROWS, STEP_WIDTH), f32)
    eg_ref[pl.ds(nc * SUBLANES, nc * SUBLANES), :] = jnp.zeros((nc * SUBLANES, STEP_WIDTH), f32)
    zero_state = jnp.zeros((HEAD_DIM, HEAD_DIM), f32)
    states = lax.fori_loop(0, n_groups, step, (zero_state,) * HEADS_PER_STEP)
    scan_group(jnp.int32((n_groups - 1) % 2), jnp.int32(n_groups - 1), states)


def _gdn(z, gb, conv_w, norm_gain, batch, seq, l):
    m = z.shape[0]
    per_seg = WIDTH // STEP_WIDTH
    cq = COL_GDN_QKV // STEP_WIDTH
    cg = COL_GDN_GATE // STEP_WIDTH

    def head_spec(c):
        return pl.BlockSpec((seq, STEP_WIDTH), lambda b, h: (b, c + h))

    def conv_spec(seg):
        return pl.BlockSpec((None, CONV_WIDTH, STEP_WIDTH), lambda b, h: (l, 0, seg * per_seg + h))

    ring_chunks = 2 * GDN_GROUP // CHUNK
    padded = pltpu.VMEM((seq + XPAD, STEP_WIDTH), bf16)
    return pl.pallas_call(
        _gdn_kernel,
        grid=(batch, per_seg),
        in_specs=[
            head_spec(cq), head_spec(cq + per_seg), head_spec(cq + 2 * per_seg), head_spec(cg),
            pl.BlockSpec((seq, LANES), lambda b, h: (b, 0)),
            conv_spec(0), conv_spec(1), conv_spec(2),
            pl.BlockSpec((None, 1, HEAD_DIM), lambda b, h: (l, 0, 0)),
        ],
        out_specs=pl.BlockSpec((seq, STEP_WIDTH), lambda b, h: (b, h)),
        out_shape=jax.ShapeDtypeStruct((m, WIDTH), bf16),
        scratch_shapes=[
            padded, padded, padded,
            pltpu.VMEM((3 * HEADS_PER_STEP, GDN_GROUP + XPAD, HEAD_DIM), f32),
            pltpu.VMEM((ring_chunks * SCAN_ROWS, STEP_WIDTH), bf16),
            pltpu.VMEM((ring_chunks * SCAN_ROWS, STEP_WIDTH), f32),
            pltpu.VMEM((ring_chunks * SUBLANES, STEP_WIDTH), f32),
        ],
        compiler_params=_params("arbitrary", "arbitrary"),
        name="gdn",
    )(z, z, z, z, gb, conv_w, conv_w, conv_w, norm_gain)


def _rows(v):
    return v.astype(f32)[:, None, :]


def _pad_rows(v):
    return _rows(jnp.pad(v, ((0, 0), (0, LANES - v.shape[1]))))


def kernel(x, w_in, conv_w, a_log, dt_bias, gdn_norm_g, gmlp_ln_g, w_spatial, b_spatial, sba_q_g, sba_k_g, w_out_a, w_out_b, w_out_c, w_out, norm_mix_g, norm_mlp_g, w_ff1, w_ff2):
    batch, seq, d = x.shape
    depth = w_in.shape[0]
    m = batch * seq

    wt = jnp.swapaxes(w_in, 1, 2).reshape(depth * D_IN, d)
    b_rep = jnp.broadcast_to(b_spatial[..., None], b_spatial.shape + (LANES,)).astype(f32)
    mix_g, mlp_g, gdn_g, ln_g = _rows(norm_mix_g), _rows(norm_mlp_g), _rows(gdn_norm_g), _rows(gmlp_ln_g)
    q_g, k_g = _rows(sba_q_g), _rows(sba_k_g)
    alog_rows, dt_rows = _pad_rows(a_log), _pad_rows(dt_bias)
    conv_w = conv_w.astype(f32)
    w_spatial = w_spatial.astype(f32)

    xf = x.reshape(m, d).astype(f32)
    for l in range(depth):
        z, gb = _inproj(xf, mix_g, wt, alog_rows, dt_rows, l)
        oa = _gdn(z, gb, conv_w, gdn_g, batch, seq, l)
        ob = _gmlp(z, ln_g, w_spatial, b_rep, l)
        oc = _sba(z, q_g, k_g, batch, seq, l)
        y = _merge(oa, ob, oc, w_out_a, w_out_b, w_out_c, z, l)
        xf = _matmul_res(y, w_out, xf, l)
        h1 = _ffn_up(xf, mlp_g, w_ff1, l)
        xf = _matmul_res(h1, w_ff2, xf, l)
    return xf.reshape(batch, seq, d).astype(x.dtype)
```

```python
import jax
import jax.numpy as jnp
import numpy as np
from jax import lax
from jax.experimental import pallas as pl
from jax.experimental.pallas import tpu as pltpu

f32 = jnp.float32
bf16 = jnp.bfloat16

D_MODEL = 2048
CHUNK = 64
HEAD_DIM = 128
N_HEADS = 8
WIDTH = N_HEADS * HEAD_DIM
CONV_WIDTH = 4
GMLP_WIDTH = 1024
GMLP_GROUPS = 8
GMLP_BLOCK = 128
N_BRANCHES = 3
EPS = 1e-6
LOG2_E = float(np.log2(np.e))
UNDERFLOW_LOG2 = 160.0

LANES = 128
SUBLANES = 8
BF16_ROWS = 16
VMEM_LIMIT_BYTES = 60000 * 1024

COL_GDN_QKV = 0
COL_GDN_GATE = 3 * WIDTH
COL_GMLP_UV = COL_GDN_GATE + WIDTH
COL_SBA_QKV = COL_GMLP_UV + 2 * GMLP_WIDTH
COL_GATES = COL_SBA_QKV + 3 * WIDTH
N_MAIN = COL_GATES + N_BRANCHES * D_MODEL
ORIG_COL_AB = 3 * WIDTH
N_AB = 2 * N_HEADS
D_IN = N_MAIN + N_AB

PROJ_TM = 2048
PROJ_TN = 1024
RES_TM = 2048
RES_TN = 1024
RES_TK = 1024
MERGE_TM = 512

HEADS_PER_STEP = 4
STEP_WIDTH = HEADS_PER_STEP * HEAD_DIM
GDN_GROUP = 512
SBA_TQ = 128
SBA_TK = 512
SBA_LEFT_TK = 256
SBA_EAGER_LEFT = 2
SBA_SUB = 128
GMLP_ROWS = 512


def _params(*sem):
    return pltpu.CompilerParams(dimension_semantics=sem, vmem_limit_bytes=VMEM_LIMIT_BYTES)


def _dot(a, b):
    return jnp.dot(a, b, preferred_element_type=f32)


def _dot_nt(a, b):
    return lax.dot_general(a, b, (((1,), (1,)), ((), ())), preferred_element_type=f32)


def _sigmoid(x):
    return lax.logistic(x)


def _silu(x):
    return x * _sigmoid(x)


def _softplus(x):
    return jnp.maximum(x, 0.0) + jnp.log1p(jnp.exp(-jnp.abs(x)))


def _block_id(idx, size):
    return lax.shift_right_logical(idx, int(size).bit_length() - 1)


def _split3(x):
    hi = x.astype(bf16)
    r1 = x - hi.astype(f32)
    mid = r1.astype(bf16)
    lo = (r1 - mid.astype(f32)).astype(bf16)
    return jnp.concatenate([hi, mid, lo], axis=-1)


def _sum3(x, n):
    return x[..., :n] + x[..., n:2 * n] + x[..., 2 * n:3 * n]


def _rms_rows(x, gain):
    ms = jnp.mean(x * x, axis=-1, keepdims=True)
    return x * lax.rsqrt(ms + EPS) * gain


NORM_ROWS = 256


def _norm_scratch():
    return [pltpu.VMEM((PROJ_TM, D_MODEL), bf16),
            pltpu.VMEM((2, NORM_ROWS, D_MODEL), f32),
            pltpu.SemaphoreType.DMA((2,))]


def _stream_row_chunks(x_hbm, row0, n_rows, buf, sem, body):
    n = n_rows // NORM_ROWS

    def copy(c, slot):
        src = x_hbm.at[pl.ds(pl.multiple_of(row0 + c * NORM_ROWS, NORM_ROWS), NORM_ROWS), :]
        return pltpu.make_async_copy(src, buf.at[slot], sem.at[slot])

    copy(0, 0).start()

    def step(c, carry):
        slot = jnp.bitwise_and(c, 1)

        @pl.when(c + 1 < n)
        def _():
            copy(c + 1, 1 - slot).start()

        copy(c, slot).wait()
        body(pl.ds(pl.multiple_of(c * NORM_ROWS, NORM_ROWS), NORM_ROWS), buf[slot])
        return carry

    lax.fori_loop(0, n, step, 0)


def _inproj_kernel(x_hbm, g_ref, wt_ref, wabt_ref, alog_ref, dt_ref, z_ref, gb_ref, hn_ref, xbuf, xsem):
    @pl.when(pl.program_id(1) == 0)
    def _():
        wab = wabt_ref[...].astype(bf16)

        def rows_step(rows, x):
            hn = _rms_rows(x, g_ref[...]).astype(bf16)
            hn_ref[rows, :] = hn
            ab = _dot_nt(hn, wab)
            lane = lax.broadcasted_iota(jnp.int32, ab.shape, 1)
            g = -jnp.exp(alog_ref[...]) * _softplus(ab + dt_ref[...])
            gb_ref[rows, :] = jnp.where(lane < N_HEADS, g, _sigmoid(ab))

        _stream_row_chunks(x_hbm, pl.program_id(0) * PROJ_TM, PROJ_TM, xbuf, xsem, rows_step)

    z_ref[...] = _dot_nt(hn_ref[...], wt_ref[...].astype(bf16)).astype(z_ref.dtype)


def _inproj(x, gain, wt, alog_row, dt_row, l):
    m = x.shape[0]
    n_direct = ORIG_COL_AB // PROJ_TN

    def w_rows(i, j):
        return pl.multiple_of(l * D_IN + j * PROJ_TN + jnp.where(j >= n_direct, N_AB, 0), SUBLANES), 0

    return pl.pallas_call(
        _inproj_kernel,
        grid=(m // PROJ_TM, N_MAIN // PROJ_TN),
        in_specs=[
            pl.BlockSpec(memory_space=pl.ANY),
            pl.BlockSpec((None, 1, D_MODEL), lambda i, j: (l, 0, 0)),
            pl.BlockSpec((pl.Element(PROJ_TN), pl.Element(D_MODEL)), w_rows),
            pl.BlockSpec((pl.Element(LANES), pl.Element(D_MODEL)), lambda i, j: (l * D_IN + ORIG_COL_AB, 0),
                         pipeline_mode=pl.Buffered(1)),
            pl.BlockSpec((None, 1, LANES), lambda i, j: (l, 0, 0)),
            pl.BlockSpec((None, 1, LANES), lambda i, j: (l, 0, 0)),
        ],
        out_specs=[
            pl.BlockSpec((PROJ_TM, PROJ_TN), lambda i, j: (i, j)),
            pl.BlockSpec((PROJ_TM, LANES), lambda i, j: (i, 0)),
        ],
        out_shape=[
            jax.ShapeDtypeStruct((m, N_MAIN), bf16),
            jax.ShapeDtypeStruct((m, LANES), f32),
        ],
        scratch_shapes=_norm_scratch(),
        compiler_params=_params("arbitrary", "arbitrary"),
        name="inproj",
    )(x, gain, wt, wt, alog_row, dt_row)


def _ffn_up_kernel(x_hbm, g_ref, w_ref, h_ref, hn_ref, xbuf, xsem):
    @pl.when(pl.program_id(1) == 0)
    def _():
        def rows_step(rows, x):
            hn_ref[rows, :] = _rms_rows(x, g_ref[...]).astype(bf16)

        _stream_row_chunks(x_hbm, pl.program_id(0) * PROJ_TM, PROJ_TM, xbuf, xsem, rows_step)

    a = jnp.maximum(_dot(hn_ref[...], w_ref[...].astype(bf16)), 0.0)
    h_ref[...] = (a * a).astype(h_ref.dtype)


def _ffn_up(x, gain, w1, l):
    m, n = x.shape[0], w1.shape[-1]
    return pl.pallas_call(
        _ffn_up_kernel,
        grid=(m // PROJ_TM, n // PROJ_TN),
        in_specs=[
            pl.BlockSpec(memory_space=pl.ANY),
            pl.BlockSpec((None, 1, D_MODEL), lambda i, j: (l, 0, 0)),
            pl.BlockSpec((None, D_MODEL, PROJ_TN), lambda i, j: (l, 0, j)),
        ],
        out_specs=pl.BlockSpec((PROJ_TM, PROJ_TN), lambda i, j: (i, j)),
        out_shape=jax.ShapeDtypeStruct((m, n), bf16),
        scratch_shapes=_norm_scratch(),
        compiler_params=_params("arbitrary", "arbitrary"),
        name="ffn_up",
    )(x, gain, w1)


def _matmul_res_kernel(a_ref, w_ref, r_ref, o_ref):
    d = _dot(a_ref[...], w_ref[...].astype(bf16))

    @pl.when(pl.program_id(2) == 0)
    def _():
        o_ref[...] = r_ref[...] + d

    @pl.when(pl.program_id(2) != 0)
    def _():
        o_ref[...] += d


def _matmul_res(a, w, res, l):
    m, k = a.shape
    n = w.shape[-1]
    tm, tk = (RES_TM // 2, 2 * RES_TK) if k > D_MODEL else (RES_TM, RES_TK)
    return pl.pallas_call(
        _matmul_res_kernel,
        grid=(m // tm, n // RES_TN, k // tk),
        in_specs=[
            pl.BlockSpec((tm, tk), lambda i, j, kk: (i, kk)),
            pl.BlockSpec((None, tk, RES_TN), lambda i, j, kk: (l, kk, j)),
            pl.BlockSpec((tm, RES_TN), lambda i, j, kk: (i, j)),
        ],
        out_specs=pl.BlockSpec((tm, RES_TN), lambda i, j, kk: (i, j)),
        out_shape=jax.ShapeDtypeStruct((m, n), f32),
        compiler_params=_params("arbitrary", "arbitrary", "arbitrary"),
        name="matmul_res",
    )(a, w, res)


MERGE_GATE_COLS = 1024
MERGE_GATE_BLOCKS = D_MODEL // MERGE_GATE_COLS


def _merge_kernel(oa_ref, ob_ref, oc_ref, wa_ref, wb_ref, wc_ref, *rest):
    gate_refs, y_ref = rest[:-1], rest[-1]
    branches = ((oa_ref, wa_ref), (ob_ref, wb_ref), (oc_ref, wc_ref))
    for c in range(MERGE_GATE_BLOCKS):
        cols = slice(c * MERGE_GATE_COLS, (c + 1) * MERGE_GATE_COLS)
        y = None
        for b, (o_ref, w_ref) in enumerate(branches):
            gate = _sigmoid(gate_refs[b * MERGE_GATE_BLOCKS + c][...].astype(f32))
            term = gate * _dot(o_ref[...], w_ref[:, cols].astype(bf16))
            y = term if y is None else y + term
        y_ref[:, cols] = y.astype(y_ref.dtype)


def _merge(oa, ob, oc, wa, wb, wc, z, l):
    m = oa.shape[0]
    gate0 = COL_GATES // MERGE_GATE_COLS
    branch_in = pl.BlockSpec((MERGE_TM, WIDTH), lambda i: (i, 0))
    branch_w = pl.BlockSpec((None, WIDTH, D_MODEL), lambda i: (l, 0, 0), pipeline_mode=pl.Buffered(1))
    gate_specs = [pl.BlockSpec((MERGE_TM, MERGE_GATE_COLS), lambda i, blk=gate0 + g: (i, blk))
                  for g in range(N_BRANCHES * MERGE_GATE_BLOCKS)]
    return pl.pallas_call(
        _merge_kernel,
        grid=(m // MERGE_TM,),
        in_specs=[branch_in, branch_in, branch_in, branch_w, branch_w, branch_w] + gate_specs,
        out_specs=pl.BlockSpec((MERGE_TM, D_MODEL), lambda i: (i, 0)),
        out_shape=jax.ShapeDtypeStruct((m, D_MODEL), bf16),
        compiler_params=_params("arbitrary"),
        name="merge",
    )(oa, ob, oc, wa, wb, wc, *([z] * (N_BRANCHES * MERGE_GATE_BLOCKS)))


def _gelu(x):
    return 0.5 * x * (1.0 + lax.erf(x * np.float32(np.sqrt(0.5))))


def _gmlp_kernel(zu_ref, zv_ref, g_ref, ws_ref, bs_ref, o_ref):
    u = _gelu(zu_ref[...].astype(f32))
    v = _gelu(zv_ref[...].astype(f32))
    mu = jnp.mean(v, axis=-1, keepdims=True)
    vc = v - mu
    var = jnp.mean(vc * vc, axis=-1, keepdims=True)
    vb = (vc * lax.rsqrt(var + EPS) * g_ref[...]).astype(bf16)

    t_chunk = _block_id(lax.broadcasted_iota(jnp.int32, (GMLP_BLOCK, GMLP_BLOCK), 0), CHUNK)
    s_chunk = _block_id(lax.broadcasted_iota(jnp.int32, (GMLP_BLOCK, GMLP_BLOCK), 1), CHUNK)
    causal = s_chunk <= t_chunk
    gdim = GMLP_WIDTH // GMLP_GROUPS
    for g in range(GMLP_GROUPS):
        ws = jnp.where(causal, ws_ref[g], 0.0).astype(bf16)
        cols = slice(g * gdim, (g + 1) * gdim)
        for blk in range(GMLP_ROWS // GMLP_BLOCK):
            rows = slice(blk * GMLP_BLOCK, (blk + 1) * GMLP_BLOCK)
            s = _dot(ws, vb[rows, cols]) + bs_ref[g]
            o_ref[rows, cols] = (u[rows, cols] * s).astype(o_ref.dtype)


def _gmlp(z, ln_gain, w_spatial, b_rep, l):
    m = z.shape[0]
    cu = COL_GMLP_UV // GMLP_WIDTH
    return pl.pallas_call(
        _gmlp_kernel,
        grid=(m // GMLP_ROWS,),
        in_specs=[
            pl.BlockSpec((GMLP_ROWS, GMLP_WIDTH), lambda i: (i, cu)),
            pl.BlockSpec((GMLP_ROWS, GMLP_WIDTH), lambda i: (i, cu + 1)),
            pl.BlockSpec((None, 1, GMLP_WIDTH), lambda i: (l, 0, 0)),
            pl.BlockSpec((None, GMLP_GROUPS, GMLP_BLOCK, GMLP_BLOCK), lambda i: (l, 0, 0, 0)),
            pl.BlockSpec((None, GMLP_GROUPS, GMLP_BLOCK, LANES), lambda i: (l, 0, 0, 0)),
        ],
        out_specs=pl.BlockSpec((GMLP_ROWS, GMLP_WIDTH), lambda i: (i, 0)),
        out_shape=jax.ShapeDtypeStruct((m, GMLP_WIDTH), bf16),
        compiler_params=_params("arbitrary"),
        name="gmlp",
    )(z, z, ln_gain, w_spatial, b_rep)


def _sba_kernel(q_ref, k_ref, v_ref, qg_ref, kg_ref, o_ref, qn_ref, kn_ref):
    t = q_ref.shape[0]
    head_cols = [slice(j * HEAD_DIM, (j + 1) * HEAD_DIM) for j in range(HEADS_PER_STEP)]
    for hc in head_cols:
        qn_ref[:, hc] = _rms_rows(q_ref[:, hc].astype(f32), qg_ref[...]).astype(bf16)
        kn_ref[:, hc] = _rms_rows(k_ref[:, hc].astype(f32), kg_ref[...]).astype(bf16)
    to_log2 = np.float32(HEAD_DIM ** -0.5 * LOG2_E)
    rows_all = HEADS_PER_STEP * SBA_TQ
    wr = lax.broadcasted_iota(jnp.int32, (2 * SBA_SUB, 2 * SBA_SUB), 0)
    wc = lax.broadcasted_iota(jnp.int32, (2 * SBA_SUB, 2 * SBA_SUB), 1)
    suffix_w = ((jnp.bitwise_and(wr, SBA_SUB - 1) > wc) | (wc >= SBA_SUB)).astype(bf16)

    below_diag = (lax.broadcasted_iota(jnp.int32, (rows_all, SBA_SUB), 1)
                  < jnp.bitwise_and(lax.broadcasted_iota(jnp.int32, (rows_all, SBA_SUB), 0), SBA_TQ - 1))

    def block(qbs, k0, n_cols, diagonal, state):
        acc, run = state
        z = jnp.concatenate([_dot_nt(qbs[j], kn_ref[pl.ds(k0, n_cols), head_cols[j]])
                             for j in range(HEADS_PER_STEP)], axis=0) * to_log2
        neg_abs = pltpu.bitcast(pltpu.bitcast(z, jnp.uint32) | jnp.uint32(0x80000000), f32)
        soft = jnp.log(1.0 + jnp.exp2(neg_abs)) * np.float32(LOG2_E)
        drop = jnp.maximum(z, 0.0) + soft
        log_beta = z - drop
        probs = []
        offs = run
        last = n_cols // SBA_SUB - 1
        for c in range(last, -1, -1):
            cols = slice(c * SBA_SUB, (c + 1) * SBA_SUB)
            masked = diagonal and c == last
            dc = drop[:, cols]
            if masked:
                dc = jnp.where(below_diag, dc, 0.0)
            hi = dc.astype(bf16)
            lo = (dc - hi.astype(f32)).astype(bf16)
            sums = _dot(jnp.concatenate([hi, lo], axis=-1), suffix_w)
            a = jnp.exp2(log_beta[:, cols] - sums[:, :SBA_SUB] - offs)
            if masked:
                a = jnp.where(below_diag, a, 0.0)
            probs.append(a.astype(bf16))
            offs = offs + sums[:, SBA_SUB:]
        a_blk = probs[0] if last == 0 else jnp.concatenate(probs[::-1], axis=-1)
        pv = jnp.concatenate(
            [_dot(a_blk[j * SBA_TQ:(j + 1) * SBA_TQ], v_ref[pl.ds(k0, n_cols), head_cols[j]])
             for j in range(HEADS_PER_STEP)], axis=0)
        return acc + pv, offs

    def key_block_row(s, has_left):
        k_diag = pl.multiple_of(s * SBA_TK, SBA_TK)
        n_q = SBA_TK // SBA_TQ
        q0s = [pl.multiple_of(k_diag + p * SBA_TQ, SBA_TQ) for p in range(n_q)]
        qbss = [[qn_ref[pl.ds(q0, SBA_TQ), hc] for hc in head_cols] for q0 in q0s]
        zeros = jnp.zeros((rows_all, HEAD_DIM), f32)
        states = [block(qbss[p], k_diag, (p + 1) * SBA_SUB, True, (zeros, zeros)) for p in range(n_q)]
        taken = [0] * n_q
        if has_left:
            k_first = pl.multiple_of(k_diag - SBA_LEFT_TK, SBA_LEFT_TK)
            for p in range(SBA_EAGER_LEFT):
                states[p] = block(qbss[p], k_first, SBA_LEFT_TK, False, states[p])
                taken[p] = 1
        for p in range(n_q):
            q0, qbs, state = q0s[p], qbss[p], states[p]

            n_left = s * (SBA_TK // SBA_LEFT_TK)

            def more(c):
                it, _, run = c
                return (it < n_left) & (jnp.min(run) <= UNDERFLOW_LOG2)

            def left_block(c):
                it, acc, run = c
                k0 = pl.multiple_of((n_left - 1 - it) * SBA_LEFT_TK, SBA_LEFT_TK)
                acc, run = block(qbs, k0, SBA_LEFT_TK, False, (acc, run))
                return it + 1, acc, run

            _, acc, _ = lax.while_loop(more, left_block, (jnp.int32(taken[p]),) + state)
            for j in range(HEADS_PER_STEP):
                o_ref[pl.ds(q0, SBA_TQ), head_cols[j]] = acc[j * SBA_TQ:(j + 1) * SBA_TQ].astype(o_ref.dtype)

    key_block_row(jnp.int32(0), False)

    def later_row(s, carry):
        key_block_row(s, True)
        return carry

    lax.fori_loop(1, t // SBA_TK, later_row, 0)


def _sba(z, q_gain, k_gain, batch, seq, l):
    m = z.shape[0]
    c0 = COL_SBA_QKV // STEP_WIDTH
    per_seg = WIDTH // STEP_WIDTH

    def head_spec(seg):
        return pl.BlockSpec((seq, STEP_WIDTH), lambda b, h: (b, c0 + seg * per_seg + h))

    gain_spec = pl.BlockSpec((None, 1, HEAD_DIM), lambda b, h: (l, 0, 0))
    tok = pltpu.VMEM((seq, STEP_WIDTH), bf16)
    return pl.pallas_call(
        _sba_kernel,
        grid=(batch, per_seg),
        in_specs=[head_spec(0), head_spec(1), head_spec(2), gain_spec, gain_spec],
        out_specs=pl.BlockSpec((seq, STEP_WIDTH), lambda b, h: (b, h)),
        out_shape=jax.ShapeDtypeStruct((m, WIDTH), bf16),
        scratch_shapes=[tok, tok],
        compiler_params=_params("arbitrary", "arbitrary"),
        name="sba",
    )(z, z, z, q_gain, k_gain)


XPAD = BF16_ROWS


def _bmm(a, b):
    return jnp.einsum('bij,bjk->bik', a, b, preferred_element_type=f32)


def _bmm_nt(a, b):
    return jnp.einsum('bid,bjd->bij', a, b, preferred_element_type=f32)


def _bmm_tn(a, b):
    return jnp.einsum('bck,bcv->bkv', a, b, preferred_element_type=f32)


def _unit_lower_inverse(l_mat, row, col):
    def same_block(d):
        return _block_id(row, d) == _block_id(col, d)

    eye = (row == col).astype(f32)
    l8 = jnp.where(same_block(8), l_mat, 0.0)
    p = l8.astype(bf16)
    x = (eye - l8).astype(bf16)
    p2 = _bmm(p, p)
    x = _bmm(x, (eye + p2).astype(bf16)).astype(bf16)
    p2 = p2.astype(bf16)
    p4 = _bmm(p2, p2)
    x = _bmm(x, (eye + p4).astype(bf16)).astype(bf16)
    d = 8
    while d < CHUNK:
        off = jnp.where(same_block(2 * d) & jnp.logical_not(same_block(d)), l_mat, 0.0).astype(bf16)
        x = _bmm(x, (eye - _bmm(off, x)).astype(bf16)).astype(bf16)
        d *= 2
    return x


SCAN_ROWS = HEAD_DIM + CHUNK


def _gdn_kernel(zq_ref, zk_ref, zv_ref, zg_ref, gb_ref, cwq_ref, cwk_ref, cwv_ref, ng_ref, o_ref,
                xq_ref, xk_ref, xv_ref, stage_ref, lin_ref, off_ref, eg_ref):
    t = zq_ref.shape[0]
    head0 = pl.program_id(1) * HEADS_PER_STEP
    grp = GDN_GROUP
    nc = grp // CHUNK
    head_cols = [slice(j * HEAD_DIM, (j + 1) * HEAD_DIM) for j in range(HEADS_PER_STEP)]

    zeros_pad = jnp.zeros((XPAD, STEP_WIDTH), bf16)
    for src, dst in ((zq_ref, xq_ref), (zk_ref, xk_ref), (zv_ref, xv_ref)):
        dst[pl.ds(0, XPAD), :] = zeros_pad
        dst[pl.ds(XPAD, t), :] = src[...]

    row = lax.broadcasted_iota(jnp.int32, (CHUNK, CHUNK), 0)
    col = lax.broadcasted_iota(jnp.int32, (CHUNK, CHUNK), 1)
    incl = row >= col
    strict = row > col
    tri_ones = jnp.concatenate([incl.astype(bf16), jnp.ones((CHUNK, CHUNK), bf16)], axis=0)
    tri_ones = jnp.broadcast_to(tri_ones, (nc, 2 * CHUNK, CHUNK))
    sel_src = jnp.bitwise_and(lax.broadcasted_iota(jnp.int32, (3 * LANES, STEP_WIDTH), 0), LANES - 1)
    sel_head = head0 + _block_id(lax.broadcasted_iota(jnp.int32, (3 * LANES, STEP_WIDTH), 1), HEAD_DIM)
    pick_g = (sel_src == sel_head).astype(bf16)
    pick_b = (sel_src == sel_head + N_HEADS).astype(bf16)
    sel_rows = (jnp.bitwise_and(lax.broadcasted_iota(jnp.int32, (SUBLANES, 3 * LANES), 1), LANES - 1)
                == head0 + lax.broadcasted_iota(jnp.int32, (SUBLANES, 3 * LANES), 0)).astype(bf16)
    sel_rows = jnp.broadcast_to(sel_rows, (nc, SUBLANES, 3 * LANES))

    def conv_silu(x_ref, cw_ref, r0, hc, slot):
        stage = stage_ref.at[slot]
        stage[...] = x_ref[pl.ds(r0, grp + XPAD), hc].astype(f32)
        acc = cw_ref[pl.ds(CONV_WIDTH - 1, 1), hc] * stage[pl.ds(XPAD, grp), :]
        for i in range(CONV_WIDTH - 1):
            acc += cw_ref[pl.ds(i, 1), hc] * stage[pl.ds(XPAD - (CONV_WIDTH - 1) + i, grp), :]
        return _silu(acc)

    def l2n(x):
        return x * lax.rsqrt(jnp.sum(x * x, axis=-1, keepdims=True) + EPS)

    def chunked(x):
        return x.reshape(nc, CHUNK, x.shape[-1])

    def slot_of(g):
        return jnp.bitwise_and(g, 1)

    def prepare(it, carry):
        r0 = pl.multiple_of(it * grp, grp)
        gb3 = _split3(gb_ref[pl.ds(r0, grp), :])
        ct = _bmm(tri_ones, chunked(gb3))
        cum3 = _split3(_sum3(ct[:, :CHUNK], LANES).reshape(grp, LANES))
        tot3 = _split3(_sum3(ct[:, CHUNK:], LANES).reshape(grp, LANES))
        gc_rows = _bmm_nt(sel_rows, chunked(cum3))

        gc_all = _dot(cum3, pick_g)
        g_tot_all = _dot(tot3, pick_g)
        beta_all = _dot(gb3, pick_b)

        k16s, kb16s, q16s, rhs16s, diffs, qds, kd16s = [], [], [], [], [], [], []
        for j, hc in enumerate(head_cols):
            gc, g_tot, beta = gc_all[:, hc], g_tot_all[:, hc], beta_all[:, hc]
            q = l2n(conv_silu(xq_ref, cwq_ref, r0, hc, 3 * j)) * np.float32(HEAD_DIM ** -0.5)
            k = l2n(conv_silu(xk_ref, cwk_ref, r0, hc, 3 * j + 1))
            v = conv_silu(xv_ref, cwv_ref, r0, hc, 3 * j + 2)
            kb = k * beta
            e_gc = jnp.exp(gc)
            k16s.append(chunked(k.astype(bf16)))
            kb16s.append(chunked(kb.astype(bf16)))
            q16s.append(chunked(q.astype(bf16)))
            rhs16s.append(chunked(jnp.concatenate([v * beta, kb * e_gc], axis=-1).astype(bf16)))
            diffs.append(chunked(gc)[:, :, :CHUNK] - gc_rows[:, j:j + 1, :])
            qds.append(chunked(q * e_gc))
            kd16s.append(chunked((k * jnp.exp(g_tot - gc)).astype(bf16)))
            e0 = pl.multiple_of(slot_of(it) * (nc * SUBLANES), nc * SUBLANES)
            eg_ref[pl.ds(e0, nc * SUBLANES), hc] = (
                chunked(jnp.exp(g_tot))[:, :SUBLANES, :].reshape(nc * SUBLANES, HEAD_DIM))

        cat = lambda xs: jnp.concatenate(xs, axis=0)
        k16, diff = cat(k16s), cat(diffs)
        decay = jnp.where(incl, jnp.exp(jnp.where(incl, diff, 0.0)), 0.0)
        l_mat = jnp.where(strict, _bmm_nt(cat(kb16s), k16) * decay, 0.0)
        intra = (_bmm_nt(cat(q16s), k16) * decay).astype(bf16)
        t_inv = _unit_lower_inverse(l_mat, row, col)
        uw = _bmm(t_inv, cat(rhs16s)).astype(bf16)
        state_map = _bmm_tn(cat(kd16s), uw)
        out_map = _bmm(intra, uw)
        q_eff = cat(qds) - out_map[:, :, HEAD_DIM:]
        lin = jnp.concatenate([state_map[:, :, HEAD_DIM:], q_eff], axis=1).astype(bf16)
        off = jnp.concatenate([state_map[:, :, :HEAD_DIM], out_map[:, :, :HEAD_DIM]], axis=1)
        s0 = pl.multiple_of(slot_of(it) * (nc * SCAN_ROWS), nc * SCAN_ROWS)
        for j, hc in enumerate(head_cols):
            mine = slice(j * nc, (j + 1) * nc)
            lin_ref[pl.ds(s0, nc * SCAN_ROWS), hc] = lin[mine].reshape(nc * SCAN_ROWS, HEAD_DIM)
            off_ref[pl.ds(s0, nc * SCAN_ROWS), hc] = off[mine].reshape(nc * SCAN_ROWS, HEAD_DIM)
        return carry

    def scan(slot_chunk, out_chunk, states):
        s0 = pl.multiple_of(slot_chunk * SCAN_ROWS, SCAN_ROWS)
        e0 = pl.multiple_of(slot_chunk * SUBLANES, SUBLANES)
        r0 = pl.multiple_of(out_chunk * CHUNK, CHUNK)
        new_states = []
        for j, hc in enumerate(head_cols):
            state = states[j]
            prod = _dot(lin_ref[pl.ds(s0, SCAN_ROWS), hc], state.astype(bf16))
            off = off_ref[pl.ds(s0, SCAN_ROWS), hc]
            new_states.append(state * eg_ref[pl.ds(e0, 1), hc] + (off[:HEAD_DIM] - prod[:HEAD_DIM]))
            o = prod[HEAD_DIM:] + off[HEAD_DIM:]
            gate = zg_ref[pl.ds(r0, CHUNK), hc].astype(f32)
            o_ref[pl.ds(r0, CHUNK), hc] = (_rms_rows(o, ng_ref[...]) * _silu(gate)).astype(o_ref.dtype)
        return tuple(new_states)

    def scan_group(slot, out_group, states):
        for c in range(nc):
            states = scan(slot * nc + c, out_group * nc + c, states)
        return states

    def step(g, states):
        states = scan_group(slot_of(g + 1), jnp.maximum(g - 1, 0), states)
        prepare(g, 0)
        return states

    n_groups = t // grp
    lin_ref[pl.ds(nc * SCAN_ROWS, nc * SCAN_ROWS), :] = jnp.zeros((nc * SCAN_ROWS, STEP_WIDTH), bf16)
    off_ref[pl.ds(nc * SCAN_ROWS, nc * SCAN_ROWS), :] = jnp.zeros((nc * SCAN_ROWS, STEP_WIDTH), f32)
    eg_ref[pl.ds(nc * SUBLANES, nc * SUBLANES), :] = jnp.zeros((nc * SUBLANES, STEP_WIDTH), f32)
    zero_state = jnp.zeros((HEAD_DIM, HEAD_DIM), f32)
    states = lax.fori_loop(0, n_groups, step, (zero_state,) * HEADS_PER_STEP)
    scan_group(jnp.int32((n_groups - 1) % 2), jnp.int32(n_groups - 1), states)


def _gdn(z, gb, conv_w, norm_gain, batch, seq, l):
    m = z.shape[0]
    per_seg = WIDTH // STEP_WIDTH
    cq = COL_GDN_QKV // STEP_WIDTH
    cg = COL_GDN_GATE // STEP_WIDTH

    def head_spec(c):
        return pl.BlockSpec((seq, STEP_WIDTH), lambda b, h: (b, c + h))

    def conv_spec(seg):
        return pl.BlockSpec((None, CONV_WIDTH, STEP_WIDTH), lambda b, h: (l, 0, seg * per_seg + h))

    ring_chunks = 2 * GDN_GROUP // CHUNK
    padded = pltpu.VMEM((seq + XPAD, STEP_WIDTH), bf16)
    return pl.pallas_call(
        _gdn_kernel,
        grid=(batch, per_seg),
        in_specs=[
            head_spec(cq), head_spec(cq + per_seg), head_spec(cq + 2 * per_seg), head_spec(cg),
            pl.BlockSpec((seq, LANES), lambda b, h: (b, 0)),
            conv_spec(0), conv_spec(1), conv_spec(2),
            pl.BlockSpec((None, 1, HEAD_DIM), lambda b, h: (l, 0, 0)),
        ],
        out_specs=pl.BlockSpec((seq, STEP_WIDTH), lambda b, h: (b, h)),
        out_shape=jax.ShapeDtypeStruct((m, WIDTH), bf16),
        scratch_shapes=[
            padded, padded, padded,
            pltpu.VMEM((3 * HEADS_PER_STEP, GDN_GROUP + XPAD, HEAD_DIM), f32),
            pltpu.VMEM((ring_chunks * SCAN_ROWS, STEP_WIDTH), bf16),
            pltpu.VMEM((ring_chunks * SCAN_ROWS, STEP_WIDTH), f32),
            pltpu.VMEM((ring_chunks * SUBLANES, STEP_WIDTH), f32),
        ],
        compiler_params=_params("arbitrary", "arbitrary"),
        name="gdn",
    )(z, z, z, z, gb, conv_w, conv_w, conv_w, norm_gain)


def _rows(v):
    return v.astype(f32)[:, None, :]


def _pad_rows(v):
    return _rows(jnp.pad(v, ((0, 0), (0, LANES - v.shape[1]))))


def kernel(x, w_in, conv_w, a_log, dt_bias, gdn_norm_g, gmlp_ln_g, w_spatial, b_spatial, sba_q_g, sba_k_g, w_out_a, w_out_b, w_out_c, w_out, norm_mix_g, norm_mlp_g, w_ff1, w_ff2):
    batch, seq, d = x.shape
    depth = w_in.shape[0]
    m = batch * seq

    wt = jnp.swapaxes(w_in, 1, 2).reshape(depth * D_IN, d)
    b_rep = jnp.broadcast_to(b_spatial[..., None], b_spatial.shape + (LANES,)).astype(f32)
    mix_g, mlp_g, gdn_g, ln_g = _rows(norm_mix_g), _rows(norm_mlp_g), _rows(gdn_norm_g), _rows(gmlp_ln_g)
    q_g, k_g = _rows(sba_q_g), _rows(sba_k_g)
    alog_rows, dt_rows = _pad_rows(a_log), _pad_rows(dt_bias)
    conv_w = conv_w.astype(f32)
    w_spatial = w_spatial.astype(f32)

    xf = x.reshape(m, d).astype(f32)
    for l in range(depth):
        z, gb = _inproj(xf, mix_g, wt, alog_rows, dt_rows, l)
        oa = _gdn(z, gb, conv_w, gdn_g, batch, seq, l)
        ob = _gmlp(z, ln_g, w_spatial, b_rep, l)
        oc = _sba(z, q_g, k_g, batch, seq, l)
        y = _merge(oa, ob, oc, w_out_a, w_out_b, w_out_c, z, l)
        xf = _matmul_res(y, w_out, xf, l)
        h1 = _ffn_up(xf, mlp_g, w_ff1, l)
        xf = _matmul_res(h1, w_ff2, xf, l)
    return xf.reshape(batch, seq, d).astype(x.dtype)
```

```python
import jax
import jax.numpy as jnp
import numpy as np
from jax import lax
from jax.experimental import pallas as pl
from jax.experimental.pallas import tpu as pltpu

f32 = jnp.float32
bf16 = jnp.bfloat16

D_MODEL = 2048
CHUNK = 64
HEAD_DIM = 128
N_HEADS = 8
WIDTH = N_HEADS * HEAD_DIM
CONV_WIDTH = 4
GMLP_WIDTH = 1024
GMLP_GROUPS = 8
GMLP_BLOCK = 128
N_BRANCHES = 3
EPS = 1e-6
LOG2_E = float(np.log2(np.e))
UNDERFLOW_LOG2 = 160.0

LANES = 128
SUBLANES = 8
BF16_ROWS = 16
VMEM_LIMIT_BYTES = 60000 * 1024

COL_GDN_QKV = 0
COL_GDN_GATE = 3 * WIDTH
COL_GMLP_UV = COL_GDN_GATE + WIDTH
COL_SBA_QKV = COL_GMLP_UV + 2 * GMLP_WIDTH
COL_GATES = COL_SBA_QKV + 3 * WIDTH
N_MAIN = COL_GATES + N_BRANCHES * D_MODEL
ORIG_COL_AB = 3 * WIDTH
N_AB = 2 * N_HEADS
D_IN = N_MAIN + N_AB

PROJ_TM = 2048
PROJ_TN = 768
FFN_UP_TN = 1024
RES_TM = 2048
RES_TN = 1024
RES_TK = 1024
MERGE_TM = 512

HEADS_PER_STEP = 4
STEP_WIDTH = HEADS_PER_STEP * HEAD_DIM
GDN_GROUP = 512
SBA_TQ = 128
SBA_TK = 512
SBA_LEFT_TK = 256
SBA_EAGER_LEFT = 2
SBA_SUB = 128
GMLP_ROWS = 512


def _params(*sem):
    return pltpu.CompilerParams(dimension_semantics=sem, vmem_limit_bytes=VMEM_LIMIT_BYTES)


def _dot(a, b):
    return jnp.dot(a, b, preferred_element_type=f32)


def _dot_nt(a, b):
    return lax.dot_general(a, b, (((1,), (1,)), ((), ())), preferred_element_type=f32)


def _sigmoid(x):
    return lax.logistic(x)


def _silu(x):
    return x * _sigmoid(x)


def _softplus(x):
    return jnp.maximum(x, 0.0) + jnp.log1p(jnp.exp(-jnp.abs(x)))


def _block_id(idx, size):
    return lax.shift_right_logical(idx, int(size).bit_length() - 1)


def _split3(x):
    hi = x.astype(bf16)
    r1 = x - hi.astype(f32)
    mid = r1.astype(bf16)
    lo = (r1 - mid.astype(f32)).astype(bf16)
    return jnp.concatenate([hi, mid, lo], axis=-1)


def _sum3(x, n):
    return x[..., :n] + x[..., n:2 * n] + x[..., 2 * n:3 * n]


def _rms_rows(x, gain):
    ms = jnp.mean(x * x, axis=-1, keepdims=True)
    return x * lax.rsqrt(ms + EPS) * gain


NORM_ROWS = 256


N_NORM_CHUNKS = PROJ_TM // NORM_ROWS


def _norm_scratch():
    return [pltpu.VMEM((2, PROJ_TM, D_MODEL), bf16),
            pltpu.VMEM((2, NORM_ROWS, D_MODEL), f32),
            pltpu.SemaphoreType.DMA((2,))]


def _stream_row_chunks(x_hbm, row0, n_rows, buf, sem, body):
    n = n_rows // NORM_ROWS

    def copy(c, slot):
        src = x_hbm.at[pl.ds(pl.multiple_of(row0 + c * NORM_ROWS, NORM_ROWS), NORM_ROWS), :]
        return pltpu.make_async_copy(src, buf.at[slot], sem.at[slot])

    copy(0, 0).start()

    def step(c, carry):
        slot = jnp.bitwise_and(c, 1)

        @pl.when(c + 1 < n)
        def _():
            copy(c + 1, 1 - slot).start()

        copy(c, slot).wait()
        body(pl.ds(pl.multiple_of(c * NORM_ROWS, NORM_ROWS), NORM_ROWS), buf[slot])
        return carry

    lax.fori_loop(0, n, step, 0)


def _projection_step(x_hbm, xbuf, xsem, normalise, project):
    i, j = pl.program_id(0), pl.program_id(1)
    cur = jnp.bitwise_and(i, 1)

    @pl.when((i == 0) & (j == 0))
    def _():
        _stream_row_chunks(x_hbm, 0, PROJ_TM, xbuf, xsem, lambda rows, x: normalise(0, rows, x))

    ahead = (j < N_NORM_CHUNKS) & (i + 1 < pl.num_programs(0))

    def ahead_copy():
        start = pl.multiple_of((i + 1) * PROJ_TM + j * NORM_ROWS, NORM_ROWS)
        return pltpu.make_async_copy(x_hbm.at[pl.ds(start, NORM_ROWS), :], xbuf.at[0], xsem.at[0])

    @pl.when(ahead)
    def _():
        ahead_copy().start()

    project(cur)

    @pl.when(ahead)
    def _():
        ahead_copy().wait()
        normalise(1 - cur, pl.ds(pl.multiple_of(j * NORM_ROWS, NORM_ROWS), NORM_ROWS), xbuf[0])


def _inproj_kernel(x_hbm, g_ref, wt_ref, wabt_ref, alog_ref, dt_ref, z_ref, gb_ref, hn_ref, xbuf, xsem, gates_ref):
    def normalise(slot, rows, x):
        hn = _rms_rows(x, g_ref[...]).astype(bf16)
        hn_ref[slot, rows, :] = hn
        ab = _dot_nt(hn, wabt_ref[...].astype(bf16))
        lane = lax.broadcasted_iota(jnp.int32, ab.shape, 1)
        g = -jnp.exp(alog_ref[...]) * _softplus(ab + dt_ref[...])
        gates_ref[slot, rows, :] = jnp.where(lane < N_HEADS, g, _sigmoid(ab))

    def project(slot):
        @pl.when(pl.program_id(1) == 0)
        def _():
            gb_ref[...] = gates_ref[slot]

        z_ref[...] = _dot_nt(hn_ref[slot], wt_ref[...].astype(bf16)).astype(z_ref.dtype)

    _projection_step(x_hbm, xbuf, xsem, normalise, project)


def _inproj(x, gain, wt, alog_row, dt_row, l):
    m = x.shape[0]
    n_direct = ORIG_COL_AB // PROJ_TN

    def w_rows(i, j):
        return pl.multiple_of(l * D_IN + j * PROJ_TN + jnp.where(j >= n_direct, N_AB, 0), SUBLANES), 0

    return pl.pallas_call(
        _inproj_kernel,
        grid=(m // PROJ_TM, N_MAIN // PROJ_TN),
        in_specs=[
            pl.BlockSpec(memory_space=pl.ANY),
            pl.BlockSpec((None, 1, D_MODEL), lambda i, j: (l, 0, 0)),
            pl.BlockSpec((pl.Element(PROJ_TN), pl.Element(D_MODEL)), w_rows),
            pl.BlockSpec((pl.Element(LANES), pl.Element(D_MODEL)), lambda i, j: (l * D_IN + ORIG_COL_AB, 0),
                         pipeline_mode=pl.Buffered(1)),
            pl.BlockSpec((None, 1, LANES), lambda i, j: (l, 0, 0)),
            pl.BlockSpec((None, 1, LANES), lambda i, j: (l, 0, 0)),
        ],
        out_specs=[
            pl.BlockSpec((PROJ_TM, PROJ_TN), lambda i, j: (i, j)),
            pl.BlockSpec((PROJ_TM, LANES), lambda i, j: (i, 0)),
        ],
        out_shape=[
            jax.ShapeDtypeStruct((m, N_MAIN), bf16),
            jax.ShapeDtypeStruct((m, LANES), f32),
        ],
        scratch_shapes=_norm_scratch() + [pltpu.VMEM((2, PROJ_TM, LANES), f32)],
        compiler_params=_params("arbitrary", "arbitrary"),
        name="inproj",
    )(x, gain, wt, wt, alog_row, dt_row)


def _ffn_up_kernel(x_hbm, g_ref, w_ref, h_ref, hn_ref, xbuf, xsem):
    def normalise(slot, rows, x):
        hn_ref[slot, rows, :] = _rms_rows(x, g_ref[...]).astype(bf16)

    def project(slot):
        a = jnp.maximum(_dot(hn_ref[slot], w_ref[...].astype(bf16)), 0.0)
        h_ref[...] = (a * a).astype(h_ref.dtype)

    _projection_step(x_hbm, xbuf, xsem, normalise, project)


def _ffn_up(x, gain, w1, l):
    m, n = x.shape[0], w1.shape[-1]
    return pl.pallas_call(
        _ffn_up_kernel,
        grid=(m // PROJ_TM, n // FFN_UP_TN),
        in_specs=[
            pl.BlockSpec(memory_space=pl.ANY),
            pl.BlockSpec((None, 1, D_MODEL), lambda i, j: (l, 0, 0)),
            pl.BlockSpec((None, D_MODEL, FFN_UP_TN), lambda i, j: (l, 0, j)),
        ],
        out_specs=pl.BlockSpec((PROJ_TM, FFN_UP_TN), lambda i, j: (i, j)),
        out_shape=jax.ShapeDtypeStruct((m, n), bf16),
        scratch_shapes=_norm_scratch(),
        compiler_params=_params("arbitrary", "arbitrary"),
        name="ffn_up",
    )(x, gain, w1)


def _matmul_res_kernel(a_ref, w_ref, r_ref, o_ref):
    d = _dot(a_ref[...], w_ref[...].astype(bf16))

    @pl.when(pl.program_id(2) == 0)
    def _():
        o_ref[...] = r_ref[...] + d

    @pl.when(pl.program_id(2) != 0)
    def _():
        o_ref[...] += d


def _matmul_res(a, w, res, l):
    m, k = a.shape
    n = w.shape[-1]
    tm, tk = (RES_TM // 2, 2 * RES_TK) if k > D_MODEL else (RES_TM, RES_TK)
    return pl.pallas_call(
        _matmul_res_kernel,
        grid=(m // tm, n // RES_TN, k // tk),
        in_specs=[
            pl.BlockSpec((tm, tk), lambda i, j, kk: (i, kk)),
            pl.BlockSpec((None, tk, RES_TN), lambda i, j, kk: (l, kk, j)),
            pl.BlockSpec((tm, RES_TN), lambda i, j, kk: (i, j)),
        ],
        out_specs=pl.BlockSpec((tm, RES_TN), lambda i, j, kk: (i, j)),
        out_shape=jax.ShapeDtypeStruct((m, n), f32),
        compiler_params=_params("arbitrary", "arbitrary", "arbitrary"),
        name="matmul_res",
    )(a, w, res)


MERGE_GATE_COLS = 1024
MERGE_GATE_BLOCKS = D_MODEL // MERGE_GATE_COLS


def _merge_kernel(oa_ref, ob_ref, oc_ref, wa_ref, wb_ref, wc_ref, *rest):
    gate_refs, y_ref = rest[:-1], rest[-1]
    branches = ((oa_ref, wa_ref), (ob_ref, wb_ref), (oc_ref, wc_ref))
    for c in range(MERGE_GATE_BLOCKS):
        cols = slice(c * MERGE_GATE_COLS, (c + 1) * MERGE_GATE_COLS)
        y = None
        for b, (o_ref, w_ref) in enumerate(branches):
            gate = _sigmoid(gate_refs[b * MERGE_GATE_BLOCKS + c][...].astype(f32))
            term = gate * _dot(o_ref[...], w_ref[:, cols].astype(bf16))
            y = term if y is None else y + term
        y_ref[:, cols] = y.astype(y_ref.dtype)


def _merge(oa, ob, oc, wa, wb, wc, z, l):
    m = oa.shape[0]
    gate0 = COL_GATES // MERGE_GATE_COLS
    branch_in = pl.BlockSpec((MERGE_TM, WIDTH), lambda i: (i, 0))
    branch_w = pl.BlockSpec((None, WIDTH, D_MODEL), lambda i: (l, 0, 0), pipeline_mode=pl.Buffered(1))
    gate_specs = [pl.BlockSpec((MERGE_TM, MERGE_GATE_COLS), lambda i, blk=gate0 + g: (i, blk))
                  for g in range(N_BRANCHES * MERGE_GATE_BLOCKS)]
    return pl.pallas_call(
        _merge_kernel,
        grid=(m // MERGE_TM,),
        in_specs=[branch_in, branch_in, branch_in, branch_w, branch_w, branch_w] + gate_specs,
        out_specs=pl.BlockSpec((MERGE_TM, D_MODEL), lambda i: (i, 0)),
        out_shape=jax.ShapeDtypeStruct((m, D_MODEL), bf16),
        compiler_params=_params("arbitrary"),
        name="merge",
    )(oa, ob, oc, wa, wb, wc, *([z] * (N_BRANCHES * MERGE_GATE_BLOCKS)))


def _gelu(x):
    return 0.5 * x * (1.0 + lax.erf(x * np.float32(np.sqrt(0.5))))


def _gmlp_kernel(zu_ref, zv_ref, g_ref, ws_ref, bs_ref, o_ref):
    u = _gelu(zu_ref[...].astype(f32))
    v = _gelu(zv_ref[...].astype(f32))
    mu = jnp.mean(v, axis=-1, keepdims=True)
    vc = v - mu
    var = jnp.mean(vc * vc, axis=-1, keepdims=True)
    vb = (vc * lax.rsqrt(var + EPS) * g_ref[...]).astype(bf16)

    t_chunk = _block_id(lax.broadcasted_iota(jnp.int32, (GMLP_BLOCK, GMLP_BLOCK), 0), CHUNK)
    s_chunk = _block_id(lax.broadcasted_iota(jnp.int32, (GMLP_BLOCK, GMLP_BLOCK), 1), CHUNK)
    causal = s_chunk <= t_chunk
    gdim = GMLP_WIDTH // GMLP_GROUPS
    for g in range(GMLP_GROUPS):
        ws = jnp.where(causal, ws_ref[g], 0.0).astype(bf16)
        cols = slice(g * gdim, (g + 1) * gdim)
        for blk in range(GMLP_ROWS // GMLP_BLOCK):
            rows = slice(blk * GMLP_BLOCK, (blk + 1) * GMLP_BLOCK)
            s = _dot(ws, vb[rows, cols]) + bs_ref[g]
            o_ref[rows, cols] = (u[rows, cols] * s).astype(o_ref.dtype)


def _gmlp(z, ln_gain, w_spatial, b_rep, l):
    m = z.shape[0]
    cu = COL_GMLP_UV // GMLP_WIDTH
    return pl.pallas_call(
        _gmlp_kernel,
        grid=(m // GMLP_ROWS,),
        in_specs=[
            pl.BlockSpec((GMLP_ROWS, GMLP_WIDTH), lambda i: (i, cu)),
            pl.BlockSpec((GMLP_ROWS, GMLP_WIDTH), lambda i: (i, cu + 1)),
            pl.BlockSpec((None, 1, GMLP_WIDTH), lambda i: (l, 0, 0)),
            pl.BlockSpec((None, GMLP_GROUPS, GMLP_BLOCK, GMLP_BLOCK), lambda i: (l, 0, 0, 0)),
            pl.BlockSpec((None, GMLP_GROUPS, GMLP_BLOCK, LANES), lambda i: (l, 0, 0, 0)),
        ],
        out_specs=pl.BlockSpec((GMLP_ROWS, GMLP_WIDTH), lambda i: (i, 0)),
        out_shape=jax.ShapeDtypeStruct((m, GMLP_WIDTH), bf16),
        compiler_params=_params("arbitrary"),
        name="gmlp",
    )(z, z, ln_gain, w_spatial, b_rep)


def _sba_kernel(q_ref, k_ref, v_ref, qg_ref, kg_ref, o_ref, qn_ref, kn_ref):
    t = q_ref.shape[0]
    head_cols = [slice(j * HEAD_DIM, (j + 1) * HEAD_DIM) for j in range(HEADS_PER_STEP)]
    for hc in head_cols:
        qn_ref[:, hc] = _rms_rows(q_ref[:, hc].astype(f32), qg_ref[...]).astype(bf16)
        kn_ref[:, hc] = _rms_rows(k_ref[:, hc].astype(f32), kg_ref[...]).astype(bf16)
    to_log2 = np.float32(HEAD_DIM ** -0.5 * LOG2_E)
    rows_all = HEADS_PER_STEP * SBA_TQ
    wr = lax.broadcasted_iota(jnp.int32, (2 * SBA_SUB, 2 * SBA_SUB), 0)
    wc = lax.broadcasted_iota(jnp.int32, (2 * SBA_SUB, 2 * SBA_SUB), 1)
    suffix_w = ((jnp.bitwise_and(wr, SBA_SUB - 1) > wc) | (wc >= SBA_SUB)).astype(bf16)

    below_diag = (lax.broadcasted_iota(jnp.int32, (rows_all, SBA_SUB), 1)
                  < jnp.bitwise_and(lax.broadcasted_iota(jnp.int32, (rows_all, SBA_SUB), 0), SBA_TQ - 1))

    def block(qbs, k0, n_cols, diagonal, state):
        acc, run = state
        z = jnp.concatenate([_dot_nt(qbs[j], kn_ref[pl.ds(k0, n_cols), head_cols[j]])
                             for j in range(HEADS_PER_STEP)], axis=0) * to_log2
        neg_abs = pltpu.bitcast(pltpu.bitcast(z, jnp.uint32) | jnp.uint32(0x80000000), f32)
        soft = jnp.log(1.0 + jnp.exp2(neg_abs)) * np.float32(LOG2_E)
        drop = jnp.maximum(z, 0.0) + soft
        log_beta = z - drop
        probs = []
        offs = run
        last = n_cols // SBA_SUB - 1
        for c in range(last, -1, -1):
            cols = slice(c * SBA_SUB, (c + 1) * SBA_SUB)
            masked = diagonal and c == last
            dc = drop[:, cols]
            if masked:
                dc = jnp.where(below_diag, dc, 0.0)
            hi = dc.astype(bf16)
            lo = (dc - hi.astype(f32)).astype(bf16)
            sums = _dot(jnp.concatenate([hi, lo], axis=-1), suffix_w)
            a = jnp.exp2(log_beta[:, cols] - sums[:, :SBA_SUB] - offs)
            if masked:
                a = jnp.where(below_diag, a, 0.0)
            probs.append(a.astype(bf16))
            offs = offs + sums[:, SBA_SUB:]
        a_blk = probs[0] if last == 0 else jnp.concatenate(probs[::-1], axis=-1)
        pv = jnp.concatenate(
            [_dot(a_blk[j * SBA_TQ:(j + 1) * SBA_TQ], v_ref[pl.ds(k0, n_cols), head_cols[j]])
             for j in range(HEADS_PER_STEP)], axis=0)
        return acc + pv, offs

    def key_block_row(s, has_left):
        k_diag = pl.multiple_of(s * SBA_TK, SBA_TK)
        n_q = SBA_TK // SBA_TQ
        q0s = [pl.multiple_of(k_diag + p * SBA_TQ, SBA_TQ) for p in range(n_q)]
        qbss = [[qn_ref[pl.ds(q0, SBA_TQ), hc] for hc in head_cols] for q0 in q0s]
        zeros = jnp.zeros((rows_all, HEAD_DIM), f32)
        states = [block(qbss[p], k_diag, (p + 1) * SBA_SUB, True, (zeros, zeros)) for p in range(n_q)]
        taken = [0] * n_q
        if has_left:
            k_first = pl.multiple_of(k_diag - SBA_LEFT_TK, SBA_LEFT_TK)
            for p in range(SBA_EAGER_LEFT):
                states[p] = block(qbss[p], k_first, SBA_LEFT_TK, False, states[p])
                taken[p] = 1
        for p in range(n_q):
            q0, qbs, state = q0s[p], qbss[p], states[p]

            n_left = s * (SBA_TK // SBA_LEFT_TK)

            def more(c):
                it, _, run = c
                return (it < n_left) & (jnp.min(run) <= UNDERFLOW_LOG2)

            def left_block(c):
                it, acc, run = c
                k0 = pl.multiple_of((n_left - 1 - it) * SBA_LEFT_TK, SBA_LEFT_TK)
                acc, run = block(qbs, k0, SBA_LEFT_TK, False, (acc, run))
                return it + 1, acc, run

            _, acc, _ = lax.while_loop(more, left_block, (jnp.int32(taken[p]),) + state)
            for j in range(HEADS_PER_STEP):
                o_ref[pl.ds(q0, SBA_TQ), head_cols[j]] = acc[j * SBA_TQ:(j + 1) * SBA_TQ].astype(o_ref.dtype)

    key_block_row(jnp.int32(0), False)

    def later_row(s, carry):
        key_block_row(s, True)
        return carry

    lax.fori_loop(1, t // SBA_TK, later_row, 0)


def _sba(z, q_gain, k_gain, batch, seq, l):
    m = z.shape[0]
    c0 = COL_SBA_QKV // STEP_WIDTH
    per_seg = WIDTH // STEP_WIDTH

    def head_spec(seg):
        return pl.BlockSpec((seq, STEP_WIDTH), lambda b, h: (b, c0 + seg * per_seg + h))

    gain_spec = pl.BlockSpec((None, 1, HEAD_DIM), lambda b, h: (l, 0, 0))
    tok = pltpu.VMEM((seq, STEP_WIDTH), bf16)
    return pl.pallas_call(
        _sba_kernel,
        grid=(batch, per_seg),
        in_specs=[head_spec(0), head_spec(1), head_spec(2), gain_spec, gain_spec],
        out_specs=pl.BlockSpec((seq, STEP_WIDTH), lambda b, h: (b, h)),
        out_shape=jax.ShapeDtypeStruct((m, WIDTH), bf16),
        scratch_shapes=[tok, tok],
        compiler_params=_params("arbitrary", "arbitrary"),
        name="sba",
    )(z, z, z, q_gain, k_gain)


XPAD = BF16_ROWS


def _bmm(a, b):
    return jnp.einsum('bij,bjk->bik', a, b, preferred_element_type=f32)


def _bmm_nt(a, b):
    return jnp.einsum('bid,bjd->bij', a, b, preferred_element_type=f32)


def _bmm_tn(a, b):
    return jnp.einsum('bck,bcv->bkv', a, b, preferred_element_type=f32)


def _unit_lower_inverse(l_mat, row, col):
    def same_block(d):
        return _block_id(row, d) == _block_id(col, d)

    eye = (row == col).astype(f32)
    l8 = jnp.where(same_block(8), l_mat, 0.0)
    p = l8.astype(bf16)
    x = (eye - l8).astype(bf16)
    p2 = _bmm(p, p)
    x = _bmm(x, (eye + p2).astype(bf16)).astype(bf16)
    p2 = p2.astype(bf16)
    p4 = _bmm(p2, p2)
    x = _bmm(x, (eye + p4).astype(bf16)).astype(bf16)
    d = 8
    while d < CHUNK:
        off = jnp.where(same_block(2 * d) & jnp.logical_not(same_block(d)), l_mat, 0.0).astype(bf16)
        x = _bmm(x, (eye - _bmm(off, x)).astype(bf16)).astype(bf16)
        d *= 2
    return x


SCAN_ROWS = HEAD_DIM + CHUNK


def _gdn_kernel(zq_ref, zk_ref, zv_ref, zg_ref, gb_ref, cwq_ref, cwk_ref, cwv_ref, ng_ref, o_ref,
                xq_ref, xk_ref, xv_ref, stage_ref, lin_ref, off_ref, eg_ref):
    t = zq_ref.shape[0]
    head0 = pl.program_id(1) * HEADS_PER_STEP
    grp = GDN_GROUP
    nc = grp // CHUNK
    head_cols = [slice(j * HEAD_DIM, (j + 1) * HEAD_DIM) for j in range(HEADS_PER_STEP)]

    zeros_pad = jnp.zeros((XPAD, STEP_WIDTH), bf16)
    for src, dst in ((zq_ref, xq_ref), (zk_ref, xk_ref), (zv_ref, xv_ref)):
        dst[pl.ds(0, XPAD), :] = zeros_pad
        dst[pl.ds(XPAD, t), :] = src[...]

    row = lax.broadcasted_iota(jnp.int32, (CHUNK, CHUNK), 0)
    col = lax.broadcasted_iota(jnp.int32, (CHUNK, CHUNK), 1)
    incl = row >= col
    strict = row > col
    tri_ones = jnp.concatenate([incl.astype(bf16), jnp.ones((CHUNK, CHUNK), bf16)], axis=0)
    tri_ones = jnp.broadcast_to(tri_ones, (nc, 2 * CHUNK, CHUNK))
    sel_src = jnp.bitwise_and(lax.broadcasted_iota(jnp.int32, (3 * LANES, STEP_WIDTH), 0), LANES - 1)
    sel_head = head0 + _block_id(lax.broadcasted_iota(jnp.int32, (3 * LANES, STEP_WIDTH), 1), HEAD_DIM)
    pick_g = (sel_src == sel_head).astype(bf16)
    pick_b = (sel_src == sel_head + N_HEADS).astype(bf16)
    sel_rows = (jnp.bitwise_and(lax.broadcasted_iota(jnp.int32, (SUBLANES, 3 * LANES), 1), LANES - 1)
                == head0 + lax.broadcasted_iota(jnp.int32, (SUBLANES, 3 * LANES), 0)).astype(bf16)
    sel_rows = jnp.broadcast_to(sel_rows, (nc, SUBLANES, 3 * LANES))

    def conv_silu(x_ref, cw_ref, r0, hc, slot):
        stage = stage_ref.at[slot]
        stage[...] = x_ref[pl.ds(r0, grp + XPAD), hc].astype(f32)
        acc = cw_ref[pl.ds(CONV_WIDTH - 1, 1), hc] * stage[pl.ds(XPAD, grp), :]
        for i in range(CONV_WIDTH - 1):
            acc += cw_ref[pl.ds(i, 1), hc] * stage[pl.ds(XPAD - (CONV_WIDTH - 1) + i, grp), :]
        return _silu(acc)

    def l2n(x):
        return x * lax.rsqrt(jnp.sum(x * x, axis=-1, keepdims=True) + EPS)

    def chunked(x):
        return x.reshape(nc, CHUNK, x.shape[-1])

    def slot_of(g):
        return jnp.bitwise_and(g, 1)

    def prepare(it, carry):
        r0 = pl.multiple_of(it * grp, grp)
        gb3 = _split3(gb_ref[pl.ds(r0, grp), :])
        ct = _bmm(tri_ones, chunked(gb3))
        cum3 = _split3(_sum3(ct[:, :CHUNK], LANES).reshape(grp, LANES))
        tot3 = _split3(_sum3(ct[:, CHUNK:], LANES).reshape(grp, LANES))
        gc_rows = _bmm_nt(sel_rows, chunked(cum3))

        gc_all = _dot(cum3, pick_g)
        g_tot_all = _dot(tot3, pick_g)
        beta_all = _dot(gb3, pick_b)

        k16s, kb16s, q16s, rhs16s, diffs, qds, kd16s = [], [], [], [], [], [], []
        for j, hc in enumerate(head_cols):
            gc, g_tot, beta = gc_all[:, hc], g_tot_all[:, hc], beta_all[:, hc]
            q = l2n(conv_silu(xq_ref, cwq_ref, r0, hc, 3 * j)) * np.float32(HEAD_DIM ** -0.5)
            k = l2n(conv_silu(xk_ref, cwk_ref, r0, hc, 3 * j + 1))
            v = conv_silu(xv_ref, cwv_ref, r0, hc, 3 * j + 2)
            kb = k * beta
            e_gc = jnp.exp(gc)
            k16s.append(chunked(k.astype(bf16)))
            kb16s.append(chunked(kb.astype(bf16)))
            q16s.append(chunked(q.astype(bf16)))
            rhs16s.append(chunked(jnp.concatenate([v * beta, kb * e_gc], axis=-1).astype(bf16)))
            diffs.append(chunked(gc)[:, :, :CHUNK] - gc_rows[:, j:j + 1, :])
            qds.append(chunked(q * e_gc))
            kd16s.append(chunked((k * jnp.exp(g_tot - gc)).astype(bf16)))
            e0 = pl.multiple_of(slot_of(it) * (nc * SUBLANES), nc * SUBLANES)
            eg_ref[pl.ds(e0, nc * SUBLANES), hc] = (
                chunked(jnp.exp(g_tot))[:, :SUBLANES, :].reshape(nc * SUBLANES, HEAD_DIM))

        cat = lambda xs: jnp.concatenate(xs, axis=0)
        k16, diff = cat(k16s), cat(diffs)
        decay = jnp.where(incl, jnp.exp(jnp.where(incl, diff, 0.0)), 0.0)
        l_mat = jnp.where(strict, _bmm_nt(cat(kb16s), k16) * decay, 0.0)
        intra = (_bmm_nt(cat(q16s), k16) * decay).astype(bf16)
        t_inv = _unit_lower_inverse(l_mat, row, col)
        uw = _bmm(t_inv, cat(rhs16s)).astype(bf16)
        state_map = _bmm_tn(cat(kd16s), uw)
        out_map = _bmm(intra, uw)
        q_eff = cat(qds) - out_map[:, :, HEAD_DIM:]
        lin = jnp.concatenate([state_map[:, :, HEAD_DIM:], q_eff], axis=1).astype(bf16)
        off = jnp.concatenate([state_map[:, :, :HEAD_DIM], out_map[:, :, :HEAD_DIM]], axis=1)
        s0 = pl.multiple_of(slot_of(it) * (nc * SCAN_ROWS), nc * SCAN_ROWS)
        for j, hc in enumerate(head_cols):
            mine = slice(j * nc, (j + 1) * nc)
            lin_ref[pl.ds(s0, nc * SCAN_ROWS), hc] = lin[mine].reshape(nc * SCAN_ROWS, HEAD_DIM)
            off_ref[pl.ds(s0, nc * SCAN_ROWS), hc] = off[mine].reshape(nc * SCAN_ROWS, HEAD_DIM)
        return carry

    def scan(slot_chunk, out_chunk, states):
        s0 = pl.multiple_of(slot_chunk * SCAN_ROWS, SCAN_ROWS)
        e0 = pl.multiple_of(slot_chunk * SUBLANES, SUBLANES)
        r0 = pl.multiple_of(out_chunk * CHUNK, CHUNK)
        new_states = []
        for j, hc in enumerate(head_cols):
            state = states[j]
            prod = _dot(lin_ref[pl.ds(s0, SCAN_ROWS), hc], state.astype(bf16))
            off = off_ref[pl.ds(s0, SCAN_ROWS), hc]
            new_states.append(state * eg_ref[pl.ds(e0, 1), hc] + (off[:HEAD_DIM] - prod[:HEAD_DIM]))
            o = prod[HEAD_DIM:] + off[HEAD_DIM:]
            gate = zg_ref[pl.ds(r0, CHUNK), hc].astype(f32)
            o_ref[pl.ds(r0, CHUNK), hc] = (_rms_rows(o, ng_ref[...]) * _silu(gate)).astype(o_ref.dtype)
        return tuple(new_states)

    def scan_group(slot, out_group, states):
        for c in range(nc):
            states = scan(slot * nc + c, out_group * nc + c, states)
        return states

    def step(g, states):
        states = scan_group(slot_of(g + 1), jnp.maximum(g - 1, 0), states)
        prepare(g, 0)
        return states

    n_groups = t // grp
    lin_ref[pl.ds(nc * SCAN_ROWS, nc * SCAN_ROWS), :] = jnp.zeros((nc * SCAN_ROWS, STEP_WIDTH), bf16)
    off_ref[pl.ds(nc * SCAN_ROWS, nc * SCAN_ROWS), :] = jnp.zeros((nc * SCAN_ROWS, STEP_WIDTH), f32)
    eg_ref[pl.ds(nc * SUBLANES, nc * SUBLANES), :] = jnp.zeros((nc * SUBLANES, STEP_WIDTH), f32)
    zero_state = jnp.zeros((HEAD_DIM, HEAD_DIM), f32)
    states = lax.fori_loop(0, n_groups, step, (zero_state,) * HEADS_PER_STEP)
    scan_group(jnp.int32((n_groups - 1) % 2), jnp.int32(n_groups - 1), states)


def _gdn(z, gb, conv_w, norm_gain, batch, seq, l):
    m = z.shape[0]
    per_seg = WIDTH // STEP_WIDTH
    cq = COL_GDN_QKV // STEP_WIDTH
    cg = COL_GDN_GATE // STEP_WIDTH

    def head_spec(c):
        return pl.BlockSpec((seq, STEP_WIDTH), lambda b, h: (b, c + h))

    def conv_spec(seg):
        return pl.BlockSpec((None, CONV_WIDTH, STEP_WIDTH), lambda b, h: (l, 0, seg * per_seg + h))

    ring_chunks = 2 * GDN_GROUP // CHUNK
    padded = pltpu.VMEM((seq + XPAD, STEP_WIDTH), bf16)
    return pl.pallas_call(
        _gdn_kernel,
        grid=(batch, per_seg),
        in_specs=[
            head_spec(cq), head_spec(cq + per_seg), head_spec(cq + 2 * per_seg), head_spec(cg),
            pl.BlockSpec((seq, LANES), lambda b, h: (b, 0)),
            conv_spec(0), conv_spec(1), conv_spec(2),
            pl.BlockSpec((None, 1, HEAD_DIM), lambda b, h: (l, 0, 0)),
        ],
        out_specs=pl.BlockSpec((seq, STEP_WIDTH), lambda b, h: (b, h)),
        out_shape=jax.ShapeDtypeStruct((m, WIDTH), bf16),
        scratch_shapes=[
            padded, padded, padded,
            pltpu.VMEM((3 * HEADS_PER_STEP, GDN_GROUP + XPAD, HEAD_DIM), f32),
            pltpu.VMEM((ring_chunks * SCAN_ROWS, STEP_WIDTH), bf16),
            pltpu.VMEM((ring_chunks * SCAN_ROWS, STEP_WIDTH), f32),
            pltpu.VMEM((ring_chunks * SUBLANES, STEP_WIDTH), f32),
        ],
        compiler_params=_params("arbitrary", "arbitrary"),
        name="gdn",
    )(z, z, z, z, gb, conv_w, conv_w, conv_w, norm_gain)


def _rows(v):
    return v.astype(f32)[:, None, :]


def _pad_rows(v):
    return _rows(jnp.pad(v, ((0, 0), (0, LANES - v.shape[1]))))


def kernel(x, w_in, conv_w, a_log, dt_bias, gdn_norm_g, gmlp_ln_g, w_spatial, b_spatial, sba_q_g, sba_k_g, w_out_a, w_out_b, w_out_c, w_out, norm_mix_g, norm_mlp_g, w_ff1, w_ff2):
    batch, seq, d = x.shape
    depth = w_in.shape[0]
    m = batch * seq

    wt = jnp.swapaxes(w_in, 1, 2).reshape(depth * D_IN, d)
    b_rep = jnp.broadcast_to(b_spatial[..., None], b_spatial.shape + (LANES,)).astype(f32)
    mix_g, mlp_g, gdn_g, ln_g = _rows(norm_mix_g), _rows(norm_mlp_g), _rows(gdn_norm_g), _rows(gmlp_ln_g)
    q_g, k_g = _rows(sba_q_g), _rows(sba_k_g)
    alog_rows, dt_rows = _pad_rows(a_log), _pad_rows(dt_bias)
    conv_w = conv_w.astype(f32)
    w_spatial = w_spatial.astype(f32)

    xf = x.reshape(m, d).astype(f32)
    for l in range(depth):
        z, gb = _inproj(xf, mix_g, wt, alog_rows, dt_rows, l)
        oa = _gdn(z, gb, conv_w, gdn_g, batch, seq, l)
        ob = _gmlp(z, ln_g, w_spatial, b_rep, l)
        oc = _sba(z, q_g, k_g, batch, seq, l)
        y = _merge(oa, ob, oc, w_out_a, w_out_b, w_out_c, z, l)
        xf = _matmul_res(y, w_out, xf, l)
        h1 = _ffn_up(xf, mlp_g, w_ff1, l)
        xf = _matmul_res(h1, w_ff2, xf, l)
    return xf.reshape(batch, seq, d).astype(x.dtype)
```

```python
import jax
import jax.numpy as jnp
import numpy as np
from jax import lax
from jax.experimental import pallas as pl
from jax.experimental.pallas import tpu as pltpu

f32 = jnp.float32
bf16 = jnp.bfloat16

D_MODEL = 2048
CHUNK = 64
HEAD_DIM = 128
N_HEADS = 8
WIDTH = N_HEADS * HEAD_DIM
CONV_WIDTH = 4
GMLP_WIDTH = 1024
GMLP_GROUPS = 8
GMLP_BLOCK = 128
N_BRANCHES = 3
EPS = 1e-6
LOG2_E = float(np.log2(np.e))
UNDERFLOW_LOG2 = 160.0

LANES = 128
SUBLANES = 8
BF16_ROWS = 16
VMEM_LIMIT_BYTES = 60000 * 1024

COL_GDN_QKV = 0
COL_GDN_GATE = 3 * WIDTH
COL_GMLP_UV = COL_GDN_GATE + WIDTH
COL_SBA_QKV = COL_GMLP_UV + 2 * GMLP_WIDTH
COL_GATES = COL_SBA_QKV + 3 * WIDTH
N_MAIN = COL_GATES + N_BRANCHES * D_MODEL
ORIG_COL_AB = 3 * WIDTH
N_AB = 2 * N_HEADS
D_IN = N_MAIN + N_AB

PROJ_TM = 2048
PROJ_TN = 768
FFN_UP_TN = 1024
RES_TM = 1024
RES_TN = 1024
RES_TK = 2048
MERGE_TM = 512

HEADS_PER_STEP = 4
STEP_WIDTH = HEADS_PER_STEP * HEAD_DIM
GDN_GROUP = 512
SBA_TQ = 128
SBA_TK = 512
SBA_LEFT_TK = 256
SBA_EAGER_LEFT = 2
SBA_SUB = 128
GMLP_ROWS = 512


def _params(*sem):
    return pltpu.CompilerParams(dimension_semantics=sem, vmem_limit_bytes=VMEM_LIMIT_BYTES)


def _dot(a, b):
    return jnp.dot(a, b, preferred_element_type=f32)


def _dot_nt(a, b):
    return lax.dot_general(a, b, (((1,), (1,)), ((), ())), preferred_element_type=f32)


def _sigmoid(x):
    return lax.logistic(x)


def _silu(x):
    return x * _sigmoid(x)


def _softplus(x):
    return jnp.maximum(x, 0.0) + jnp.log1p(jnp.exp(-jnp.abs(x)))


def _block_id(idx, size):
    return lax.shift_right_logical(idx, int(size).bit_length() - 1)


def _split3(x):
    hi = x.astype(bf16)
    r1 = x - hi.astype(f32)
    mid = r1.astype(bf16)
    lo = (r1 - mid.astype(f32)).astype(bf16)
    return jnp.concatenate([hi, mid, lo], axis=-1)


def _sum3(x, n):
    return x[..., :n] + x[..., n:2 * n] + x[..., 2 * n:3 * n]


def _rms_rows(x, gain):
    ms = jnp.mean(x * x, axis=-1, keepdims=True)
    return x * lax.rsqrt(ms + EPS) * gain


NORM_ROWS = 256


N_NORM_CHUNKS = PROJ_TM // NORM_ROWS


def _norm_scratch():
    return [pltpu.VMEM((2, PROJ_TM, D_MODEL), bf16),
            pltpu.VMEM((2, NORM_ROWS, D_MODEL), f32),
            pltpu.SemaphoreType.DMA((2,))]


def _stream_row_chunks(x_hbm, row0, n_rows, buf, sem, body):
    n = n_rows // NORM_ROWS

    def copy(c, slot):
        src = x_hbm.at[pl.ds(pl.multiple_of(row0 + c * NORM_ROWS, NORM_ROWS), NORM_ROWS), :]
        return pltpu.make_async_copy(src, buf.at[slot], sem.at[slot])

    copy(0, 0).start()

    def step(c, carry):
        slot = jnp.bitwise_and(c, 1)

        @pl.when(c + 1 < n)
        def _():
            copy(c + 1, 1 - slot).start()

        copy(c, slot).wait()
        body(pl.ds(pl.multiple_of(c * NORM_ROWS, NORM_ROWS), NORM_ROWS), buf[slot])
        return carry

    lax.fori_loop(0, n, step, 0)


def _projection_step(x_hbm, xbuf, xsem, normalise, project):
    i, j = pl.program_id(0), pl.program_id(1)
    cur = jnp.bitwise_and(i, 1)

    @pl.when((i == 0) & (j == 0))
    def _():
        _stream_row_chunks(x_hbm, 0, PROJ_TM, xbuf, xsem, lambda rows, x: normalise(0, rows, x))

    ahead = (j < N_NORM_CHUNKS) & (i + 1 < pl.num_programs(0))

    def ahead_copy():
        start = pl.multiple_of((i + 1) * PROJ_TM + j * NORM_ROWS, NORM_ROWS)
        return pltpu.make_async_copy(x_hbm.at[pl.ds(start, NORM_ROWS), :], xbuf.at[0], xsem.at[0])

    @pl.when(ahead)
    def _():
        ahead_copy().start()

    project(cur)

    @pl.when(ahead)
    def _():
        ahead_copy().wait()
        normalise(1 - cur, pl.ds(pl.multiple_of(j * NORM_ROWS, NORM_ROWS), NORM_ROWS), xbuf[0])


def _inproj_kernel(x_hbm, g_ref, wt_ref, wabt_ref, alog_ref, dt_ref, z_ref, gb_ref, hn_ref, xbuf, xsem, gates_ref):
    def normalise(slot, rows, x):
        hn = _rms_rows(x, g_ref[...]).astype(bf16)
        hn_ref[slot, rows, :] = hn
        ab = _dot_nt(hn, wabt_ref[...].astype(bf16))
        lane = lax.broadcasted_iota(jnp.int32, ab.shape, 1)
        g = -jnp.exp(alog_ref[...]) * _softplus(ab + dt_ref[...])
        gates_ref[slot, rows, :] = jnp.where(lane < N_HEADS, g, _sigmoid(ab))

    def project(slot):
        @pl.when(pl.program_id(1) == 0)
        def _():
            gb_ref[...] = gates_ref[slot]

        z_ref[...] = _dot_nt(hn_ref[slot], wt_ref[...].astype(bf16)).astype(z_ref.dtype)

    _projection_step(x_hbm, xbuf, xsem, normalise, project)


def _inproj(x, gain, wt, alog_row, dt_row, l):
    m = x.shape[0]
    assert N_MAIN // PROJ_TN >= N_NORM_CHUNKS
    n_direct = ORIG_COL_AB // PROJ_TN

    def w_rows(i, j):
        return pl.multiple_of(l * D_IN + j * PROJ_TN + jnp.where(j >= n_direct, N_AB, 0), SUBLANES), 0

    return pl.pallas_call(
        _inproj_kernel,
        grid=(m // PROJ_TM, N_MAIN // PROJ_TN),
        in_specs=[
            pl.BlockSpec(memory_space=pl.ANY),
            pl.BlockSpec((None, 1, D_MODEL), lambda i, j: (l, 0, 0)),
            pl.BlockSpec((pl.Element(PROJ_TN), pl.Element(D_MODEL)), w_rows),
            pl.BlockSpec((pl.Element(LANES), pl.Element(D_MODEL)), lambda i, j: (l * D_IN + ORIG_COL_AB, 0),
                         pipeline_mode=pl.Buffered(1)),
            pl.BlockSpec((None, 1, LANES), lambda i, j: (l, 0, 0)),
            pl.BlockSpec((None, 1, LANES), lambda i, j: (l, 0, 0)),
        ],
        out_specs=[
            pl.BlockSpec((PROJ_TM, PROJ_TN), lambda i, j: (i, j)),
            pl.BlockSpec((PROJ_TM, LANES), lambda i, j: (i, 0)),
        ],
        out_shape=[
            jax.ShapeDtypeStruct((m, N_MAIN), bf16),
            jax.ShapeDtypeStruct((m, LANES), f32),
        ],
        scratch_shapes=_norm_scratch() + [pltpu.VMEM((2, PROJ_TM, LANES), f32)],
        compiler_params=_params("arbitrary", "arbitrary"),
        name="inproj",
    )(x, gain, wt, wt, alog_row, dt_row)


def _ffn_up_kernel(x_hbm, g_ref, w_ref, h_ref, hn_ref, xbuf, xsem):
    def normalise(slot, rows, x):
        hn_ref[slot, rows, :] = _rms_rows(x, g_ref[...]).astype(bf16)

    def project(slot):
        a = jnp.maximum(_dot(hn_ref[slot], w_ref[...].astype(bf16)), 0.0)
        h_ref[...] = (a * a).astype(h_ref.dtype)

    _projection_step(x_hbm, xbuf, xsem, normalise, project)


def _ffn_up(x, gain, w1, l):
    m, n = x.shape[0], w1.shape[-1]
    assert n // FFN_UP_TN >= N_NORM_CHUNKS
    return pl.pallas_call(
        _ffn_up_kernel,
        grid=(m // PROJ_TM, n // FFN_UP_TN),
        in_specs=[
            pl.BlockSpec(memory_space=pl.ANY),
            pl.BlockSpec((None, 1, D_MODEL), lambda i, j: (l, 0, 0)),
            pl.BlockSpec((None, D_MODEL, FFN_UP_TN), lambda i, j: (l, 0, j)),
        ],
        out_specs=pl.BlockSpec((PROJ_TM, FFN_UP_TN), lambda i, j: (i, j)),
        out_shape=jax.ShapeDtypeStruct((m, n), bf16),
        scratch_shapes=_norm_scratch(),
        compiler_params=_params("arbitrary", "arbitrary"),
        name="ffn_up",
    )(x, gain, w1)


def _matmul_res_kernel(a_ref, w_ref, r_ref, o_ref):
    d = _dot(a_ref[...], w_ref[...].astype(bf16))

    @pl.when(pl.program_id(2) == 0)
    def _():
        o_ref[...] = r_ref[...] + d

    @pl.when(pl.program_id(2) != 0)
    def _():
        o_ref[...] += d


def _matmul_res(a, w, res, l):
    m, k = a.shape
    n = w.shape[-1]
    return pl.pallas_call(
        _matmul_res_kernel,
        grid=(m // RES_TM, n // RES_TN, k // RES_TK),
        in_specs=[
            pl.BlockSpec((RES_TM, RES_TK), lambda i, j, kk: (i, kk)),
            pl.BlockSpec((None, RES_TK, RES_TN), lambda i, j, kk: (l, kk, j)),
            pl.BlockSpec((RES_TM, RES_TN), lambda i, j, kk: (i, j)),
        ],
        out_specs=pl.BlockSpec((RES_TM, RES_TN), lambda i, j, kk: (i, j)),
        out_shape=jax.ShapeDtypeStruct((m, n), f32),
        compiler_params=_params("arbitrary", "arbitrary", "arbitrary"),
        name="matmul_res",
    )(a, w, res)


MERGE_GATE_COLS = 1024
MERGE_GATE_BLOCKS = D_MODEL // MERGE_GATE_COLS


def _merge_kernel(oa_ref, ob_ref, oc_ref, wa_ref, wb_ref, wc_ref, *rest):
    gate_refs, y_ref = rest[:-1], rest[-1]
    branches = ((oa_ref, wa_ref), (ob_ref, wb_ref), (oc_ref, wc_ref))
    for c in range(MERGE_GATE_BLOCKS):
        cols = slice(c * MERGE_GATE_COLS, (c + 1) * MERGE_GATE_COLS)
        y = None
        for b, (o_ref, w_ref) in enumerate(branches):
            gate = _sigmoid(gate_refs[b * MERGE_GATE_BLOCKS + c][...].astype(f32))
            term = gate * _dot(o_ref[...], w_ref[:, cols].astype(bf16))
            y = term if y is None else y + term
        y_ref[:, cols] = y.astype(y_ref.dtype)


def _merge(oa, ob, oc, wa, wb, wc, z, l):
    m = oa.shape[0]
    gate0 = COL_GATES // MERGE_GATE_COLS
    branch_in = pl.BlockSpec((MERGE_TM, WIDTH), lambda i: (i, 0))
    branch_w = pl.BlockSpec((None, WIDTH, D_MODEL), lambda i: (l, 0, 0), pipeline_mode=pl.Buffered(1))
    gate_specs = [pl.BlockSpec((MERGE_TM, MERGE_GATE_COLS), lambda i, blk=gate0 + g: (i, blk))
                  for g in range(N_BRANCHES * MERGE_GATE_BLOCKS)]
    return pl.pallas_call(
        _merge_kernel,
        grid=(m // MERGE_TM,),
        in_specs=[branch_in, branch_in, branch_in, branch_w, branch_w, branch_w] + gate_specs,
        out_specs=pl.BlockSpec((MERGE_TM, D_MODEL), lambda i: (i, 0)),
        out_shape=jax.ShapeDtypeStruct((m, D_MODEL), bf16),
        compiler_params=_params("arbitrary"),
        name="merge",
    )(oa, ob, oc, wa, wb, wc, *([z] * (N_BRANCHES * MERGE_GATE_BLOCKS)))


def _gelu(x):
    return 0.5 * x * (1.0 + lax.erf(x * np.float32(np.sqrt(0.5))))


def _gmlp_kernel(zu_ref, zv_ref, g_ref, ws_ref, bs_ref, o_ref):
    u = _gelu(zu_ref[...].astype(f32))
    v = _gelu(zv_ref[...].astype(f32))
    mu = jnp.mean(v, axis=-1, keepdims=True)
    vc = v - mu
    var = jnp.mean(vc * vc, axis=-1, keepdims=True)
    vb = (vc * lax.rsqrt(var + EPS) * g_ref[...]).astype(bf16)

    t_chunk = _block_id(lax.broadcasted_iota(jnp.int32, (GMLP_BLOCK, GMLP_BLOCK), 0), CHUNK)
    s_chunk = _block_id(lax.broadcasted_iota(jnp.int32, (GMLP_BLOCK, GMLP_BLOCK), 1), CHUNK)
    causal = s_chunk <= t_chunk
    gdim = GMLP_WIDTH // GMLP_GROUPS
    for g in range(GMLP_GROUPS):
        ws = jnp.where(causal, ws_ref[g], 0.0).astype(bf16)
        cols = slice(g * gdim, (g + 1) * gdim)
        for blk in range(GMLP_ROWS // GMLP_BLOCK):
            rows = slice(blk * GMLP_BLOCK, (blk + 1) * GMLP_BLOCK)
            s = _dot(ws, vb[rows, cols]) + bs_ref[g]
            o_ref[rows, cols] = (u[rows, cols] * s).astype(o_ref.dtype)


def _gmlp(z, ln_gain, w_spatial, b_rep, l):
    m = z.shape[0]
    cu = COL_GMLP_UV // GMLP_WIDTH
    return pl.pallas_call(
        _gmlp_kernel,
        grid=(m // GMLP_ROWS,),
        in_specs=[
            pl.BlockSpec((GMLP_ROWS, GMLP_WIDTH), lambda i: (i, cu)),
            pl.BlockSpec((GMLP_ROWS, GMLP_WIDTH), lambda i: (i, cu + 1)),
            pl.BlockSpec((None, 1, GMLP_WIDTH), lambda i: (l, 0, 0)),
            pl.BlockSpec((None, GMLP_GROUPS, GMLP_BLOCK, GMLP_BLOCK), lambda i: (l, 0, 0, 0)),
            pl.BlockSpec((None, GMLP_GROUPS, GMLP_BLOCK, LANES), lambda i: (l, 0, 0, 0)),
        ],
        out_specs=pl.BlockSpec((GMLP_ROWS, GMLP_WIDTH), lambda i: (i, 0)),
        out_shape=jax.ShapeDtypeStruct((m, GMLP_WIDTH), bf16),
        compiler_params=_params("arbitrary"),
        name="gmlp",
    )(z, z, ln_gain, w_spatial, b_rep)


def _sba_kernel(q_ref, k_ref, v_ref, qg_ref, kg_ref, o_ref, qn_ref, kn_ref):
    t = q_ref.shape[0]
    head_cols = [slice(j * HEAD_DIM, (j + 1) * HEAD_DIM) for j in range(HEADS_PER_STEP)]
    for hc in head_cols:
        qn_ref[:, hc] = _rms_rows(q_ref[:, hc].astype(f32), qg_ref[...]).astype(bf16)
        kn_ref[:, hc] = _rms_rows(k_ref[:, hc].astype(f32), kg_ref[...]).astype(bf16)
    to_log2 = np.float32(HEAD_DIM ** -0.5 * LOG2_E)
    rows_all = HEADS_PER_STEP * SBA_TQ
    wr = lax.broadcasted_iota(jnp.int32, (2 * SBA_SUB, 2 * SBA_SUB), 0)
    wc = lax.broadcasted_iota(jnp.int32, (2 * SBA_SUB, 2 * SBA_SUB), 1)
    suffix_w = ((jnp.bitwise_and(wr, SBA_SUB - 1) > wc) | (wc >= SBA_SUB)).astype(bf16)

    below_diag = (lax.broadcasted_iota(jnp.int32, (rows_all, SBA_SUB), 1)
                  < jnp.bitwise_and(lax.broadcasted_iota(jnp.int32, (rows_all, SBA_SUB), 0), SBA_TQ - 1))

    def block(qbs, k0, n_cols, diagonal, state):
        acc, run = state
        z = jnp.concatenate([_dot_nt(qbs[j], kn_ref[pl.ds(k0, n_cols), head_cols[j]])
                             for j in range(HEADS_PER_STEP)], axis=0) * to_log2
        neg_abs = pltpu.bitcast(pltpu.bitcast(z, jnp.uint32) | jnp.uint32(0x80000000), f32)
        soft = jnp.log(1.0 + jnp.exp2(neg_abs)) * np.float32(LOG2_E)
        drop = jnp.maximum(z, 0.0) + soft
        log_beta = z - drop
        probs = []
        offs = run
        last = n_cols // SBA_SUB - 1
        for c in range(last, -1, -1):
            cols = slice(c * SBA_SUB, (c + 1) * SBA_SUB)
            masked = diagonal and c == last
            dc = drop[:, cols]
            if masked:
                dc = jnp.where(below_diag, dc, 0.0)
            hi = dc.astype(bf16)
            lo = (dc - hi.astype(f32)).astype(bf16)
            sums = _dot(jnp.concatenate([hi, lo], axis=-1), suffix_w)
            a = jnp.exp2(log_beta[:, cols] - sums[:, :SBA_SUB] - offs)
            if masked:
                a = jnp.where(below_diag, a, 0.0)
            probs.append(a.astype(bf16))
            offs = offs + sums[:, SBA_SUB:]
        a_blk = probs[0] if last == 0 else jnp.concatenate(probs[::-1], axis=-1)
        pv = jnp.concatenate(
            [_dot(a_blk[j * SBA_TQ:(j + 1) * SBA_TQ], v_ref[pl.ds(k0, n_cols), head_cols[j]])
             for j in range(HEADS_PER_STEP)], axis=0)
        return acc + pv, offs

    def key_block_row(s, has_left):
        k_diag = pl.multiple_of(s * SBA_TK, SBA_TK)
        n_q = SBA_TK // SBA_TQ
        q0s = [pl.multiple_of(k_diag + p * SBA_TQ, SBA_TQ) for p in range(n_q)]
        qbss = [[qn_ref[pl.ds(q0, SBA_TQ), hc] for hc in head_cols] for q0 in q0s]
        zeros = jnp.zeros((rows_all, HEAD_DIM), f32)
        states = [block(qbss[p], k_diag, (p + 1) * SBA_SUB, True, (zeros, zeros)) for p in range(n_q)]
        taken = [0] * n_q
        if has_left:
            k_first = pl.multiple_of(k_diag - SBA_LEFT_TK, SBA_LEFT_TK)
            for p in range(SBA_EAGER_LEFT):
                states[p] = block(qbss[p], k_first, SBA_LEFT_TK, False, states[p])
                taken[p] = 1
        for p in range(n_q):
            q0, qbs, state = q0s[p], qbss[p], states[p]

            n_left = s * (SBA_TK // SBA_LEFT_TK)

            def more(c):
                it, _, run = c
                return (it < n_left) & (jnp.min(run) <= UNDERFLOW_LOG2)

            def left_block(c):
                it, acc, run = c
                k0 = pl.multiple_of((n_left - 1 - it) * SBA_LEFT_TK, SBA_LEFT_TK)
                acc, run = block(qbs, k0, SBA_LEFT_TK, False, (acc, run))
                return it + 1, acc, run

            _, acc, _ = lax.while_loop(more, left_block, (jnp.int32(taken[p]),) + state)
            for j in range(HEADS_PER_STEP):
                o_ref[pl.ds(q0, SBA_TQ), head_cols[j]] = acc[j * SBA_TQ:(j + 1) * SBA_TQ].astype(o_ref.dtype)

    key_block_row(jnp.int32(0), False)

    def later_row(s, carry):
        key_block_row(s, True)
        return carry

    lax.fori_loop(1, t // SBA_TK, later_row, 0)


def _sba(z, q_gain, k_gain, batch, seq, l):
    m = z.shape[0]
    c0 = COL_SBA_QKV // STEP_WIDTH
    per_seg = WIDTH // STEP_WIDTH

    def head_spec(seg):
        return pl.BlockSpec((seq, STEP_WIDTH), lambda b, h: (b, c0 + seg * per_seg + h))

    gain_spec = pl.BlockSpec((None, 1, HEAD_DIM), lambda b, h: (l, 0, 0))
    tok = pltpu.VMEM((seq, STEP_WIDTH), bf16)
    return pl.pallas_call(
        _sba_kernel,
        grid=(batch, per_seg),
        in_specs=[head_spec(0), head_spec(1), head_spec(2), gain_spec, gain_spec],
        out_specs=pl.BlockSpec((seq, STEP_WIDTH), lambda b, h: (b, h)),
        out_shape=jax.ShapeDtypeStruct((m, WIDTH), bf16),
        scratch_shapes=[tok, tok],
        compiler_params=_params("arbitrary", "arbitrary"),
        name="sba",
    )(z, z, z, q_gain, k_gain)


XPAD = BF16_ROWS


def _bmm(a, b):
    return jnp.einsum('bij,bjk->bik', a, b, preferred_element_type=f32)


def _bmm_nt(a, b):
    return jnp.einsum('bid,bjd->bij', a, b, preferred_element_type=f32)


def _bmm_tn(a, b):
    return jnp.einsum('bck,bcv->bkv', a, b, preferred_element_type=f32)


def _unit_lower_inverse(l_mat, row, col):
    def same_block(d):
        return _block_id(row, d) == _block_id(col, d)

    eye = (row == col).astype(f32)
    l8 = jnp.where(same_block(8), l_mat, 0.0)
    p = l8.astype(bf16)
    x = (eye - l8).astype(bf16)
    p2 = _bmm(p, p)
    x = _bmm(x, (eye + p2).astype(bf16)).astype(bf16)
    p2 = p2.astype(bf16)
    p4 = _bmm(p2, p2)
    x = _bmm(x, (eye + p4).astype(bf16)).astype(bf16)
    d = 8
    while d < CHUNK:
        off = jnp.where(same_block(2 * d) & jnp.logical_not(same_block(d)), l_mat, 0.0).astype(bf16)
        x = _bmm(x, (eye - _bmm(off, x)).astype(bf16)).astype(bf16)
        d *= 2
    return x


SCAN_ROWS = HEAD_DIM + CHUNK


def _gdn_kernel(zq_ref, zk_ref, zv_ref, zg_ref, gb_ref, cwq_ref, cwk_ref, cwv_ref, ng_ref, o_ref,
                xq_ref, xk_ref, xv_ref, stage_ref, lin_ref, off_ref, eg_ref):
    t = zq_ref.shape[0]
    head0 = pl.program_id(1) * HEADS_PER_STEP
    grp = GDN_GROUP
    nc = grp // CHUNK
    head_cols = [slice(j * HEAD_DIM, (j + 1) * HEAD_DIM) for j in range(HEADS_PER_STEP)]

    zeros_pad = jnp.zeros((XPAD, STEP_WIDTH), bf16)
    for src, dst in ((zq_ref, xq_ref), (zk_ref, xk_ref), (zv_ref, xv_ref)):
        dst[pl.ds(0, XPAD), :] = zeros_pad
        dst[pl.ds(XPAD, t), :] = src[...]

    row = lax.broadcasted_iota(jnp.int32, (CHUNK, CHUNK), 0)
    col = lax.broadcasted_iota(jnp.int32, (CHUNK, CHUNK), 1)
    incl = row >= col
    strict = row > col
    tri_ones = jnp.concatenate([incl.astype(bf16), jnp.ones((CHUNK, CHUNK), bf16)], axis=0)
    tri_ones = jnp.broadcast_to(tri_ones, (nc, 2 * CHUNK, CHUNK))
    sel_src = jnp.bitwise_and(lax.broadcasted_iota(jnp.int32, (3 * LANES, STEP_WIDTH), 0), LANES - 1)
    sel_head = head0 + _block_id(lax.broadcasted_iota(jnp.int32, (3 * LANES, STEP_WIDTH), 1), HEAD_DIM)
    pick_g = (sel_src == sel_head).astype(bf16)
    pick_b = (sel_src == sel_head + N_HEADS).astype(bf16)
    sel_rows = (jnp.bitwise_and(lax.broadcasted_iota(jnp.int32, (SUBLANES, 3 * LANES), 1), LANES - 1)
                == head0 + lax.broadcasted_iota(jnp.int32, (SUBLANES, 3 * LANES), 0)).astype(bf16)
    sel_rows = jnp.broadcast_to(sel_rows, (nc, SUBLANES, 3 * LANES))

    def conv_silu(x_ref, cw_ref, r0, hc, slot):
        stage = stage_ref.at[slot]
        stage[...] = x_ref[pl.ds(r0, grp + XPAD), hc].astype(f32)
        acc = cw_ref[pl.ds(CONV_WIDTH - 1, 1), hc] * stage[pl.ds(XPAD, grp), :]
        for i in range(CONV_WIDTH - 1):
            acc += cw_ref[pl.ds(i, 1), hc] * stage[pl.ds(XPAD - (CONV_WIDTH - 1) + i, grp), :]
        return _silu(acc)

    def l2n(x):
        return x * lax.rsqrt(jnp.sum(x * x, axis=-1, keepdims=True) + EPS)

    def chunked(x):
        return x.reshape(nc, CHUNK, x.shape[-1])

    def slot_of(g):
        return jnp.bitwise_and(g, 1)

    def prepare(it, carry):
        r0 = pl.multiple_of(it * grp, grp)
        gb3 = _split3(gb_ref[pl.ds(r0, grp), :])
        ct = _bmm(tri_ones, chunked(gb3))
        cum3 = _split3(_sum3(ct[:, :CHUNK], LANES).reshape(grp, LANES))
        tot3 = _split3(_sum3(ct[:, CHUNK:], LANES).reshape(grp, LANES))
        gc_rows = _bmm_nt(sel_rows, chunked(cum3))

        gc_all = _dot(cum3, pick_g)
        g_tot_all = _dot(tot3, pick_g)
        beta_all = _dot(gb3, pick_b)

        k16s, kb16s, q16s, rhs16s, diffs, qds, kd16s = [], [], [], [], [], [], []
        for j, hc in enumerate(head_cols):
            gc, g_tot, beta = gc_all[:, hc], g_tot_all[:, hc], beta_all[:, hc]
            q = l2n(conv_silu(xq_ref, cwq_ref, r0, hc, 3 * j)) * np.float32(HEAD_DIM ** -0.5)
            k = l2n(conv_silu(xk_ref, cwk_ref, r0, hc, 3 * j + 1))
            v = conv_silu(xv_ref, cwv_ref, r0, hc, 3 * j + 2)
            kb = k * beta
            e_gc = jnp.exp(gc)
            k16s.append(chunked(k.astype(bf16)))
            kb16s.append(chunked(kb.astype(bf16)))
            q16s.append(chunked(q.astype(bf16)))
            rhs16s.append(chunked(jnp.concatenate([v * beta, kb * e_gc], axis=-1).astype(bf16)))
            diffs.append(chunked(gc)[:, :, :CHUNK] - gc_rows[:, j:j + 1, :])
            qds.append(chunked(q * e_gc))
            kd16s.append(chunked((k * jnp.exp(g_tot - gc)).astype(bf16)))
            e0 = pl.multiple_of(slot_of(it) * (nc * SUBLANES), nc * SUBLANES)
            eg_ref[pl.ds(e0, nc * SUBLANES), hc] = (
                chunked(jnp.exp(g_tot))[:, :SUBLANES, :].reshape(nc * SUBLANES, HEAD_DIM))

        cat = lambda xs: jnp.concatenate(xs, axis=0)
        k16, diff = cat(k16s), cat(diffs)
        decay = jnp.where(incl, jnp.exp(jnp.where(incl, diff, 0.0)), 0.0)
        l_mat = jnp.where(strict, _bmm_nt(cat(kb16s), k16) * decay, 0.0)
        intra = (_bmm_nt(cat(q16s), k16) * decay).astype(bf16)
        t_inv = _unit_lower_inverse(l_mat, row, col)
        uw = _bmm(t_inv, cat(rhs16s)).astype(bf16)
        state_map = _bmm_tn(cat(kd16s), uw)
        out_map = _bmm(intra, uw)
        q_eff = cat(qds) - out_map[:, :, HEAD_DIM:]
        lin = jnp.concatenate([state_map[:, :, HEAD_DIM:], q_eff], axis=1).astype(bf16)
        off = jnp.concatenate([state_map[:, :, :HEAD_DIM], out_map[:, :, :HEAD_DIM]], axis=1)
        s0 = pl.multiple_of(slot_of(it) * (nc * SCAN_ROWS), nc * SCAN_ROWS)
        for j, hc in enumerate(head_cols):
            mine = slice(j * nc, (j + 1) * nc)
            lin_ref[pl.ds(s0, nc * SCAN_ROWS), hc] = lin[mine].reshape(nc * SCAN_ROWS, HEAD_DIM)
            off_ref[pl.ds(s0, nc * SCAN_ROWS), hc] = off[mine].reshape(nc * SCAN_ROWS, HEAD_DIM)
        return carry

    def scan(slot_chunk, out_chunk, states):
        s0 = pl.multiple_of(slot_chunk * SCAN_ROWS, SCAN_ROWS)
        e0 = pl.multiple_of(slot_chunk * SUBLANES, SUBLANES)
        r0 = pl.multiple_of(out_chunk * CHUNK, CHUNK)
        new_states = []
        for j, hc in enumerate(head_cols):
            state = states[j]
            prod = _dot(lin_ref[pl.ds(s0, SCAN_ROWS), hc], state.astype(bf16))
            off = off_ref[pl.ds(s0, SCAN_ROWS), hc]
            new_states.append(state * eg_ref[pl.ds(e0, 1), hc] + (off[:HEAD_DIM] - prod[:HEAD_DIM]))
            o = prod[HEAD_DIM:] + off[HEAD_DIM:]
            gate = zg_ref[pl.ds(r0, CHUNK), hc].astype(f32)
            o_ref[pl.ds(r0, CHUNK), hc] = (_rms_rows(o, ng_ref[...]) * _silu(gate)).astype(o_ref.dtype)
        return tuple(new_states)

    def scan_group(slot, out_group, states):
        for c in range(nc):
            states = scan(slot * nc + c, out_group * nc + c, states)
        return states

    def step(g, states):
        states = scan_group(slot_of(g + 1), jnp.maximum(g - 1, 0), states)
        prepare(g, 0)
        return states

    n_groups = t // grp
    lin_ref[pl.ds(nc * SCAN_ROWS, nc * SCAN_ROWS), :] = jnp.zeros((nc * SCAN_ROWS, STEP_WIDTH), bf16)
    off_ref[pl.ds(nc * SCAN_ROWS, nc * SCAN_ROWS), :] = jnp.zeros((nc * SCAN_ROWS, STEP_WIDTH), f32)
    eg_ref[pl.ds(nc * SUBLANES, nc * SUBLANES), :] = jnp.zeros((nc * SUBLANES, STEP_WIDTH), f32)
    zero_state = jnp.zeros((HEAD_DIM, HEAD_DIM), f32)
    states = lax.fori_loop(0, n_groups, step, (zero_state,) * HEADS_PER_STEP)
    scan_group(jnp.int32((n_groups - 1) % 2), jnp.int32(n_groups - 1), states)


def _gdn(z, gb, conv_w, norm_gain, batch, seq, l):
    m = z.shape[0]
    per_seg = WIDTH // STEP_WIDTH
    cq = COL_GDN_QKV // STEP_WIDTH
    cg = COL_GDN_GATE // STEP_WIDTH

    def head_spec(c):
        return pl.BlockSpec((seq, STEP_WIDTH), lambda b, h: (b, c + h))

    def conv_spec(seg):
        return pl.BlockSpec((None, CONV_WIDTH, STEP_WIDTH), lambda b, h: (l, 0, seg * per_seg + h))

    ring_chunks = 2 * GDN_GROUP // CHUNK
    padded = pltpu.VMEM((seq + XPAD, STEP_WIDTH), bf16)
    return pl.pallas_call(
        _gdn_kernel,
        grid=(batch, per_seg),
        in_specs=[
            head_spec(cq), head_spec(cq + per_seg), head_spec(cq + 2 * per_seg), head_spec(cg),
            pl.BlockSpec((seq, LANES), lambda b, h: (b, 0)),
            conv_spec(0), conv_spec(1), conv_spec(2),
            pl.BlockSpec((None, 1, HEAD_DIM), lambda b, h: (l, 0, 0)),
        ],
        out_specs=pl.BlockSpec((seq, STEP_WIDTH), lambda b, h: (b, h)),
        out_shape=jax.ShapeDtypeStruct((m, WIDTH), bf16),
        scratch_shapes=[
            padded, padded, padded,
            pltpu.VMEM((3 * HEADS_PER_STEP, GDN_GROUP + XPAD, HEAD_DIM), f32),
            pltpu.VMEM((ring_chunks * SCAN_ROWS, STEP_WIDTH), bf16),
            pltpu.VMEM((ring_chunks * SCAN_ROWS, STEP_WIDTH), f32),
            pltpu.VMEM((ring_chunks * SUBLANES, STEP_WIDTH), f32),
        ],
        compiler_params=_params("arbitrary", "arbitrary"),
        name="gdn",
    )(z, z, z, z, gb, conv_w, conv_w, conv_w, norm_gain)


def _rows(v):
    return v.astype(f32)[:, None, :]


def _pad_rows(v):
    return _rows(jnp.pad(v, ((0, 0), (0, LANES - v.shape[1]))))


def kernel(x, w_in, conv_w, a_log, dt_bias, gdn_norm_g, gmlp_ln_g, w_spatial, b_spatial, sba_q_g, sba_k_g, w_out_a, w_out_b, w_out_c, w_out, norm_mix_g, norm_mlp_g, w_ff1, w_ff2):
    batch, seq, d = x.shape
    depth = w_in.shape[0]
    m = batch * seq

    wt = jnp.swapaxes(w_in, 1, 2).reshape(depth * D_IN, d)
    b_rep = jnp.broadcast_to(b_spatial[..., None], b_spatial.shape + (LANES,)).astype(f32)
    mix_g, mlp_g, gdn_g, ln_g = _rows(norm_mix_g), _rows(norm_mlp_g), _rows(gdn_norm_g), _rows(gmlp_ln_g)
    q_g, k_g = _rows(sba_q_g), _rows(sba_k_g)
    alog_rows, dt_rows = _pad_rows(a_log), _pad_rows(dt_bias)
    conv_w = conv_w.astype(f32)
    w_spatial = w_spatial.astype(f32)

    xf = x.reshape(m, d).astype(f32)
    for l in range(depth):
        z, gb = _inproj(xf, mix_g, wt, alog_rows, dt_rows, l)
        oa = _gdn(z, gb, conv_w, gdn_g, batch, seq, l)
        ob = _gmlp(z, ln_g, w_spatial, b_rep, l)
        oc = _sba(z, q_g, k_g, batch, seq, l)
        y = _merge(oa, ob, oc, w_out_a, w_out_b, w_out_c, z, l)
        xf = _matmul_res(y, w_out, xf, l)
        h1 = _ffn_up(xf, mlp_g, w_ff1, l)
        xf = _matmul_res(h1, w_ff2, xf, l)
    return xf.reshape(batch, seq, d).astype(x.dtype)
```

```python
import jax
import jax.numpy as jnp
import numpy as np
from jax import lax
from jax.experimental import pallas as pl
from jax.experimental.pallas import tpu as pltpu

f32 = jnp.float32
bf16 = jnp.bfloat16

D_MODEL = 2048
CHUNK = 64
HEAD_DIM = 128
N_HEADS = 8
WIDTH = N_HEADS * HEAD_DIM
CONV_WIDTH = 4
GMLP_WIDTH = 1024
GMLP_GROUPS = 8
GMLP_BLOCK = 128
N_BRANCHES = 3
EPS = 1e-6
LOG2_E = float(np.log2(np.e))
UNDERFLOW_LOG2 = 160.0

LANES = 128
SUBLANES = 8
BF16_ROWS = 16
VMEM_LIMIT_BYTES = 60000 * 1024

COL_GDN_QKV = 0
COL_GDN_GATE = 3 * WIDTH
COL_GMLP_UV = COL_GDN_GATE + WIDTH
COL_SBA_QKV = COL_GMLP_UV + 2 * GMLP_WIDTH
COL_GATES = COL_SBA_QKV + 3 * WIDTH
N_MAIN = COL_GATES + N_BRANCHES * D_MODEL
ORIG_COL_AB = 3 * WIDTH
N_AB = 2 * N_HEADS
D_IN = N_MAIN + N_AB

PROJ_TM = 2048
PROJ_TN = 768
FFN_UP_TN = 1024
RES_TM = 2048
RES_TN = 1024
RES_TK = 1024
MERGE_TM = 512

HEADS_PER_STEP = 4
STEP_WIDTH = HEADS_PER_STEP * HEAD_DIM
GDN_GROUP = 512
SBA_TQ = 128
SBA_TK = 512
SBA_LEFT_TK = 256
SBA_EAGER_LEFT = 2
SBA_SUB = 128
GMLP_ROWS = 512


def _params(*sem):
    return pltpu.CompilerParams(dimension_semantics=sem, vmem_limit_bytes=VMEM_LIMIT_BYTES)


def _dot(a, b):
    return jnp.dot(a, b, preferred_element_type=f32)


def _dot_nt(a, b):
    return lax.dot_general(a, b, (((1,), (1,)), ((), ())), preferred_element_type=f32)


def _sigmoid(x):
    return lax.logistic(x)


def _silu(x):
    return x * _sigmoid(x)


def _softplus(x):
    return jnp.maximum(x, 0.0) + jnp.log1p(jnp.exp(-jnp.abs(x)))


def _block_id(idx, size):
    return lax.shift_right_logical(idx, int(size).bit_length() - 1)


def _split3(x):
    hi = x.astype(bf16)
    r1 = x - hi.astype(f32)
    mid = r1.astype(bf16)
    lo = (r1 - mid.astype(f32)).astype(bf16)
    return jnp.concatenate([hi, mid, lo], axis=-1)


def _sum3(x, n):
    return x[..., :n] + x[..., n:2 * n] + x[..., 2 * n:3 * n]


def _rms_rows(x, gain):
    ms = jnp.mean(x * x, axis=-1, keepdims=True)
    return x * lax.rsqrt(ms + EPS) * gain


NORM_ROWS = 256


N_NORM_CHUNKS = PROJ_TM // NORM_ROWS


def _norm_scratch():
    return [pltpu.VMEM((2, PROJ_TM, D_MODEL), bf16),
            pltpu.VMEM((2, NORM_ROWS, D_MODEL), f32),
            pltpu.SemaphoreType.DMA((2,))]


def _stream_row_chunks(x_hbm, row0, n_rows, buf, sem, body):
    n = n_rows // NORM_ROWS

    def copy(c, slot):
        src = x_hbm.at[pl.ds(pl.multiple_of(row0 + c * NORM_ROWS, NORM_ROWS), NORM_ROWS), :]
        return pltpu.make_async_copy(src, buf.at[slot], sem.at[slot])

    copy(0, 0).start()

    def step(c, carry):
        slot = jnp.bitwise_and(c, 1)

        @pl.when(c + 1 < n)
        def _():
            copy(c + 1, 1 - slot).start()

        copy(c, slot).wait()
        body(pl.ds(pl.multiple_of(c * NORM_ROWS, NORM_ROWS), NORM_ROWS), buf[slot])
        return carry

    lax.fori_loop(0, n, step, 0)


def _projection_step(x_hbm, xbuf, xsem, normalise, project):
    i, j = pl.program_id(0), pl.program_id(1)
    cur = jnp.bitwise_and(i, 1)

    @pl.when((i == 0) & (j == 0))
    def _():
        _stream_row_chunks(x_hbm, 0, PROJ_TM, xbuf, xsem, lambda rows, x: normalise(0, rows, x))

    ahead = (j < N_NORM_CHUNKS) & (i + 1 < pl.num_programs(0))

    def ahead_copy():
        start = pl.multiple_of((i + 1) * PROJ_TM + j * NORM_ROWS, NORM_ROWS)
        return pltpu.make_async_copy(x_hbm.at[pl.ds(start, NORM_ROWS), :], xbuf.at[0], xsem.at[0])

    @pl.when(ahead)
    def _():
        ahead_copy().start()

    project(cur)

    @pl.when(ahead)
    def _():
        ahead_copy().wait()
        normalise(1 - cur, pl.ds(pl.multiple_of(j * NORM_ROWS, NORM_ROWS), NORM_ROWS), xbuf[0])


def _inproj_kernel(x_hbm, g_ref, wt_ref, wabt_ref, alog_ref, dt_ref, z_ref, gb_ref, hn_ref, xbuf, xsem, gates_ref):
    def normalise(slot, rows, x):
        hn = _rms_rows(x, g_ref[...]).astype(bf16)
        hn_ref[slot, rows, :] = hn
        ab = _dot_nt(hn, wabt_ref[...].astype(bf16))
        lane = lax.broadcasted_iota(jnp.int32, ab.shape, 1)
        g = -jnp.exp(alog_ref[...]) * _softplus(ab + dt_ref[...])
        gates_ref[slot, rows, :] = jnp.where(lane < N_HEADS, g, _sigmoid(ab))

    def project(slot):
        @pl.when(pl.program_id(1) == 0)
        def _():
            gb_ref[...] = gates_ref[slot]

        z_ref[...] = _dot_nt(hn_ref[slot], wt_ref[...].astype(bf16)).astype(z_ref.dtype)

    _projection_step(x_hbm, xbuf, xsem, normalise, project)


def _inproj(x, gain, wt, alog_row, dt_row, l):
    m = x.shape[0]
    assert N_MAIN // PROJ_TN >= N_NORM_CHUNKS
    n_direct = ORIG_COL_AB // PROJ_TN

    def w_rows(i, j):
        return pl.multiple_of(l * D_IN + j * PROJ_TN + jnp.where(j >= n_direct, N_AB, 0), SUBLANES), 0

    return pl.pallas_call(
        _inproj_kernel,
        grid=(m // PROJ_TM, N_MAIN // PROJ_TN),
        in_specs=[
            pl.BlockSpec(memory_space=pl.ANY),
            pl.BlockSpec((None, 1, D_MODEL), lambda i, j: (l, 0, 0)),
            pl.BlockSpec((pl.Element(PROJ_TN), pl.Element(D_MODEL)), w_rows),
            pl.BlockSpec((pl.Element(LANES), pl.Element(D_MODEL)), lambda i, j: (l * D_IN + ORIG_COL_AB, 0),
                         pipeline_mode=pl.Buffered(1)),
            pl.BlockSpec((None, 1, LANES), lambda i, j: (l, 0, 0)),
            pl.BlockSpec((None, 1, LANES), lambda i, j: (l, 0, 0)),
        ],
        out_specs=[
            pl.BlockSpec((PROJ_TM, PROJ_TN), lambda i, j: (i, j)),
            pl.BlockSpec((PROJ_TM, LANES), lambda i, j: (i, 0)),
        ],
        out_shape=[
            jax.ShapeDtypeStruct((m, N_MAIN), bf16),
            jax.ShapeDtypeStruct((m, LANES), f32),
        ],
        scratch_shapes=_norm_scratch() + [pltpu.VMEM((2, PROJ_TM, LANES), f32)],
        compiler_params=_params("arbitrary", "arbitrary"),
        name="inproj",
    )(x, gain, wt, wt, alog_row, dt_row)


def _ffn_up_kernel(x_hbm, g_ref, w_ref, h_ref, hn_ref, xbuf, xsem):
    def normalise(slot, rows, x):
        hn_ref[slot, rows, :] = _rms_rows(x, g_ref[...]).astype(bf16)

    def project(slot):
        a = jnp.maximum(_dot(hn_ref[slot], w_ref[...].astype(bf16)), 0.0)
        h_ref[...] = (a * a).astype(h_ref.dtype)

    _projection_step(x_hbm, xbuf, xsem, normalise, project)


def _ffn_up(x, gain, w1, l):
    m, n = x.shape[0], w1.shape[-1]
    assert n // FFN_UP_TN >= N_NORM_CHUNKS
    return pl.pallas_call(
        _ffn_up_kernel,
        grid=(m // PROJ_TM, n // FFN_UP_TN),
        in_specs=[
            pl.BlockSpec(memory_space=pl.ANY),
            pl.BlockSpec((None, 1, D_MODEL), lambda i, j: (l, 0, 0)),
            pl.BlockSpec((None, D_MODEL, FFN_UP_TN), lambda i, j: (l, 0, j)),
        ],
        out_specs=pl.BlockSpec((PROJ_TM, FFN_UP_TN), lambda i, j: (i, j)),
        out_shape=jax.ShapeDtypeStruct((m, n), bf16),
        scratch_shapes=_norm_scratch(),
        compiler_params=_params("arbitrary", "arbitrary"),
        name="ffn_up",
    )(x, gain, w1)


def _matmul_res_kernel(a_ref, w_ref, r_ref, o_ref):
    d = _dot(a_ref[...], w_ref[...].astype(bf16))

    @pl.when(pl.program_id(2) == 0)
    def _():
        o_ref[...] = r_ref[...] + d

    @pl.when(pl.program_id(2) != 0)
    def _():
        o_ref[...] += d


def _matmul_res(a, w, res, l):
    m, k = a.shape
    n = w.shape[-1]
    tm, tk = (RES_TM // 2, 2 * RES_TK) if k > D_MODEL else (RES_TM, RES_TK)
    return pl.pallas_call(
        _matmul_res_kernel,
        grid=(m // tm, n // RES_TN, k // tk),
        in_specs=[
            pl.BlockSpec((tm, tk), lambda i, j, kk: (i, kk)),
            pl.BlockSpec((None, tk, RES_TN), lambda i, j, kk: (l, kk, j)),
            pl.BlockSpec((tm, RES_TN), lambda i, j, kk: (i, j)),
        ],
        out_specs=pl.BlockSpec((tm, RES_TN), lambda i, j, kk: (i, j)),
        out_shape=jax.ShapeDtypeStruct((m, n), f32),
        compiler_params=_params("arbitrary", "arbitrary", "arbitrary"),
        name="matmul_res",
    )(a, w, res)


MERGE_GATE_COLS = 1024
MERGE_GATE_BLOCKS = D_MODEL // MERGE_GATE_COLS


def _merge_kernel(oa_ref, ob_ref, oc_ref, wa_ref, wb_ref, wc_ref, *rest):
    gate_refs, y_ref = rest[:-1], rest[-1]
    branches = ((oa_ref, wa_ref), (ob_ref, wb_ref), (oc_ref, wc_ref))
    for c in range(MERGE_GATE_BLOCKS):
        cols = slice(c * MERGE_GATE_COLS, (c + 1) * MERGE_GATE_COLS)
        y = None
        for b, (o_ref, w_ref) in enumerate(branches):
            gate = _sigmoid(gate_refs[b * MERGE_GATE_BLOCKS + c][...].astype(f32))
            term = gate * _dot(o_ref[...], w_ref[:, cols].astype(bf16))
            y = term if y is None else y + term
        y_ref[:, cols] = y.astype(y_ref.dtype)


def _merge(oa, ob, oc, wa, wb, wc, z, l):
    m = oa.shape[0]
    gate0 = COL_GATES // MERGE_GATE_COLS
    branch_in = pl.BlockSpec((MERGE_TM, WIDTH), lambda i: (i, 0))
    branch_w = pl.BlockSpec((None, WIDTH, D_MODEL), lambda i: (l, 0, 0), pipeline_mode=pl.Buffered(1))
    gate_specs = [pl.BlockSpec((MERGE_TM, MERGE_GATE_COLS), lambda i, blk=gate0 + g: (i, blk))
                  for g in range(N_BRANCHES * MERGE_GATE_BLOCKS)]
    return pl.pallas_call(
        _merge_kernel,
        grid=(m // MERGE_TM,),
        in_specs=[branch_in, branch_in, branch_in, branch_w, branch_w, branch_w] + gate_specs,
        out_specs=pl.BlockSpec((MERGE_TM, D_MODEL), lambda i: (i, 0)),
        out_shape=jax.ShapeDtypeStruct((m, D_MODEL), bf16),
        compiler_params=_params("arbitrary"),
        name="merge",
    )(oa, ob, oc, wa, wb, wc, *([z] * (N_BRANCHES * MERGE_GATE_BLOCKS)))


def _gelu(x):
    return 0.5 * x * (1.0 + lax.erf(x * np.float32(np.sqrt(0.5))))


def _gmlp_kernel(zu_ref, zv_ref, g_ref, ws_ref, bs_ref, o_ref):
    u = _gelu(zu_ref[...].astype(f32))
    v = _gelu(zv_ref[...].astype(f32))
    mu = jnp.mean(v, axis=-1, keepdims=True)
    vc = v - mu
    var = jnp.mean(vc * vc, axis=-1, keepdims=True)
    vb = (vc * lax.rsqrt(var + EPS) * g_ref[...]).astype(bf16)

    t_chunk = _block_id(lax.broadcasted_iota(jnp.int32, (GMLP_BLOCK, GMLP_BLOCK), 0), CHUNK)
    s_chunk = _block_id(lax.broadcasted_iota(jnp.int32, (GMLP_BLOCK, GMLP_BLOCK), 1), CHUNK)
    causal = s_chunk <= t_chunk
    gdim = GMLP_WIDTH // GMLP_GROUPS
    for g in range(GMLP_GROUPS):
        ws = jnp.where(causal, ws_ref[g], 0.0).astype(bf16)
        cols = slice(g * gdim, (g + 1) * gdim)
        for blk in range(GMLP_ROWS // GMLP_BLOCK):
            rows = slice(blk * GMLP_BLOCK, (blk + 1) * GMLP_BLOCK)
            s = _dot(ws, vb[rows, cols]) + bs_ref[g]
            o_ref[rows, cols] = (u[rows, cols] * s).astype(o_ref.dtype)


def _gmlp(z, ln_gain, w_spatial, b_rep, l):
    m = z.shape[0]
    cu = COL_GMLP_UV // GMLP_WIDTH
    return pl.pallas_call(
        _gmlp_kernel,
        grid=(m // GMLP_ROWS,),
        in_specs=[
            pl.BlockSpec((GMLP_ROWS, GMLP_WIDTH), lambda i: (i, cu)),
            pl.BlockSpec((GMLP_ROWS, GMLP_WIDTH), lambda i: (i, cu + 1)),
            pl.BlockSpec((None, 1, GMLP_WIDTH), lambda i: (l, 0, 0)),
            pl.BlockSpec((None, GMLP_GROUPS, GMLP_BLOCK, GMLP_BLOCK), lambda i: (l, 0, 0, 0)),
            pl.BlockSpec((None, GMLP_GROUPS, GMLP_BLOCK, LANES), lambda i: (l, 0, 0, 0)),
        ],
        out_specs=pl.BlockSpec((GMLP_ROWS, GMLP_WIDTH), lambda i: (i, 0)),
        out_shape=jax.ShapeDtypeStruct((m, GMLP_WIDTH), bf16),
        compiler_params=_params("arbitrary"),
        name="gmlp",
    )(z, z, ln_gain, w_spatial, b_rep)


def _sba_kernel(q_ref, k_ref, v_ref, qg_ref, kg_ref, o_ref, qn_ref, kn_ref):
    t = q_ref.shape[0]
    head_cols = [slice(j * HEAD_DIM, (j + 1) * HEAD_DIM) for j in range(HEADS_PER_STEP)]
    for hc in head_cols:
        qn_ref[:, hc] = _rms_rows(q_ref[:, hc].astype(f32), qg_ref[...]).astype(bf16)
        kn_ref[:, hc] = _rms_rows(k_ref[:, hc].astype(f32), kg_ref[...]).astype(bf16)
    to_log2 = np.float32(HEAD_DIM ** -0.5 * LOG2_E)
    rows_all = HEADS_PER_STEP * SBA_TQ
    wr = lax.broadcasted_iota(jnp.int32, (2 * SBA_SUB, 2 * SBA_SUB), 0)
    wc = lax.broadcasted_iota(jnp.int32, (2 * SBA_SUB, 2 * SBA_SUB), 1)
    suffix_w = ((jnp.bitwise_and(wr, SBA_SUB - 1) > wc) | (wc >= SBA_SUB)).astype(bf16)

    below_diag = (lax.broadcasted_iota(jnp.int32, (rows_all, SBA_SUB), 1)
                  < jnp.bitwise_and(lax.broadcasted_iota(jnp.int32, (rows_all, SBA_SUB), 0), SBA_TQ - 1))

    def block(qbs, k0, n_cols, diagonal, state):
        acc, run = state
        z = jnp.concatenate([_dot_nt(qbs[j], kn_ref[pl.ds(k0, n_cols), head_cols[j]])
                             for j in range(HEADS_PER_STEP)], axis=0) * to_log2
        neg_abs = pltpu.bitcast(pltpu.bitcast(z, jnp.uint32) | jnp.uint32(0x80000000), f32)
        soft = jnp.log(1.0 + jnp.exp2(neg_abs)) * np.float32(LOG2_E)
        drop = jnp.maximum(z, 0.0) + soft
        log_beta = z - drop
        probs = []
        offs = run
        last = n_cols // SBA_SUB - 1
        for c in range(last, -1, -1):
            cols = slice(c * SBA_SUB, (c + 1) * SBA_SUB)
            masked = diagonal and c == last
            dc = drop[:, cols]
            if masked:
                dc = jnp.where(below_diag, dc, 0.0)
            hi = dc.astype(bf16)
            lo = (dc - hi.astype(f32)).astype(bf16)
            sums = _dot(jnp.concatenate([hi, lo], axis=-1), suffix_w)
            a = jnp.exp2(log_beta[:, cols] - sums[:, :SBA_SUB] - offs)
            if masked:
                a = jnp.where(below_diag, a, 0.0)
            probs.append(a.astype(bf16))
            offs = offs + sums[:, SBA_SUB:]
        a_blk = probs[0] if last == 0 else jnp.concatenate(probs[::-1], axis=-1)
        pv = jnp.concatenate(
            [_dot(a_blk[j * SBA_TQ:(j + 1) * SBA_TQ], v_ref[pl.ds(k0, n_cols), head_cols[j]])
             for j in range(HEADS_PER_STEP)], axis=0)
        return acc + pv, offs

    def key_block_row(s, has_left):
        k_diag = pl.multiple_of(s * SBA_TK, SBA_TK)
        n_q = SBA_TK // SBA_TQ
        q0s = [pl.multiple_of(k_diag + p * SBA_TQ, SBA_TQ) for p in range(n_q)]
        qbss = [[qn_ref[pl.ds(q0, SBA_TQ), hc] for hc in head_cols] for q0 in q0s]
        zeros = jnp.zeros((rows_all, HEAD_DIM), f32)
        states = [block(qbss[p], k_diag, (p + 1) * SBA_SUB, True, (zeros, zeros)) for p in range(n_q)]
        taken = [0] * n_q
        if has_left:
            k_first = pl.multiple_of(k_diag - SBA_LEFT_TK, SBA_LEFT_TK)
            for p in range(SBA_EAGER_LEFT):
                states[p] = block(qbss[p], k_first, SBA_LEFT_TK, False, states[p])
                taken[p] = 1
        for p in range(n_q):
            q0, qbs, state = q0s[p], qbss[p], states[p]

            n_left = s * (SBA_TK // SBA_LEFT_TK)

            def more(c):
                it, _, run = c
                return (it < n_left) & (jnp.min(run) <= UNDERFLOW_LOG2)

            def left_block(c):
                it, acc, run = c
                k0 = pl.multiple_of((n_left - 1 - it) * SBA_LEFT_TK, SBA_LEFT_TK)
                acc, run = block(qbs, k0, SBA_LEFT_TK, False, (acc, run))
                return it + 1, acc, run

            _, acc, _ = lax.while_loop(more, left_block, (jnp.int32(taken[p]),) + state)
            for j in range(HEADS_PER_STEP):
                o_ref[pl.ds(q0, SBA_TQ), head_cols[j]] = acc[j * SBA_TQ:(j + 1) * SBA_TQ].astype(o_ref.dtype)

    key_block_row(jnp.int32(0), False)

    def later_row(s, carry):
        key_block_row(s, True)
        return carry

    lax.fori_loop(1, t // SBA_TK, later_row, 0)


def _sba(z, q_gain, k_gain, batch, seq, l):
    m = z.shape[0]
    c0 = COL_SBA_QKV // STEP_WIDTH
    per_seg = WIDTH // STEP_WIDTH

    def head_spec(seg):
        return pl.BlockSpec((seq, STEP_WIDTH), lambda b, h: (b, c0 + seg * per_seg + h))

    gain_spec = pl.BlockSpec((None, 1, HEAD_DIM), lambda b, h: (l, 0, 0))
    tok = pltpu.VMEM((seq, STEP_WIDTH), bf16)
    return pl.pallas_call(
        _sba_kernel,
        grid=(batch, per_seg),
        in_specs=[head_spec(0), head_spec(1), head_spec(2), gain_spec, gain_spec],
        out_specs=pl.BlockSpec((seq, STEP_WIDTH), lambda b, h: (b, h)),
        out_shape=jax.ShapeDtypeStruct((m, WIDTH), bf16),
        scratch_shapes=[tok, tok],
        compiler_params=_params("arbitrary", "arbitrary"),
        name="sba",
    )(z, z, z, q_gain, k_gain)


XPAD = BF16_ROWS


def _bmm(a, b):
    return jnp.einsum('bij,bjk->bik', a, b, preferred_element_type=f32)


def _bmm_nt(a, b):
    return jnp.einsum('bid,bjd->bij', a, b, preferred_element_type=f32)


def _bmm_tn(a, b):
    return jnp.einsum('bck,bcv->bkv', a, b, preferred_element_type=f32)


def _unit_lower_inverse(l_mat, row, col):
    def same_block(d):
        return _block_id(row, d) == _block_id(col, d)

    eye = (row == col).astype(f32)
    l8 = jnp.where(same_block(8), l_mat, 0.0)
    p = l8.astype(bf16)
    x = (eye - l8).astype(bf16)
    p2 = _bmm(p, p)
    x = _bmm(x, (eye + p2).astype(bf16)).astype(bf16)
    p2 = p2.astype(bf16)
    p4 = _bmm(p2, p2)
    x = _bmm(x, (eye + p4).astype(bf16)).astype(bf16)
    d = 8
    while d < CHUNK:
        off = jnp.where(same_block(2 * d) & jnp.logical_not(same_block(d)), l_mat, 0.0).astype(bf16)
        x = _bmm(x, (eye - _bmm(off, x)).astype(bf16)).astype(bf16)
        d *= 2
    return x


SCAN_ROWS = HEAD_DIM + CHUNK


def _gdn_kernel(zq_ref, zk_ref, zv_ref, zg_ref, gb_ref, cwq_ref, cwk_ref, cwv_ref, ng_ref, o_ref,
                xq_ref, xk_ref, xv_ref, stage_ref, lin_ref, off_ref, eg_ref):
    t = zq_ref.shape[0]
    head0 = pl.program_id(1) * HEADS_PER_STEP
    grp = GDN_GROUP
    nc = grp // CHUNK
    head_cols = [slice(j * HEAD_DIM, (j + 1) * HEAD_DIM) for j in range(HEADS_PER_STEP)]

    zeros_pad = jnp.zeros((XPAD, STEP_WIDTH), bf16)
    for src, dst in ((zq_ref, xq_ref), (zk_ref, xk_ref), (zv_ref, xv_ref)):
        dst[pl.ds(0, XPAD), :] = zeros_pad
        dst[pl.ds(XPAD, t), :] = src[...]

    row = lax.broadcasted_iota(jnp.int32, (CHUNK, CHUNK), 0)
    col = lax.broadcasted_iota(jnp.int32, (CHUNK, CHUNK), 1)
    incl = row >= col
    strict = row > col
    tri_ones = jnp.concatenate([incl.astype(bf16), jnp.ones((CHUNK, CHUNK), bf16)], axis=0)
    tri_ones = jnp.broadcast_to(tri_ones, (nc, 2 * CHUNK, CHUNK))
    sel_src = jnp.bitwise_and(lax.broadcasted_iota(jnp.int32, (3 * LANES, STEP_WIDTH), 0), LANES - 1)
    sel_head = head0 + _block_id(lax.broadcasted_iota(jnp.int32, (3 * LANES, STEP_WIDTH), 1), HEAD_DIM)
    pick_g = (sel_src == sel_head).astype(bf16)
    pick_b = (sel_src == sel_head + N_HEADS).astype(bf16)
    sel_rows = (jnp.bitwise_and(lax.broadcasted_iota(jnp.int32, (SUBLANES, 3 * LANES), 1), LANES - 1)
                == head0 + lax.broadcasted_iota(jnp.int32, (SUBLANES, 3 * LANES), 0)).astype(bf16)
    sel_rows = jnp.broadcast_to(sel_rows, (nc, SUBLANES, 3 * LANES))

    def conv_silu(x_ref, cw_ref, r0, hc, slot):
        stage = stage_ref.at[slot]
        stage[...] = x_ref[pl.ds(r0, grp + XPAD), hc].astype(f32)
        acc = cw_ref[pl.ds(CONV_WIDTH - 1, 1), hc] * stage[pl.ds(XPAD, grp), :]
        for i in range(CONV_WIDTH - 1):
            acc += cw_ref[pl.ds(i, 1), hc] * stage[pl.ds(XPAD - (CONV_WIDTH - 1) + i, grp), :]
        return _silu(acc)

    def l2n(x):
        return x * lax.rsqrt(jnp.sum(x * x, axis=-1, keepdims=True) + EPS)

    def chunked(x):
        return x.reshape(nc, CHUNK, x.shape[-1])

    def slot_of(g):
        return jnp.bitwise_and(g, 1)

    def prepare(it, carry):
        r0 = pl.multiple_of(it * grp, grp)
        gb3 = _split3(gb_ref[pl.ds(r0, grp), :])
        ct = _bmm(tri_ones, chunked(gb3))
        cum3 = _split3(_sum3(ct[:, :CHUNK], LANES).reshape(grp, LANES))
        tot3 = _split3(_sum3(ct[:, CHUNK:], LANES).reshape(grp, LANES))
        gc_rows = _bmm_nt(sel_rows, chunked(cum3))

        gc_all = _dot(cum3, pick_g)
        g_tot_all = _dot(tot3, pick_g)
        beta_all = _dot(gb3, pick_b)

        k16s, kb16s, q16s, rhs16s, diffs, qds, kd16s = [], [], [], [], [], [], []
        for j, hc in enumerate(head_cols):
            gc, g_tot, beta = gc_all[:, hc], g_tot_all[:, hc], beta_all[:, hc]
            q = l2n(conv_silu(xq_ref, cwq_ref, r0, hc, 3 * j)) * np.float32(HEAD_DIM ** -0.5)
            k = l2n(conv_silu(xk_ref, cwk_ref, r0, hc, 3 * j + 1))
            v = conv_silu(xv_ref, cwv_ref, r0, hc, 3 * j + 2)
            kb = k * beta
            e_gc = jnp.exp(gc)
            k16s.append(chunked(k.astype(bf16)))
            kb16s.append(chunked(kb.astype(bf16)))
            q16s.append(chunked(q.astype(bf16)))
            rhs16s.append(chunked(jnp.concatenate([v * beta, kb * e_gc], axis=-1).astype(bf16)))
            diffs.append(chunked(gc)[:, :, :CHUNK] - gc_rows[:, j:j + 1, :])
            qds.append(chunked(q * e_gc))
            kd16s.append(chunked((k * jnp.exp(g_tot - gc)).astype(bf16)))
            e0 = pl.multiple_of(slot_of(it) * (nc * SUBLANES), nc * SUBLANES)
            eg_ref[pl.ds(e0, nc * SUBLANES), hc] = (
                chunked(jnp.exp(g_tot))[:, :SUBLANES, :].reshape(nc * SUBLANES, HEAD_DIM))

        cat = lambda xs: jnp.concatenate(xs, axis=0)
        k16, diff = cat(k16s), cat(diffs)
        decay = jnp.where(incl, jnp.exp(jnp.where(incl, diff, 0.0)), 0.0)
        l_mat = jnp.where(strict, _bmm_nt(cat(kb16s), k16) * decay, 0.0)
        intra = (_bmm_nt(cat(q16s), k16) * decay).astype(bf16)
        t_inv = _unit_lower_inverse(l_mat, row, col)
        uw = _bmm(t_inv, cat(rhs16s)).astype(bf16)
        state_map = _bmm_tn(cat(kd16s), uw)
        out_map = _bmm(intra, uw)
        q_eff = cat(qds) - out_map[:, :, HEAD_DIM:]
        lin = jnp.concatenate([state_map[:, :, HEAD_DIM:], q_eff], axis=1).astype(bf16)
        off = jnp.concatenate([state_map[:, :, :HEAD_DIM], out_map[:, :, :HEAD_DIM]], axis=1)
        s0 = pl.multiple_of(slot_of(it) * (nc * SCAN_ROWS), nc * SCAN_ROWS)
        for j, hc in enumerate(head_cols):
            mine = slice(j * nc, (j + 1) * nc)
            lin_ref[pl.ds(s0, nc * SCAN_ROWS), hc] = lin[mine].reshape(nc * SCAN_ROWS, HEAD_DIM)
            off_ref[pl.ds(s0, nc * SCAN_ROWS), hc] = off[mine].reshape(nc * SCAN_ROWS, HEAD_DIM)
        return carry

    def scan(slot_chunk, out_chunk, states):
        s0 = pl.multiple_of(slot_chunk * SCAN_ROWS, SCAN_ROWS)
        e0 = pl.multiple_of(slot_chunk * SUBLANES, SUBLANES)
        r0 = pl.multiple_of(out_chunk * CHUNK, CHUNK)
        new_states = []
        for j, hc in enumerate(head_cols):
            state = states[j]
            prod = _dot(lin_ref[pl.ds(s0, SCAN_ROWS), hc], state.astype(bf16))
            off = off_ref[pl.ds(s0, SCAN_ROWS), hc]
            new_states.append(state * eg_ref[pl.ds(e0, 1), hc] + (off[:HEAD_DIM] - prod[:HEAD_DIM]))
            o = prod[HEAD_DIM:] + off[HEAD_DIM:]
            gate = zg_ref[pl.ds(r0, CHUNK), hc].astype(f32)
            o_ref[pl.ds(r0, CHUNK), hc] = (_rms_rows(o, ng_ref[...]) * _silu(gate)).astype(o_ref.dtype)
        return tuple(new_states)

    def scan_group(slot, out_group, states):
        for c in range(nc):
            states = scan(slot * nc + c, out_group * nc + c, states)
        return states

    def step(g, states):
        states = scan_group(slot_of(g + 1), jnp.maximum(g - 1, 0), states)
        prepare(g, 0)
        return states

    n_groups = t // grp
    lin_ref[pl.ds(nc * SCAN_ROWS, nc * SCAN_ROWS), :] = jnp.zeros((nc * SCAN_ROWS, STEP_WIDTH), bf16)
    off_ref[pl.ds(nc * SCAN_ROWS, nc * SCAN_ROWS), :] = jnp.zeros((nc * SCAN_ROWS, STEP_WIDTH), f32)
    eg_ref[pl.ds(nc * SUBLANES, nc * SUBLANES), :] = jnp.zeros((nc * SUBLANES, STEP_WIDTH), f32)
    zero_state = jnp.zeros((HEAD_DIM, HEAD_DIM), f32)
    states = lax.fori_loop(0, n_groups, step, (zero_state,) * HEADS_PER_STEP)
    scan_group(jnp.int32((n_groups - 1) % 2), jnp.int32(n_groups - 1), states)


def _gdn(z, gb, conv_w, norm_gain, batch, seq, l):
    m = z.shape[0]
    per_seg = WIDTH // STEP_WIDTH
    cq = COL_GDN_QKV // STEP_WIDTH
    cg = COL_GDN_GATE // STEP_WIDTH

    def head_spec(c):
        return pl.BlockSpec((seq, STEP_WIDTH), lambda b, h: (b, c + h))

    def conv_spec(seg):
        return pl.BlockSpec((None, CONV_WIDTH, STEP_WIDTH), lambda b, h: (l, 0, seg * per_seg + h))

    ring_chunks = 2 * GDN_GROUP // CHUNK
    padded = pltpu.VMEM((seq + XPAD, STEP_WIDTH), bf16)
    return pl.pallas_call(
        _gdn_kernel,
        grid=(batch, per_seg),
        in_specs=[
            head_spec(cq), head_spec(cq + per_seg), head_spec(cq + 2 * per_seg), head_spec(cg),
            pl.BlockSpec((seq, LANES), lambda b, h: (b, 0)),
            conv_spec(0), conv_spec(1), conv_spec(2),
            pl.BlockSpec((None, 1, HEAD_DIM), lambda b, h: (l, 0, 0)),
        ],
        out_specs=pl.BlockSpec((seq, STEP_WIDTH), lambda b, h: (b, h)),
        out_shape=jax.ShapeDtypeStruct((m, WIDTH), bf16),
        scratch_shapes=[
            padded, padded, padded,
            pltpu.VMEM((3 * HEADS_PER_STEP, GDN_GROUP + XPAD, HEAD_DIM), f32),
            pltpu.VMEM((ring_chunks * SCAN_ROWS, STEP_WIDTH), bf16),
            pltpu.VMEM((ring_chunks * SCAN_ROWS, STEP_WIDTH), f32),
            pltpu.VMEM((ring_chunks * SUBLANES, STEP_WIDTH), f32),
        ],
        compiler_params=_params("arbitrary", "arbitrary"),
        name="gdn",
    )(z, z, z, z, gb, conv_w, conv_w, conv_w, norm_gain)


def _rows(v):
    return v.astype(f32)[:, None, :]


def _pad_rows(v):
    return _rows(jnp.pad(v, ((0, 0), (0, LANES - v.shape[1]))))


def kernel(x, w_in, conv_w, a_log, dt_bias, gdn_norm_g, gmlp_ln_g, w_spatial, b_spatial, sba_q_g, sba_k_g, w_out_a, w_out_b, w_out_c, w_out, norm_mix_g, norm_mlp_g, w_ff1, w_ff2):
    batch, seq, d = x.shape
    depth = w_in.shape[0]
    m = batch * seq

    wt = jnp.swapaxes(w_in, 1, 2).reshape(depth * D_IN, d)
    b_rep = jnp.broadcast_to(b_spatial[..., None], b_spatial.shape + (LANES,)).astype(f32)
    mix_g, mlp_g, gdn_g, ln_g = _rows(norm_mix_g), _rows(norm_mlp_g), _rows(gdn_norm_g), _rows(gmlp_ln_g)
    q_g, k_g = _rows(sba_q_g), _rows(sba_k_g)
    alog_rows, dt_rows = _pad_rows(a_log), _pad_rows(dt_bias)
    conv_w = conv_w.astype(f32)
    w_spatial = w_spatial.astype(f32)

    xf = x.reshape(m, d).astype(f32)
    for l in range(depth):
        z, gb = _inproj(xf, mix_g, wt, alog_rows, dt_rows, l)
        oa = _gdn(z, gb, conv_w, gdn_g, batch, seq, l)
        ob = _gmlp(z, ln_g, w_spatial, b_rep, l)
        oc = _sba(z, q_g, k_g, batch, seq, l)
        y = _merge(oa, ob, oc, w_out_a, w_out_b, w_out_c, z, l)
        xf = _matmul_res(y, w_out, xf, l)
        h1 = _ffn_up(xf, mlp_g, w_ff1, l)
        xf = _matmul_res(h1, w_ff2, xf, l)
    return xf.reshape(batch, seq, d).astype(x.dtype)
```
